```python
import jax, jax.numpy as jnp
from jax import lax
import numpy as np

D_MODEL = 1024
BATCH = 4
SEQ = 4096
DEPTH = 1

CONV_WIDTH = 512
CONV_K = 3
RET_HEADS = 8
RET_DK = 64
RET_DV = 128
RET_CHUNK = 128
QK_WIDTH = RET_HEADS * RET_DK
V_WIDTH = RET_HEADS * RET_DV
ROPE_BASE = 10000.0
IN_SPLITS = (CONV_WIDTH, CONV_WIDTH, CONV_WIDTH, QK_WIDTH, QK_WIDTH, V_WIDTH, V_WIDTH, D_MODEL, D_MODEL)
IN_WIDTH = sum(IN_SPLITS)
SPLIT_POINTS = [sum(IN_SPLITS[:i + 1]) for i in range(len(IN_SPLITS) - 1)]
MEM_LEN = 256
XA_HEADS = 4
XA_HEAD_DIM = D_MODEL // XA_HEADS
N_GROUPS = 4
EXPERTS_PER_GROUP = 8
N_EXPERTS = N_GROUPS * EXPERTS_PER_GROUP
TOP_K = 2
EXPERT_HIDDEN = D_MODEL // 2
EXPERT_BLOCK = 128
EPS = 1e-6

kernel_name = "hybrid_conv_retention_xattn_hmoe"


def rmsnorm(x, g):
    xf = x.astype(jnp.float32)
    y = xf * lax.rsqrt(jnp.mean(xf * xf, axis=-1, keepdims=True) + EPS)
    return (y * g.astype(jnp.float32)).astype(x.dtype)


def short_conv_branch(xin, bg, cg, conv_w, w_out):
    u = cg * xin
    rhs = conv_w[:, None, :].astype(u.dtype)
    c = lax.conv_general_dilated(u, rhs, window_strides=(1,), padding=((CONV_K - 1, 0),),
                                 dimension_numbers=('NWC', 'WIO', 'NWC'),
                                 feature_group_count=CONV_WIDTH)
    return (bg * c) @ w_out


def rotary(t, cos, sin):
    t1, t2 = jnp.split(t.astype(jnp.float32), 2, axis=-1)
    return jnp.concatenate([t1 * cos - t2 * sin, t1 * sin + t2 * cos], axis=-1).astype(t.dtype)


def retention_branch(q, k, v, g, w_out):
    b, s, _ = q.shape
    n = s // RET_CHUNK
    pos = jnp.arange(s, dtype=jnp.float32)
    inv_freq = ROPE_BASE ** (-jnp.arange(0, RET_DK, 2, dtype=jnp.float32) / RET_DK)
    ang = pos[:, None] * inv_freq[None, :]
    cos, sin = jnp.cos(ang)[:, None, :], jnp.sin(ang)[:, None, :]
    q = rotary(q.reshape(b, s, RET_HEADS, RET_DK), cos, sin)
    k = rotary(k.reshape(b, s, RET_HEADS, RET_DK), cos, sin) * (RET_DK ** -0.5)
    v = v.reshape(b, s, RET_HEADS, RET_DV)

    def to_chunks(t, d):
        return t.reshape(b, n, RET_CHUNK, RET_HEADS, d).transpose(0, 3, 1, 2, 4)

    qc, kc, vc = to_chunks(q, RET_DK), to_chunks(k, RET_DK), to_chunks(v, RET_DV)

    log_g = jnp.log(1.0 - jnp.power(2.0, -5.0 - jnp.arange(RET_HEADS, dtype=jnp.float32)))
    idx = jnp.arange(RET_CHUNK, dtype=jnp.float32)
    diff = idx[:, None] - idx[None, :]
    decay = jnp.where(diff >= 0, jnp.exp(jnp.maximum(diff, 0.0)[None] * log_g[:, None, None]), 0.0)
    zeta = jnp.exp((RET_CHUNK - 1 - idx)[None, :] * log_g[:, None])
    xi = jnp.exp((idx + 1)[None, :] * log_g[:, None])
    chunk_decay = jnp.exp(RET_CHUNK * log_g)

    scores = jnp.einsum('bhnqd,bhnkd->bhnqk', qc, kc) * decay[None, :, None]
    inner = jnp.einsum('bhnqk,bhnke->bhnqe', scores, vc)
    kv = jnp.einsum('bhnkd,bhnke->nbhde', kc * zeta[None, :, None, :, None], vc)

    def step(state, kv_n):
        return chunk_decay[None, :, None, None] * state + kv_n, state

    _, state_prev = lax.scan(step, jnp.zeros_like(kv[0]), kv)
    cross = jnp.einsum('bhnqd,nbhde->bhnqe', qc, state_prev) * xi[None, :, None, :, None]

    o = (inner + cross).astype(jnp.float32).transpose(0, 2, 3, 1, 4).reshape(b, s, RET_HEADS, RET_DV)
    mu = jnp.mean(o, axis=-1, keepdims=True)
    var = jnp.mean(jnp.square(o - mu), axis=-1, keepdims=True)
    o = ((o - mu) * lax.rsqrt(var + EPS)).reshape(b, s, V_WIDTH).astype(g.dtype)
    return (jax.nn.silu(g) * o) @ w_out


def cross_attn(hn, mn, w_q, w_kv, w_o):
    b, s, d = hn.shape
    m = mn.shape[1]
    q = (hn @ w_q).reshape(b, s, XA_HEADS, XA_HEAD_DIM)
    k, v = jnp.split(mn @ w_kv, 2, axis=-1)
    k = k.reshape(b, m, XA_HEADS, XA_HEAD_DIM)
    v = v.reshape(b, m, XA_HEADS, XA_HEAD_DIM)
    scores = jnp.einsum('bshd,bmhd->bhsm', q, k).astype(jnp.float32) * (XA_HEAD_DIM ** -0.5)
    p = jax.nn.softmax(scores, axis=-1).astype(v.dtype)
    o = jnp.einsum('bhsm,bmhd->bshd', p, v).reshape(b, s, d)
    return o @ w_o


def hier_moe(xn, w_group, b_group, w_router, b_router, w_gate, w_up, w_down):
    b, s, d = xn.shape
    t = b * s
    xf = xn.reshape(t, d)
    grp_prob = jax.nn.softmax((xf @ w_group + b_group).astype(jnp.float32), axis=-1)
    p_g, g_idx = lax.top_k(grp_prob, 1)
    exp_logits = (xf @ w_router + b_router).astype(jnp.float32).reshape(t, N_GROUPS, EXPERTS_PER_GROUP)
    sel = jnp.take_along_axis(exp_logits, g_idx[:, :, None], axis=1)[:, 0]
    p_e, e_local = lax.top_k(jax.nn.softmax(sel, axis=-1), TOP_K)
    weights = p_g * p_e / jnp.sum(p_e, axis=-1, keepdims=True)
    expert_id = g_idx * EXPERTS_PER_GROUP + e_local

    m = t * TOP_K
    flat_e = expert_id.reshape(-1)
    flat_w = weights.reshape(-1)
    flat_tok = jnp.repeat(jnp.arange(t, dtype=jnp.int32), TOP_K)
    order = jnp.argsort(flat_e)
    sorted_e = flat_e[order]
    counts = jnp.bincount(flat_e, length=N_EXPERTS)
    starts = jnp.cumsum(counts) - counts
    padded = (counts + EXPERT_BLOCK - 1) // EXPERT_BLOCK * EXPERT_BLOCK
    padded_ends = jnp.cumsum(padded)
    padded_starts = padded_ends - padded
    dest = padded_starts[sorted_e] + jnp.arange(m) - starts[sorted_e]
    n_rows = -(-(m + N_EXPERTS * (EXPERT_BLOCK - 1)) // EXPERT_BLOCK) * EXPERT_BLOCK
    n_blocks = n_rows // EXPERT_BLOCK
    row_tok = jnp.zeros((n_rows,), jnp.int32).at[dest].set(flat_tok[order])
    row_w = jnp.zeros((n_rows,), jnp.float32).at[dest].set(flat_w[order])
    block_e = jnp.minimum(jnp.searchsorted(padded_ends, jnp.arange(n_blocks) * EXPERT_BLOCK, side='right'),
                          N_EXPERTS - 1)
    xs = xf[row_tok].reshape(n_blocks, EXPERT_BLOCK, d)

    def expert_block(args):
        xb, e = args
        hid = jax.nn.silu(xb @ w_gate[e]) * (xb @ w_up[e])
        return hid @ w_down[e]

    ys = lax.map(expert_block, (xs, block_e)).reshape(n_rows, d)
    out = jnp.zeros((t, d), xn.dtype).at[row_tok].add((ys * row_w[:, None]).astype(xn.dtype))
    return out.reshape(b, s, d)


def _w(key, shape, fan_in):
    return jax.random.normal(key, shape, jnp.float32) * (fan_in ** -0.5)


def _gain(key, shape):
    return 1.0 + 0.01 * jax.random.normal(key, shape, jnp.float32)


def setup_inputs(seed: int = 0) -> dict:
    key = jax.random.key(seed)
    ks = jax.random.split(key, 22)
    L, D = DEPTH, D_MODEL
    return {
        "x": jax.random.normal(ks[0], (BATCH, SEQ, D), jnp.float32),
        "mem": jax.random.normal(ks[1], (BATCH, MEM_LEN, D), jnp.float32),
        "mix_norm_g": _gain(ks[2], (L, D)),
        "w_in": _w(ks[3], (L, D, IN_WIDTH), D),
        "conv_w": _w(ks[4], (L, CONV_K, CONV_WIDTH), CONV_K),
        "w_conv_out": _w(ks[5], (L, CONV_WIDTH, D), CONV_WIDTH),
        "w_ret_out": _w(ks[6], (L, V_WIDTH, D), V_WIDTH),
        "w_mix_out": _w(ks[7], (L, D, D), D),
        "xa_norm_g": _gain(ks[8], (L, D)),
        "mem_norm_g": _gain(ks[9], (L, D)),
        "w_xa_q": _w(ks[10], (L, D, D), D),
        "w_xa_kv": _w(ks[11], (L, D, 2 * D), D),
        "w_xa_o": _w(ks[12], (L, D, D), D),
        "moe_norm_g": _gain(ks[13], (L, D)),
        "w_group": _w(ks[14], (L, D, N_GROUPS), D),
        "b_group": 0.01 * jax.random.normal(ks[15], (L, N_GROUPS), jnp.float32),
        "w_router": _w(ks[16], (L, D, N_EXPERTS), D),
        "b_router": 0.01 * jax.random.normal(ks[17], (L, N_EXPERTS), jnp.float32),
        "w_gate": _w(ks[18], (L, N_EXPERTS, D, EXPERT_HIDDEN), D),
        "w_up": _w(ks[19], (L, N_EXPERTS, D, EXPERT_HIDDEN), D),
        "w_down": _w(ks[20], (L, N_EXPERTS, EXPERT_HIDDEN, D), EXPERT_HIDDEN),
        "final_norm_g": _gain(ks[21], (D,)),
    }


def reference(x, mem, mix_norm_g, w_in, conv_w, w_conv_out, w_ret_out, w_mix_out,
              xa_norm_g, mem_norm_g, w_xa_q, w_xa_kv, w_xa_o, moe_norm_g,
              w_group, b_group, w_router, b_router, w_gate, w_up, w_down, final_norm_g):
    h = x
    for l in range(DEPTH):
        xn = rmsnorm(h, mix_norm_g[l])
        xin, bg, cg, q, k, v, g, gate_c, gate_r = jnp.split(xn @ w_in[l], SPLIT_POINTS, axis=-1)
        y_conv = short_conv_branch(xin, bg, cg, conv_w[l], w_conv_out[l])
        y_ret = retention_branch(q, k, v, g, w_ret_out[l])
        merged = jax.nn.sigmoid(gate_c) * y_conv + jax.nn.sigmoid(gate_r) * y_ret
        h = h + merged @ w_mix_out[l]
        h = h + cross_attn(rmsnorm(h, xa_norm_g[l]), rmsnorm(mem, mem_norm_g[l]),
                           w_xa_q[l], w_xa_kv[l], w_xa_o[l])
        h = h + hier_moe(rmsnorm(h, moe_norm_g[l]), w_group[l], b_group[l], w_router[l], b_router[l],
                         w_gate[l], w_up[l], w_down[l])
    return rmsnorm(h, final_norm_g)
```

```python
import functools

import numpy as np
import jax
import jax.numpy as jnp
from jax import lax
from jax.experimental import pallas as pl
from jax.experimental.pallas import tpu as pltpu

F32 = jnp.float32
BF16 = jnp.bfloat16

D_MODEL = 1024
CONV_WIDTH = 512
CONV_K = 3
RET_HEADS = 8
RET_DK = 64
RET_DV = 128
RET_CHUNK = 128
QK_WIDTH = RET_HEADS * RET_DK
V_WIDTH = RET_HEADS * RET_DV
ROPE_BASE = 10000.0
IN_WIDTH = 3 * CONV_WIDTH + 2 * QK_WIDTH + 2 * V_WIDTH + 2 * D_MODEL
OFF_XIN = 0
OFF_BG = OFF_XIN + CONV_WIDTH
OFF_CG = OFF_BG + CONV_WIDTH
OFF_Q = OFF_CG + CONV_WIDTH
OFF_K = OFF_Q + QK_WIDTH
OFF_V = OFF_K + QK_WIDTH
OFF_G = OFF_V + V_WIDTH
OFF_GATE_C = OFF_G + V_WIDTH
OFF_GATE_R = OFF_GATE_C + D_MODEL
MEM_LEN = 256
XA_HEADS = 4
XA_HEAD_DIM = D_MODEL // XA_HEADS
N_GROUPS = 4
EXPERTS_PER_GROUP = 8
N_EXPERTS = N_GROUPS * EXPERTS_PER_GROUP
TOP_K = 2
EXPERT_HIDDEN = D_MODEL // 2
EPS = 1e-6

LANES = 128
SUBLANES = 8
VMEM_LIMIT = 56 * 1024 * 1024

IN_TM = 1024
IN_TN = 512
MIX_TS = 256
XA_TS = 512
ROUTE_ROWS = 40
ROUTE_E0 = 8
MOE_M = 256
DISP_T = 512
COMB_T = 256


def _rms(x, g):
    ms = jnp.mean(x * x, axis=-1, keepdims=True)
    return x * lax.rsqrt(ms + EPS) * g


def _sigmoid(x):
    return 1.0 / (1.0 + jnp.exp(-x))


def _dot(a, b):
    return jnp.dot(a, b, preferred_element_type=F32)


def _dot_nt(a, b):
    return lax.dot_general(a, b, (((1,), (1,)), ((), ())), preferred_element_type=F32)


def _dot_tn(a, b):
    return lax.dot_general(a, b, (((0,), (0,)), ((), ())), preferred_element_type=F32)


def _inproj_kernel(x_ref, g_ref, w_ref, o_ref, xn_ref):
    @pl.when(pl.program_id(1) == 0)
    def _():
        xn_ref[...] = _rms(x_ref[...], g_ref[...]).astype(BF16)

    o_ref[...] = _dot(xn_ref[...], w_ref[...]).astype(BF16)


def _in_proj(x2, g, w_bf):
    t = x2.shape[0]
    return pl.pallas_call(
        _inproj_kernel,
        out_shape=jax.ShapeDtypeStruct((t, IN_WIDTH), BF16),
        grid=(t // IN_TM, IN_WIDTH // IN_TN),
        in_specs=[
            pl.BlockSpec((IN_TM, D_MODEL), lambda i, j: (i, 0)),
            pl.BlockSpec((1, D_MODEL), lambda i, j: (0, 0)),
            pl.BlockSpec((D_MODEL, IN_TN), lambda i, j: (0, j)),
        ],
        out_specs=pl.BlockSpec((IN_TM, IN_TN), lambda i, j: (i, j)),
        scratch_shapes=[pltpu.VMEM((IN_TM, D_MODEL), BF16)],
        compiler_params=pltpu.CompilerParams(
            dimension_semantics=("arbitrary", "arbitrary"), vmem_limit_bytes=VMEM_LIMIT),
        name="in_proj",
    )(x2, g, w_bf)


def _retention_constants(seq):
    pos = np.arange(seq, dtype=np.float64)
    inv_freq = ROPE_BASE ** (-np.arange(0, RET_DK, 2, dtype=np.float64) / RET_DK)
    ang = pos[:, None] * inv_freq[None, :]
    cos, sin = np.cos(ang), np.sin(ang)
    cos_t = np.concatenate([cos, cos, cos, cos], axis=1)
    sin_t = np.concatenate([-sin, sin, -sin, sin], axis=1)
    log_g = np.log(1.0 - 2.0 ** (-5.0 - np.arange(RET_HEADS, dtype=np.float64)))
    idx = np.arange(RET_CHUNK, dtype=np.float64)
    diff = idx[:, None] - idx[None, :]
    decay = np.where(diff >= 0, np.exp(np.maximum(diff, 0.0)[None] * log_g[:, None, None]), 0.0)
    zeta = np.exp((RET_CHUNK - 1 - idx)[None, :] * log_g[:, None])
    xi = np.exp((idx + 1)[None, :] * log_g[:, None])
    zeta_t = np.repeat(zeta.T, RET_DK, axis=1)
    xi_t = np.repeat(xi.T, RET_DK, axis=1)
    chunk_decay = np.exp(RET_CHUNK * log_g)
    f = lambda a: jnp.asarray(a, dtype=F32)
    return f(cos_t), f(sin_t), f(xi_t), f(zeta_t), f(decay), [float(c) for c in chunk_decay]


def _mixer_kernel(chunk_decay, proj_ref, x_ref, cos_ref, sin_ref, xi_ref, zeta_ref, decay_ref,
                  convw_ref, wc_ref, wr_ref, wm_ref, o_ref, state_ref, tail_ref, yin_ref):
    ts = x_ref.shape[0]

    @pl.when(pl.program_id(1) == 0)
    def _():
        state_ref[...] = jnp.zeros_like(state_ref)
        tail_ref[...] = jnp.zeros_like(tail_ref)

    xin = proj_ref[:, OFF_XIN:OFF_XIN + CONV_WIDTH].astype(F32)
    bg = proj_ref[:, OFF_BG:OFF_BG + CONV_WIDTH].astype(F32)
    cg = proj_ref[:, OFF_CG:OFF_CG + CONV_WIDTH].astype(F32)
    u = cg * xin
    ue = jnp.concatenate([tail_ref[...], u], axis=0)
    u1 = pltpu.roll(ue, 1, 0)[SUBLANES:]
    u2 = pltpu.roll(ue, 2, 0)[SUBLANES:]
    tail_ref[...] = u[ts - SUBLANES:]
    c = convw_ref[2:3, :] * u + convw_ref[1:2, :] * u1 + convw_ref[0:1, :] * u2
    y_conv = _dot((bg * c).astype(BF16), wc_ref[...])

    lane = lax.broadcasted_iota(jnp.int32, (1, LANES), 1)
    low_half = (lane % RET_DK) < (RET_DK // 2)
    head_masks = [(lane // RET_DK) == j for j in range(LANES // RET_DK)]

    def rotary(t, cosv, sinv):
        outs = []
        for p in range(QK_WIDTH // LANES):
            tp = t[:, p * LANES:(p + 1) * LANES]
            fwd = pltpu.roll(tp, LANES - RET_DK // 2, 1)
            bwd = pltpu.roll(tp, RET_DK // 2, 1)
            outs.append(tp * cosv + jnp.where(low_half, fwd, bwd) * sinv)
        return jnp.concatenate(outs, axis=1)

    for ci in range(ts // RET_CHUNK):
        rows = slice(ci * RET_CHUNK, (ci + 1) * RET_CHUNK)
        cosv = cos_ref[rows, :]
        sinv = sin_ref[rows, :]
        qr = rotary(proj_ref[rows, OFF_Q:OFF_Q + QK_WIDTH].astype(F32), cosv, sinv)
        kr = rotary(proj_ref[rows, OFF_K:OFF_K + QK_WIDTH].astype(F32), cosv, sinv) * (RET_DK ** -0.5)
        qb = qr.astype(BF16)
        qx = (qr * xi_ref[...]).astype(BF16)
        kb = kr.astype(BF16)
        kz = (kr * zeta_ref[...]).astype(BF16)
        for h in range(RET_HEADS):
            lanes = slice((h // 2) * LANES, (h // 2 + 1) * LANES)
            m = head_masks[h % 2]
            zero = jnp.zeros((), BF16)
            qm = jnp.where(m, qb[:, lanes], zero)
            qxm = jnp.where(m, qx[:, lanes], zero)
            vh = proj_ref[rows, OFF_V + h * RET_DV:OFF_V + (h + 1) * RET_DV]
            scores = _dot_nt(qm, kb[:, lanes]) * decay_ref[h]
            st = state_ref[h]
            o = _dot(scores.astype(BF16), vh) + _dot(qxm, st.astype(BF16))
            state_ref[h] = chunk_decay[h] * st + _dot_tn(kz[:, lanes], vh)
            mu = jnp.mean(o, axis=-1, keepdims=True)
            dlt = o - mu
            var = jnp.mean(dlt * dlt, axis=-1, keepdims=True)
            g = proj_ref[rows, OFF_G + h * RET_DV:OFF_G + (h + 1) * RET_DV].astype(F32)
            yin_ref[rows, h * RET_DV:(h + 1) * RET_DV] = (
                g * _sigmoid(g) * (dlt * lax.rsqrt(var + EPS))).astype(BF16)

    y_ret = _dot(yin_ref[...], wr_ref[...])
    gate_c = proj_ref[:, OFF_GATE_C:OFF_GATE_C + D_MODEL].astype(F32)
    gate_r = proj_ref[:, OFF_GATE_R:OFF_GATE_R + D_MODEL].astype(F32)
    merged = _sigmoid(gate_c) * y_conv + _sigmoid(gate_r) * y_ret
    o_ref[...] = x_ref[...] + _dot(merged.astype(BF16), wm_ref[...])


def _mixer(proj, x2, conv_w, wc_bf, wr_bf, wm_bf, batch, seq):
    ts = MIX_TS
    ns = seq // ts
    cos_t, sin_t, xi_t, zeta_t, decay, chunk_decay = _retention_constants(seq)
    full = lambda shape: pl.BlockSpec(shape, lambda b, s: (0,) * len(shape))
    return pl.pallas_call(
        functools.partial(_mixer_kernel, chunk_decay),
        out_shape=jax.ShapeDtypeStruct((batch * seq, D_MODEL), F32),
        grid=(batch, ns),
        in_specs=[
            pl.BlockSpec((ts, IN_WIDTH), lambda b, s: (b * ns + s, 0)),
            pl.BlockSpec((ts, D_MODEL), lambda b, s: (b * ns + s, 0)),
            pl.BlockSpec((ts, LANES), lambda b, s: (s, 0)),
            pl.BlockSpec((ts, LANES), lambda b, s: (s, 0)),
            full((RET_CHUNK, QK_WIDTH)),
            full((RET_CHUNK, QK_WIDTH)),
            full((RET_HEADS, RET_CHUNK, RET_CHUNK)),
            full((CONV_K, CONV_WIDTH)),
            full((CONV_WIDTH, D_MODEL)),
            full((V_WIDTH, D_MODEL)),
            full((D_MODEL, D_MODEL)),
        ],
        out_specs=pl.BlockSpec((ts, D_MODEL), lambda b, s: (b * ns + s, 0)),
        scratch_shapes=[
            pltpu.VMEM((RET_HEADS, LANES, RET_DV), F32),
            pltpu.VMEM((SUBLANES, CONV_WIDTH), F32),
            pltpu.VMEM((ts, V_WIDTH), BF16),
        ],
        compiler_params=pltpu.CompilerParams(
            dimension_semantics=("arbitrary", "arbitrary"), vmem_limit_bytes=VMEM_LIMIT),
        name="mixer",
    )(proj, x2, cos_t, sin_t, xi_t, zeta_t, decay, conv_w, wc_bf, wr_bf, wm_bf)


def _memkv_kernel(m_ref, g_ref, w_ref, o_ref):
    o_ref[0] = _dot(_rms(m_ref[0], g_ref[...]).astype(BF16), w_ref[...]).astype(BF16)


def _mem_kv(mem, g, wkv_bf):
    b = mem.shape[0]
    return pl.pallas_call(
        _memkv_kernel,
        out_shape=jax.ShapeDtypeStruct((b, MEM_LEN, 2 * D_MODEL), BF16),
        grid=(b,),
        in_specs=[
            pl.BlockSpec((1, MEM_LEN, D_MODEL), lambda i: (i, 0, 0)),
            pl.BlockSpec((1, D_MODEL), lambda i: (0, 0)),
            pl.BlockSpec((D_MODEL, 2 * D_MODEL), lambda i: (0, 0)),
        ],
        out_specs=pl.BlockSpec((1, MEM_LEN, 2 * D_MODEL), lambda i: (i, 0, 0)),
        compiler_params=pltpu.CompilerParams(
            dimension_semantics=("arbitrary",), vmem_limit_bytes=VMEM_LIMIT),
        name="mem_kv",
    )(mem, g, wkv_bf)


def _xattn_kernel(h_ref, kv_ref, gx_ref, wq_ref, wo_ref, gm_ref, rwh_ref, rwl_ref, rb_ref,
                  h2_ref, xn2_ref, rid_ref, rwt_ref):
    ts = h_ref.shape[0]
    h = h_ref[...]
    q = _dot(_rms(h, gx_ref[...]).astype(BF16), wq_ref[...])
    outs = []
    for hd in range(XA_HEADS):
        cols = slice(hd * XA_HEAD_DIM, (hd + 1) * XA_HEAD_DIM)
        kh = kv_ref[0, :, cols]
        vh = kv_ref[0, :, D_MODEL + hd * XA_HEAD_DIM:D_MODEL + (hd + 1) * XA_HEAD_DIM]
        s = _dot_nt(q[:, cols].astype(BF16), kh) * (XA_HEAD_DIM ** -0.5)
        e = jnp.exp(s - jnp.max(s, axis=-1, keepdims=True))
        p = e / jnp.sum(e, axis=-1, keepdims=True)
        outs.append(_dot(p.astype(BF16), vh))
    h2 = h + _dot(jnp.concatenate(outs, axis=1).astype(BF16), wo_ref[...])
    h2_ref[...] = h2
    xn = _rms(h2, gm_ref[...])
    xn2_ref[...] = xn

    hi = xn.astype(BF16)
    lo = (xn - hi.astype(F32)).astype(BF16)
    lg = (_dot_nt(rwh_ref[...], hi) + _dot_nt(rwh_ref[...], lo) + _dot_nt(rwl_ref[...], hi)
          + rb_ref[:, 0:1])

    gl = lg[0:N_GROUPS]
    ge = jnp.exp(gl - jnp.max(gl, axis=0, keepdims=True))
    gp = ge / jnp.sum(ge, axis=0, keepdims=True)
    p_g = gp[0:1]
    g_idx = jnp.zeros((1, ts), jnp.int32)
    for i in range(1, N_GROUPS):
        better = gp[i:i + 1] > p_g
        g_idx = jnp.where(better, i, g_idx)
        p_g = jnp.where(better, gp[i:i + 1], p_g)
    sel = jnp.zeros((EXPERTS_PER_GROUP, ts), F32)
    for i in range(N_GROUPS):
        r0 = ROUTE_E0 + i * EXPERTS_PER_GROUP
        sel = jnp.where(g_idx == i, lg[r0:r0 + EXPERTS_PER_GROUP], sel)
    se = jnp.exp(sel - jnp.max(sel, axis=0, keepdims=True))
    sp = se / jnp.sum(se, axis=0, keepdims=True)
    ridx = lax.broadcasted_iota(jnp.int32, (EXPERTS_PER_GROUP, ts), 0)
    m1 = jnp.max(sp, axis=0, keepdims=True)
    i1 = jnp.min(jnp.where(sp == m1, ridx, EXPERTS_PER_GROUP), axis=0, keepdims=True)
    sp2 = jnp.where(ridx == i1, -1.0, sp)
    m2 = jnp.max(sp2, axis=0, keepdims=True)
    i2 = jnp.min(jnp.where(sp2 == m2, ridx, EXPERTS_PER_GROUP), axis=0, keepdims=True)
    den = m1 + m2
    rid_ref[...] = jnp.zeros_like(rid_ref)
    rwt_ref[...] = jnp.zeros_like(rwt_ref)
    rid_ref[0:1, :] = g_idx * EXPERTS_PER_GROUP + i1
    rid_ref[1:2, :] = g_idx * EXPERTS_PER_GROUP + i2
    rwt_ref[0:1, :] = p_g * m1 / den
    rwt_ref[1:2, :] = p_g * m2 / den


def _xattn(h1, kv, gx, wq_bf, wo_bf, gm, rwh, rwl, rb, batch, seq):
    ts = XA_TS
    ns = seq // ts
    t = batch * seq
    full = lambda shape: pl.BlockSpec(shape, lambda b, s: (0,) * len(shape))
    return pl.pallas_call(
        _xattn_kernel,
        out_shape=(
            jax.ShapeDtypeStruct((t, D_MODEL), F32),
            jax.ShapeDtypeStruct((t, D_MODEL), F32),
            jax.ShapeDtypeStruct((SUBLANES, t), jnp.int32),
            jax.ShapeDtypeStruct((SUBLANES, t), F32),
        ),
        grid=(batch, ns),
        in_specs=[
            pl.BlockSpec((ts, D_MODEL), lambda b, s: (b * ns + s, 0)),
            pl.BlockSpec((1, MEM_LEN, 2 * D_MODEL), lambda b, s: (b, 0, 0)),
            full((1, D_MODEL)),
            full((D_MODEL, D_MODEL)),
            full((D_MODEL, D_MODEL)),
            full((1, D_MODEL)),
            full((ROUTE_ROWS, D_MODEL)),
            full((ROUTE_ROWS, D_MODEL)),
            full((ROUTE_ROWS, LANES)),
        ],
        out_specs=(
            pl.BlockSpec((ts, D_MODEL), lambda b, s: (b * ns + s, 0)),
            pl.BlockSpec((ts, D_MODEL), lambda b, s: (b * ns + s, 0)),
            pl.BlockSpec((SUBLANES, ts), lambda b, s: (0, b * ns + s)),
            pl.BlockSpec((SUBLANES, ts), lambda b, s: (0, b * ns + s)),
        ),
        compiler_params=pltpu.CompilerParams(
            dimension_semantics=("arbitrary", "arbitrary"), vmem_limit_bytes=VMEM_LIMIT),
        name="xattn",
    )(h1, kv, gx, wq_bf, wo_bf, gm, rwh, rwl, rb)


def _row_copy(src_ref, src_row, dst_ref, dst_row, sem):
    return pltpu.make_async_copy(src_ref.at[pl.ds(src_row, 1)], dst_ref.at[pl.ds(dst_row, 1)], sem)


def _dispatch_kernel(last_ref, has_ref, nu_ref, pos_ref, xn_ref, xs_ref, zero_ref, sem, zsem):
    td = xn_ref.shape[0]
    n_blocks = xs_ref.shape[0] // MOE_M

    @pl.when(pl.program_id(0) == 0)
    def _():
        zero_ref[...] = jnp.zeros_like(zero_ref)

        def fill(start):
            start = pl.multiple_of(start, MOE_M)
            return pltpu.make_async_copy(zero_ref, xs_ref.at[pl.ds(start, MOE_M)], zsem)

        for e in range(N_EXPERTS):
            @pl.when(has_ref[e] > 0)
            def _():
                fill(last_ref[e]).start()
        for e in range(N_EXPERTS):
            @pl.when(has_ref[e] > 0)
            def _():
                fill(last_ref[e]).wait()

        def fill_tail(b, carry):
            fill(b * MOE_M).start()
            fill(b * MOE_M).wait()
            return carry

        lax.fori_loop(nu_ref[0], n_blocks, fill_tail, 0)

    def start(r, carry):
        for k in range(TOP_K):
            _row_copy(xn_ref, r, xs_ref, pos_ref[0, k, r], sem).start()
        return carry

    def wait(r, carry):
        for k in range(TOP_K):
            _row_copy(xn_ref, 0, xs_ref, 0, sem).wait()
        return carry

    lax.fori_loop(0, td, start, 0, unroll=8)
    lax.fori_loop(0, td, wait, 0, unroll=8)


def _dispatch(last_blk, has, n_used, pos3, xn2, n_rows):
    t = xn2.shape[0]
    td = DISP_T
    return pl.pallas_call(
        _dispatch_kernel,
        out_shape=jax.ShapeDtypeStruct((n_rows, D_MODEL), F32),
        grid_spec=pltpu.PrefetchScalarGridSpec(
            num_scalar_prefetch=3,
            grid=(t // td,),
            in_specs=[
                pl.BlockSpec((1, TOP_K, td), lambda i, *_: (i, 0, 0), memory_space=pltpu.SMEM),
                pl.BlockSpec((td, D_MODEL), lambda i, *_: (i, 0)),
            ],
            out_specs=pl.BlockSpec(memory_space=pl.ANY),
            scratch_shapes=[
                pltpu.VMEM((MOE_M, D_MODEL), F32),
                pltpu.SemaphoreType.DMA,
                pltpu.SemaphoreType.DMA,
            ],
        ),
        compiler_params=pltpu.CompilerParams(
            dimension_semantics=("arbitrary",), vmem_limit_bytes=VMEM_LIMIT),
        name="dispatch",
    )(last_blk, has, n_used, pos3, xn2)


def _expert_kernel(be_ref, nu_ref, xs_ref, wg_ref, wu_ref, wd_ref, ys_ref):
    @pl.when(pl.program_id(0) < nu_ref[0])
    def _():
        x = xs_ref[...].astype(BF16)
        g = _dot(x, wg_ref[0])
        u = _dot(x, wu_ref[0])
        ys_ref[...] = _dot((g * _sigmoid(g) * u).astype(BF16), wd_ref[0])

    @pl.when(pl.program_id(0) >= nu_ref[0])
    def _():
        ys_ref[...] = jnp.zeros_like(ys_ref)


def _experts(block_e, n_used, xs, wg_bf, wu_bf, wd_bf):
    n_rows = xs.shape[0]
    row_map = lambda i, be, nu: (jnp.minimum(i, nu[0] - 1), 0)
    out_map = lambda i, be, nu: (i, 0)
    w_map = lambda i, be, nu: (be[i], 0, 0)
    return pl.pallas_call(
        _expert_kernel,
        out_shape=jax.ShapeDtypeStruct((n_rows, D_MODEL), F32),
        grid_spec=pltpu.PrefetchScalarGridSpec(
            num_scalar_prefetch=2,
            grid=(n_rows // MOE_M,),
            in_specs=[
                pl.BlockSpec((MOE_M, D_MODEL), row_map),
                pl.BlockSpec((1, D_MODEL, EXPERT_HIDDEN), w_map),
                pl.BlockSpec((1, D_MODEL, EXPERT_HIDDEN), w_map),
                pl.BlockSpec((1, EXPERT_HIDDEN, D_MODEL), w_map),
            ],
            out_specs=pl.BlockSpec((MOE_M, D_MODEL), out_map),
        ),
        compiler_params=pltpu.CompilerParams(
            dimension_semantics=("arbitrary",), vmem_limit_bytes=VMEM_LIMIT),
        name="experts",
    )(block_e, n_used, xs, wg_bf, wu_bf, wd_bf)


def _combine_kernel(final_norm, pos_ref, h2_ref, wt_ref, g_ref, ys_ref, o_ref, buf_ref, sem):
    tc = h2_ref.shape[0]

    def start(r, carry):
        for k in range(TOP_K):
            _row_copy(ys_ref, pos_ref[0, k, r], buf_ref.at[k], r, sem).start()
        return carry

    def wait(r, carry):
        for k in range(TOP_K):
            _row_copy(ys_ref, 0, buf_ref.at[k], 0, sem).wait()
        return carry

    lax.fori_loop(0, tc, start, 0, unroll=8)
    lax.fori_loop(0, tc, wait, 0, unroll=8)
    y = h2_ref[...] + wt_ref[:, 0:1] * buf_ref[0] + wt_ref[:, 1:2] * buf_ref[1]
    o_ref[...] = _rms(y, g_ref[...]) if final_norm else y


def _combine(pos3, h2, wt, g, ys, final_norm):
    t = h2.shape[0]
    tc = COMB_T
    return pl.pallas_call(
        functools.partial(_combine_kernel, final_norm),
        out_shape=jax.ShapeDtypeStruct((t, D_MODEL), F32),
        grid=(t // tc,),
        in_specs=[
            pl.BlockSpec((1, TOP_K, tc), lambda i: (i, 0, 0), memory_space=pltpu.SMEM),
            pl.BlockSpec((tc, D_MODEL), lambda i: (i, 0)),
            pl.BlockSpec((tc, TOP_K), lambda i: (i, 0)),
            pl.BlockSpec((1, D_MODEL), lambda i: (0, 0)),
            pl.BlockSpec(memory_space=pl.ANY),
        ],
        out_specs=pl.BlockSpec((tc, D_MODEL), lambda i: (i, 0)),
        scratch_shapes=[
            pltpu.VMEM((TOP_K, tc, D_MODEL), F32),
            pltpu.SemaphoreType.DMA,
        ],
        compiler_params=pltpu.CompilerParams(
            dimension_semantics=("arbitrary",), vmem_limit_bytes=VMEM_LIMIT),
        name="combine",
    )(pos3, h2, wt, g, ys)


def _routing_tables(rid, n_tok):
    e_flat = rid[:TOP_K].reshape(-1)
    onehot = (e_flat[None, :] == jnp.arange(N_EXPERTS, dtype=jnp.int32)[:, None]).astype(jnp.int32)
    csum = jnp.cumsum(onehot, axis=1)
    counts = csum[:, -1]
    rank = jnp.sum(onehot * csum, axis=0) - 1
    padded = (counts + MOE_M - 1) // MOE_M * MOE_M
    pend = jnp.cumsum(padded)
    pstart = pend - padded
    pos = jnp.sum(onehot * pstart[:, None], axis=0) + rank
    n_blocks = (TOP_K * n_tok + N_EXPERTS * (MOE_M - 1) + MOE_M - 1) // MOE_M
    n_used = pend[-1] // MOE_M
    blk = jnp.arange(n_blocks, dtype=jnp.int32)
    blk = jnp.minimum(blk, n_used - 1)
    block_e = jnp.sum((pend[None, :] <= (blk * MOE_M)[:, None]).astype(jnp.int32), axis=1)
    block_e = jnp.minimum(block_e, N_EXPERTS - 1)
    last_blk = jnp.maximum(pend - MOE_M, 0).astype(jnp.int32)
    has = (counts > 0).astype(jnp.int32)
    return (pos.reshape(TOP_K, n_tok).astype(jnp.int32), block_e.astype(jnp.int32),
            n_used.reshape(1).astype(jnp.int32), last_blk, has, n_blocks * MOE_M)


def _tile_pos(pos, tile):
    k, t = pos.shape
    return pos.reshape(k, t // tile, tile).transpose(1, 0, 2)


def kernel(x, mem, mix_norm_g, w_in, conv_w, w_conv_out, w_ret_out, w_mix_out, xa_norm_g, mem_norm_g,
           w_xa_q, w_xa_kv, w_xa_o, moe_norm_g, w_group, b_group, w_router, b_router, w_gate, w_up,
           w_down, final_norm_g):
    batch, seq, d = x.shape
    depth = w_in.shape[0]
    t = batch * seq
    h = x.reshape(t, d)
    for l in range(depth):
        proj = _in_proj(h, mix_norm_g[l][None], w_in[l].astype(BF16))
        h1 = _mixer(proj, h, conv_w[l], w_conv_out[l].astype(BF16), w_ret_out[l].astype(BF16),
                    w_mix_out[l].astype(BF16), batch, seq)
        kv = _mem_kv(mem, mem_norm_g[l][None], w_xa_kv[l].astype(BF16))

        rw = jnp.zeros((ROUTE_ROWS, d), F32)
        rw = rw.at[0:N_GROUPS].set(w_group[l].T).at[ROUTE_E0:ROUTE_E0 + N_EXPERTS].set(w_router[l].T)
        rwh = rw.astype(BF16)
        rwl = (rw - rwh.astype(F32)).astype(BF16)
        rb = jnp.zeros((ROUTE_ROWS,), F32)
        rb = rb.at[0:N_GROUPS].set(b_group[l]).at[ROUTE_E0:ROUTE_E0 + N_EXPERTS].set(b_router[l])
        rb = jnp.broadcast_to(rb[:, None], (ROUTE_ROWS, LANES))

        h2, xn2, rid, rwt = _xattn(h1, kv, xa_norm_g[l][None], w_xa_q[l].astype(BF16),
                                   w_xa_o[l].astype(BF16), moe_norm_g[l][None], rwh, rwl, rb, batch, seq)

        pos, block_e, n_used, last_blk, has, n_rows = _routing_tables(rid, t)
        xs = _dispatch(last_blk, has, n_used, _tile_pos(pos, DISP_T), xn2, n_rows)
        ys = _experts(block_e, n_used, xs, w_gate[l].astype(BF16), w_up[l].astype(BF16),
                      w_down[l].astype(BF16))
        h = _combine(_tile_pos(pos, COMB_T), h2, rwt[:TOP_K].T, final_norm_g[None], ys,
                     final_norm=(l == depth - 1))
    return h.reshape(batch, seq, d)
```

```python
import functools

import numpy as np
import jax
import jax.numpy as jnp
from jax import lax
from jax.experimental import pallas as pl
from jax.experimental.pallas import tpu as pltpu

F32 = jnp.float32
BF16 = jnp.bfloat16

D_MODEL = 1024
CONV_WIDTH = 512
CONV_K = 3
RET_HEADS = 8
RET_DK = 64
RET_DV = 128
RET_CHUNK = 128
QK_WIDTH = RET_HEADS * RET_DK
V_WIDTH = RET_HEADS * RET_DV
ROPE_BASE = 10000.0
IN_WIDTH = 3 * CONV_WIDTH + 2 * QK_WIDTH + 2 * V_WIDTH + 2 * D_MODEL
OFF_XIN = 0
OFF_BG = OFF_XIN + CONV_WIDTH
OFF_CG = OFF_BG + CONV_WIDTH
OFF_Q = OFF_CG + CONV_WIDTH
OFF_K = OFF_Q + QK_WIDTH
OFF_V = OFF_K + QK_WIDTH
OFF_G = OFF_V + V_WIDTH
OFF_GATE_C = OFF_G + V_WIDTH
OFF_GATE_R = OFF_GATE_C + D_MODEL
MEM_LEN = 256
XA_HEADS = 4
XA_HEAD_DIM = D_MODEL // XA_HEADS
N_GROUPS = 4
EXPERTS_PER_GROUP = 8
N_EXPERTS = N_GROUPS * EXPERTS_PER_GROUP
TOP_K = 2
EXPERT_HIDDEN = D_MODEL // 2
EPS = 1e-6

LANES = 128
SUBLANES = 8
VMEM_LIMIT = 56 * 1024 * 1024

IN_TM = 2048
IN_TN = 512
MIX_TS = 256
XA_TS = 512
ROUTE_ROWS = 40
ROUTE_E0 = 8
MOE_M = 256
COMB_T = 512


def _rms(x, g):
    ms = jnp.mean(x * x, axis=-1, keepdims=True)
    return x * lax.rsqrt(ms + EPS) * g


def _sigmoid(x):
    return 1.0 / (1.0 + jnp.exp(-x))


def _dot(a, b):
    return jnp.dot(a, b, preferred_element_type=F32)


def _dot_nt(a, b):
    return lax.dot_general(a, b, (((1,), (1,)), ((), ())), preferred_element_type=F32)


def _dot_tn(a, b):
    return lax.dot_general(a, b, (((0,), (0,)), ((), ())), preferred_element_type=F32)


def _inproj_kernel(x_ref, g_ref, w_ref, o_ref, xn_ref):
    @pl.when(pl.program_id(1) == 0)
    def _():
        xn_ref[...] = _rms(x_ref[...], g_ref[...]).astype(BF16)

    o_ref[...] = _dot(xn_ref[...], w_ref[...]).astype(BF16)


def _in_proj(x2, g, w_bf):
    t = x2.shape[0]
    return pl.pallas_call(
        _inproj_kernel,
        out_shape=jax.ShapeDtypeStruct((t, IN_WIDTH), BF16),
        grid=(t // IN_TM, IN_WIDTH // IN_TN),
        in_specs=[
            pl.BlockSpec((IN_TM, D_MODEL), lambda i, j: (i, 0)),
            pl.BlockSpec((1, D_MODEL), lambda i, j: (0, 0)),
            pl.BlockSpec((D_MODEL, IN_TN), lambda i, j: (0, j)),
        ],
        out_specs=pl.BlockSpec((IN_TM, IN_TN), lambda i, j: (i, j)),
        scratch_shapes=[pltpu.VMEM((IN_TM, D_MODEL), BF16)],
        compiler_params=pltpu.CompilerParams(
            dimension_semantics=("arbitrary", "arbitrary"), vmem_limit_bytes=VMEM_LIMIT),
        name="in_proj",
    )(x2, g, w_bf)


def _retention_constants(seq):
    pos = np.arange(seq, dtype=np.float64)
    inv_freq = ROPE_BASE ** (-np.arange(0, RET_DK, 2, dtype=np.float64) / RET_DK)
    ang = pos[:, None] * inv_freq[None, :]
    cos, sin = np.cos(ang), np.sin(ang)
    cos_t = np.concatenate([cos, cos, cos, cos], axis=1)
    sin_t = np.concatenate([-sin, sin, -sin, sin], axis=1)
    log_g = np.log(1.0 - 2.0 ** (-5.0 - np.arange(RET_HEADS, dtype=np.float64)))
    idx = np.arange(RET_CHUNK, dtype=np.float64)
    diff = idx[:, None] - idx[None, :]
    decay = np.where(diff >= 0, np.exp(np.maximum(diff, 0.0)[None] * log_g[:, None, None]), 0.0)
    zeta = np.exp((RET_CHUNK - 1 - idx)[None, :] * log_g[:, None])
    xi = np.exp((idx + 1)[None, :] * log_g[:, None])
    zeta_t = np.repeat(zeta.T, RET_DK, axis=1)
    xi_t = np.repeat(xi.T, RET_DK, axis=1)
    chunk_decay = np.exp(RET_CHUNK * log_g)
    f = lambda a: jnp.asarray(a, dtype=F32)
    return f(cos_t), f(sin_t), f(xi_t), f(zeta_t), f(decay), [float(c) for c in chunk_decay]


def _mixer_kernel(chunk_decay, proj_ref, x_ref, cos_ref, sin_ref, xi_ref, zeta_ref, decay_ref,
                  convw_ref, wc_ref, wr_ref, wm_ref, o_ref, state_ref, tail_ref, yin_ref):
    ts = x_ref.shape[0]

    @pl.when(pl.program_id(1) == 0)
    def _():
        state_ref[...] = jnp.zeros_like(state_ref)
        tail_ref[...] = jnp.zeros_like(tail_ref)

    xin = proj_ref[:, OFF_XIN:OFF_XIN + CONV_WIDTH].astype(F32)
    bg = proj_ref[:, OFF_BG:OFF_BG + CONV_WIDTH].astype(F32)
    cg = proj_ref[:, OFF_CG:OFF_CG + CONV_WIDTH].astype(F32)
    u = cg * xin
    ue = jnp.concatenate([tail_ref[...], u], axis=0)
    u1 = pltpu.roll(ue, 1, 0)[SUBLANES:]
    u2 = pltpu.roll(ue, 2, 0)[SUBLANES:]
    tail_ref[...] = u[ts - SUBLANES:]
    c = convw_ref[2:3, :] * u + convw_ref[1:2, :] * u1 + convw_ref[0:1, :] * u2
    y_conv = _dot((bg * c).astype(BF16), wc_ref[...])

    lane = lax.broadcasted_iota(jnp.int32, (1, LANES), 1)
    low_half = (lane % RET_DK) < (RET_DK // 2)
    head_masks = [(lane // RET_DK) == j for j in range(LANES // RET_DK)]

    def rotary(t, cosv, sinv):
        outs = []
        for p in range(QK_WIDTH // LANES):
            tp = t[:, p * LANES:(p + 1) * LANES]
            fwd = pltpu.roll(tp, LANES - RET_DK // 2, 1)
            bwd = pltpu.roll(tp, RET_DK // 2, 1)
            outs.append(tp * cosv + jnp.where(low_half, fwd, bwd) * sinv)
        return jnp.concatenate(outs, axis=1)

    for ci in range(ts // RET_CHUNK):
        rows = slice(ci * RET_CHUNK, (ci + 1) * RET_CHUNK)
        cosv = cos_ref[rows, :]
        sinv = sin_ref[rows, :]
        qr = rotary(proj_ref[rows, OFF_Q:OFF_Q + QK_WIDTH].astype(F32), cosv, sinv)
        kr = rotary(proj_ref[rows, OFF_K:OFF_K + QK_WIDTH].astype(F32), cosv, sinv) * (RET_DK ** -0.5)
        qb = qr.astype(BF16)
        qx = (qr * xi_ref[...]).astype(BF16)
        kb = kr.astype(BF16)
        kz = (kr * zeta_ref[...]).astype(BF16)
        for h in range(RET_HEADS):
            lanes = slice((h // 2) * LANES, (h // 2 + 1) * LANES)
            m = head_masks[h % 2]
            zero = jnp.zeros((), BF16)
            qm = jnp.where(m, qb[:, lanes], zero)
            qxm = jnp.where(m, qx[:, lanes], zero)
            vh = proj_ref[rows, OFF_V + h * RET_DV:OFF_V + (h + 1) * RET_DV]
            scores = _dot_nt(qm, kb[:, lanes]) * decay_ref[h]
            st = state_ref[h]
            o = _dot(scores.astype(BF16), vh) + _dot(qxm, st.astype(BF16))
            state_ref[h] = chunk_decay[h] * st + _dot_tn(kz[:, lanes], vh)
            mu = jnp.mean(o, axis=-1, keepdims=True)
            dlt = o - mu
            var = jnp.mean(dlt * dlt, axis=-1, keepdims=True)
            g = proj_ref[rows, OFF_G + h * RET_DV:OFF_G + (h + 1) * RET_DV].astype(F32)
            yin_ref[rows, h * RET_DV:(h + 1) * RET_DV] = (
                g * _sigmoid(g) * (dlt * lax.rsqrt(var + EPS))).astype(BF16)

    y_ret = _dot(yin_ref[...], wr_ref[...])
    gate_c = proj_ref[:, OFF_GATE_C:OFF_GATE_C + D_MODEL].astype(F32)
    gate_r = proj_ref[:, OFF_GATE_R:OFF_GATE_R + D_MODEL].astype(F32)
    merged = _sigmoid(gate_c) * y_conv + _sigmoid(gate_r) * y_ret
    o_ref[...] = x_ref[...] + _dot(merged.astype(BF16), wm_ref[...])


def _mixer(proj, x2, conv_w, wc_bf, wr_bf, wm_bf, batch, seq):
    ts = MIX_TS
    ns = seq // ts
    cos_t, sin_t, xi_t, zeta_t, decay, chunk_decay = _retention_constants(seq)
    full = lambda shape: pl.BlockSpec(shape, lambda b, s: (0,) * len(shape))
    return pl.pallas_call(
        functools.partial(_mixer_kernel, chunk_decay),
        out_shape=jax.ShapeDtypeStruct((batch * seq, D_MODEL), F32),
        grid=(batch, ns),
        in_specs=[
            pl.BlockSpec((ts, IN_WIDTH), lambda b, s: (b * ns + s, 0)),
            pl.BlockSpec((ts, D_MODEL), lambda b, s: (b * ns + s, 0)),
            pl.BlockSpec((ts, LANES), lambda b, s: (s, 0)),
            pl.BlockSpec((ts, LANES), lambda b, s: (s, 0)),
            full((RET_CHUNK, QK_WIDTH)),
            full((RET_CHUNK, QK_WIDTH)),
            full((RET_HEADS, RET_CHUNK, RET_CHUNK)),
            full((CONV_K, CONV_WIDTH)),
            full((CONV_WIDTH, D_MODEL)),
            full((V_WIDTH, D_MODEL)),
            full((D_MODEL, D_MODEL)),
        ],
        out_specs=pl.BlockSpec((ts, D_MODEL), lambda b, s: (b * ns + s, 0)),
        scratch_shapes=[
            pltpu.VMEM((RET_HEADS, LANES, RET_DV), F32),
            pltpu.VMEM((SUBLANES, CONV_WIDTH), F32),
            pltpu.VMEM((ts, V_WIDTH), BF16),
        ],
        compiler_params=pltpu.CompilerParams(
            dimension_semantics=("arbitrary", "arbitrary"), vmem_limit_bytes=VMEM_LIMIT),
        name="mixer",
    )(proj, x2, cos_t, sin_t, xi_t, zeta_t, decay, conv_w, wc_bf, wr_bf, wm_bf)


def _memkv_kernel(m_ref, g_ref, w_ref, o_ref):
    o_ref[0] = _dot(_rms(m_ref[0], g_ref[...]).astype(BF16), w_ref[...]).astype(BF16)


def _mem_kv(mem, g, wkv_bf):
    b = mem.shape[0]
    return pl.pallas_call(
        _memkv_kernel,
        out_shape=jax.ShapeDtypeStruct((b, MEM_LEN, 2 * D_MODEL), BF16),
        grid=(b,),
        in_specs=[
            pl.BlockSpec((1, MEM_LEN, D_MODEL), lambda i: (i, 0, 0)),
            pl.BlockSpec((1, D_MODEL), lambda i: (0, 0)),
            pl.BlockSpec((D_MODEL, 2 * D_MODEL), lambda i: (0, 0)),
        ],
        out_specs=pl.BlockSpec((1, MEM_LEN, 2 * D_MODEL), lambda i: (i, 0, 0)),
        compiler_params=pltpu.CompilerParams(
            dimension_semantics=("arbitrary",), vmem_limit_bytes=VMEM_LIMIT),
        name="mem_kv",
    )(mem, g, wkv_bf)


def _xattn_kernel(h_ref, kv_ref, gx_ref, wq_ref, wo_ref, gm_ref, rwh_ref, rwl_ref, rb_ref,
                  h2_ref, xn2_ref, rid_ref, rwt_ref):
    ts = h_ref.shape[0]
    h = h_ref[...]
    q = _dot(_rms(h, gx_ref[...]).astype(BF16), wq_ref[...])
    outs = []
    for hd in range(XA_HEADS):
        cols = slice(hd * XA_HEAD_DIM, (hd + 1) * XA_HEAD_DIM)
        kh = kv_ref[0, :, cols]
        vh = kv_ref[0, :, D_MODEL + hd * XA_HEAD_DIM:D_MODEL + (hd + 1) * XA_HEAD_DIM]
        s = _dot_nt(q[:, cols].astype(BF16), kh) * (XA_HEAD_DIM ** -0.5)
        e = jnp.exp(s - jnp.max(s, axis=-1, keepdims=True))
        p = e / jnp.sum(e, axis=-1, keepdims=True)
        outs.append(_dot(p.astype(BF16), vh))
    h2 = h + _dot(jnp.concatenate(outs, axis=1).astype(BF16), wo_ref[...])
    h2_ref[...] = h2
    xn = _rms(h2, gm_ref[...])
    xn2_ref[...] = xn

    hi = xn.astype(BF16)
    lo = (xn - hi.astype(F32)).astype(BF16)
    lg = (_dot_nt(rwh_ref[...], hi) + _dot_nt(rwh_ref[...], lo) + _dot_nt(rwl_ref[...], hi)
          + rb_ref[:, 0:1])

    gl = lg[0:N_GROUPS]
    ge = jnp.exp(gl - jnp.max(gl, axis=0, keepdims=True))
    gp = ge / jnp.sum(ge, axis=0, keepdims=True)
    p_g = gp[0:1]
    g_idx = jnp.zeros((1, ts), jnp.int32)
    for i in range(1, N_GROUPS):
        better = gp[i:i + 1] > p_g
        g_idx = jnp.where(better, i, g_idx)
        p_g = jnp.where(better, gp[i:i + 1], p_g)
    sel = jnp.zeros((EXPERTS_PER_GROUP, ts), F32)
    for i in range(N_GROUPS):
        r0 = ROUTE_E0 + i * EXPERTS_PER_GROUP
        sel = jnp.where(g_idx == i, lg[r0:r0 + EXPERTS_PER_GROUP], sel)
    se = jnp.exp(sel - jnp.max(sel, axis=0, keepdims=True))
    sp = se / jnp.sum(se, axis=0, keepdims=True)
    ridx = lax.broadcasted_iota(jnp.int32, (EXPERTS_PER_GROUP, ts), 0)
    m1 = jnp.max(sp, axis=0, keepdims=True)
    i1 = jnp.min(jnp.where(sp == m1, ridx, EXPERTS_PER_GROUP), axis=0, keepdims=True)
    sp2 = jnp.where(ridx == i1, -1.0, sp)
    m2 = jnp.max(sp2, axis=0, keepdims=True)
    i2 = jnp.min(jnp.where(sp2 == m2, ridx, EXPERTS_PER_GROUP), axis=0, keepdims=True)
    den = m1 + m2
    rid_ref[...] = jnp.zeros_like(rid_ref)
    rwt_ref[...] = jnp.zeros_like(rwt_ref)
    rid_ref[0:1, :] = g_idx * EXPERTS_PER_GROUP + i1
    rid_ref[1:2, :] = g_idx * EXPERTS_PER_GROUP + i2
    rwt_ref[0:1, :] = p_g * m1 / den
    rwt_ref[1:2, :] = p_g * m2 / den


def _xattn(h1, kv, gx, wq_bf, wo_bf, gm, rwh, rwl, rb, batch, seq):
    ts = XA_TS
    ns = seq // ts
    t = batch * seq
    full = lambda shape: pl.BlockSpec(shape, lambda b, s: (0,) * len(shape))
    return pl.pallas_call(
        _xattn_kernel,
        out_shape=(
            jax.ShapeDtypeStruct((t, D_MODEL), F32),
            jax.ShapeDtypeStruct((t, D_MODEL), F32),
            jax.ShapeDtypeStruct((SUBLANES, t), jnp.int32),
            jax.ShapeDtypeStruct((SUBLANES, t), F32),
        ),
        grid=(batch, ns),
        in_specs=[
            pl.BlockSpec((ts, D_MODEL), lambda b, s: (b * ns + s, 0)),
            pl.BlockSpec((1, MEM_LEN, 2 * D_MODEL), lambda b, s: (b, 0, 0)),
            full((1, D_MODEL)),
            full((D_MODEL, D_MODEL)),
            full((D_MODEL, D_MODEL)),
            full((1, D_MODEL)),
            full((ROUTE_ROWS, D_MODEL)),
            full((ROUTE_ROWS, D_MODEL)),
            full((ROUTE_ROWS, LANES)),
        ],
        out_specs=(
            pl.BlockSpec((ts, D_MODEL), lambda b, s: (b * ns + s, 0)),
            pl.BlockSpec((ts, D_MODEL), lambda b, s: (b * ns + s, 0)),
            pl.BlockSpec((SUBLANES, ts), lambda b, s: (0, b * ns + s)),
            pl.BlockSpec((SUBLANES, ts), lambda b, s: (0, b * ns + s)),
        ),
        compiler_params=pltpu.CompilerParams(
            dimension_semantics=("arbitrary", "arbitrary"), vmem_limit_bytes=VMEM_LIMIT),
        name="xattn",
    )(h1, kv, gx, wq_bf, wo_bf, gm, rwh, rwl, rb)


def _row_copy(src_ref, src_row, dst_ref, dst_row, sem):
    return pltpu.make_async_copy(src_ref.at[pl.ds(src_row, 1)], dst_ref.at[pl.ds(dst_row, 1)], sem)


def _expert_kernel(n_tok, be_ref, nu_ref, src_ref, nxt_ref, dst_ref, xn_ref, wg_ref, wu_ref, wd_ref,
                   o_ref, xbuf, ybuf, wgb, wub, wdb, gsem, ssem, zsem):
    i = pl.program_id(0)
    nu = nu_ref[0]

    def gather(rows_ref, s):
        return [_row_copy(xn_ref, rows_ref[0, 0, r], xbuf.at[s], r, gsem.at[s]) for r in range(MOE_M)]

    def scatter(s):
        return [_row_copy(ybuf.at[s], r, o_ref, dst_ref[0, 0, r], ssem.at[s]) for r in range(MOE_M)]

    @pl.when(i == 0)
    def _():
        ybuf[...] = jnp.zeros_like(ybuf)
        fills = [pltpu.make_async_copy(ybuf.at[s], o_ref.at[pl.ds(TOP_K * n_tok + s * MOE_M, MOE_M)], zsem)
                 for s in range(2)]
        for c in fills:
            c.start()
        for c in fills:
            c.wait()
        for c in gather(src_ref, 0):
            c.start()

    @pl.when((i == 0) | (be_ref[i] != be_ref[jnp.maximum(i - 1, 0)]))
    def _():
        wgb[...] = wg_ref[0].astype(BF16)
        wub[...] = wu_ref[0].astype(BF16)
        wdb[...] = wd_ref[0].astype(BF16)

    def block(s):
        for c in gather(src_ref, s):
            c.wait()
        for c in gather(nxt_ref, 1 - s):
            c.start()

        @pl.when(i >= 2)
        def _():
            for c in scatter(s):
                c.wait()

        x = xbuf[s].astype(BF16)
        g = _dot(x, wgb[...])
        u = _dot(x, wub[...])
        ybuf[s] = _dot((g * _sigmoid(g) * u).astype(BF16), wdb[...])
        for c in scatter(s):
            c.start()

        @pl.when(i + 2 >= nu)
        def _():
            for c in scatter(s):
                c.wait()

        @pl.when(i + 1 >= nu)
        def _():
            for c in gather(nxt_ref, 1 - s):
                c.wait()

    for s in range(2):
        @pl.when((i < nu) & (i % 2 == s))
        def _():
            block(s)


def _experts(block_e, n_used, row_dst, xn2, w_gate, w_up, w_down):
    n_tok = xn2.shape[0]
    n_blocks = row_dst.shape[0] // MOE_M
    dst3 = row_dst.reshape(n_blocks, 1, MOE_M)
    src3 = dst3 % n_tok
    cur_map = lambda i, be, nu: (jnp.minimum(i, nu[0] - 1), 0, 0)
    nxt_map = lambda i, be, nu: (jnp.minimum(i + 1, nu[0] - 1), 0, 0)
    w_map = lambda i, be, nu: (be[i], 0, 0)
    return pl.pallas_call(
        functools.partial(_expert_kernel, n_tok),
        out_shape=jax.ShapeDtypeStruct((TOP_K * n_tok + 2 * MOE_M, D_MODEL), F32),
        grid_spec=pltpu.PrefetchScalarGridSpec(
            num_scalar_prefetch=2,
            grid=(n_blocks,),
            in_specs=[
                pl.BlockSpec((1, 1, MOE_M), cur_map, memory_space=pltpu.SMEM),
                pl.BlockSpec((1, 1, MOE_M), nxt_map, memory_space=pltpu.SMEM),
                pl.BlockSpec((1, 1, MOE_M), cur_map, memory_space=pltpu.SMEM),
                pl.BlockSpec(memory_space=pl.ANY),
                pl.BlockSpec((1, D_MODEL, EXPERT_HIDDEN), w_map),
                pl.BlockSpec((1, D_MODEL, EXPERT_HIDDEN), w_map),
                pl.BlockSpec((1, EXPERT_HIDDEN, D_MODEL), w_map),
            ],
            out_specs=pl.BlockSpec(memory_space=pl.ANY),
            scratch_shapes=[
                pltpu.VMEM((2, MOE_M, D_MODEL), F32),
                pltpu.VMEM((2, MOE_M, D_MODEL), F32),
                pltpu.VMEM((D_MODEL, EXPERT_HIDDEN), BF16),
                pltpu.VMEM((D_MODEL, EXPERT_HIDDEN), BF16),
                pltpu.VMEM((EXPERT_HIDDEN, D_MODEL), BF16),
                pltpu.SemaphoreType.DMA((2,)),
                pltpu.SemaphoreType.DMA((2,)),
                pltpu.SemaphoreType.DMA,
            ],
        ),
        compiler_params=pltpu.CompilerParams(
            dimension_semantics=("arbitrary",), vmem_limit_bytes=VMEM_LIMIT),
        name="experts",
    )(block_e, n_used, src3, src3, dst3, xn2, w_gate, w_up, w_down)


def _combine_kernel(final_norm, h2_ref, y0_ref, y1_ref, wt_ref, g_ref, o_ref):
    y = h2_ref[...] + wt_ref[:, 0:1] * y0_ref[...] + wt_ref[:, 1:2] * y1_ref[...]
    o_ref[...] = _rms(y, g_ref[...]) if final_norm else y


def _combine(h2, y2, wt, g, final_norm):
    t = h2.shape[0]
    tc = COMB_T
    nt = t // tc
    return pl.pallas_call(
        functools.partial(_combine_kernel, final_norm),
        out_shape=jax.ShapeDtypeStruct((t, D_MODEL), F32),
        grid=(nt,),
        in_specs=[
            pl.BlockSpec((tc, D_MODEL), lambda i: (i, 0)),
            pl.BlockSpec((tc, D_MODEL), lambda i: (i, 0)),
            pl.BlockSpec((tc, D_MODEL), lambda i: (nt + i, 0)),
            pl.BlockSpec((tc, TOP_K), lambda i: (i, 0)),
            pl.BlockSpec((1, D_MODEL), lambda i: (0, 0)),
        ],
        out_specs=pl.BlockSpec((tc, D_MODEL), lambda i: (i, 0)),
        compiler_params=pltpu.CompilerParams(
            dimension_semantics=("arbitrary",), vmem_limit_bytes=VMEM_LIMIT),
        name="combine",
    )(h2, y2, y2, wt, g)


def _routing_tables(rid, n_tok):
    n_asg = TOP_K * n_tok
    e_flat = rid[:TOP_K].reshape(-1)
    onehot = (e_flat[None, :] == jnp.arange(N_EXPERTS, dtype=jnp.int32)[:, None]).astype(jnp.int32)
    csum = jnp.cumsum(onehot, axis=1)
    counts = csum[:, -1]
    rank = jnp.sum(onehot * csum, axis=0) - 1
    padded = (counts + MOE_M - 1) // MOE_M * MOE_M
    pend = jnp.cumsum(padded)
    pstart = pend - padded
    pos = jnp.sum(onehot * pstart[:, None], axis=0) + rank
    n_blocks = (n_asg + N_EXPERTS * (MOE_M - 1) + MOE_M - 1) // MOE_M
    n_rows = n_blocks * MOE_M
    spare = n_asg + jnp.arange(n_rows, dtype=jnp.int32) % (2 * MOE_M)
    row_dst = spare.at[pos].set(jnp.arange(n_asg, dtype=jnp.int32), unique_indices=True)
    n_used = pend[-1] // MOE_M
    blk = jnp.minimum(jnp.arange(n_blocks, dtype=jnp.int32), n_used - 1)
    block_e = jnp.sum((pend[None, :] <= (blk * MOE_M)[:, None]).astype(jnp.int32), axis=1)
    block_e = jnp.minimum(block_e, N_EXPERTS - 1).astype(jnp.int32)
    return row_dst, block_e, n_used.reshape(1).astype(jnp.int32)


def kernel(x, mem, mix_norm_g, w_in, conv_w, w_conv_out, w_ret_out, w_mix_out, xa_norm_g, mem_norm_g,
           w_xa_q, w_xa_kv, w_xa_o, moe_norm_g, w_group, b_group, w_router, b_router, w_gate, w_up,
           w_down, final_norm_g):
    batch, seq, d = x.shape
    depth = w_in.shape[0]
    t = batch * seq
    h = x.reshape(t, d)
    for l in range(depth):
        proj = _in_proj(h, mix_norm_g[l][None], w_in[l].astype(BF16))
        h1 = _mixer(proj, h, conv_w[l], w_conv_out[l].astype(BF16), w_ret_out[l].astype(BF16),
                    w_mix_out[l].astype(BF16), batch, seq)
        kv = _mem_kv(mem, mem_norm_g[l][None], w_xa_kv[l].astype(BF16))

        rw = jnp.zeros((ROUTE_ROWS, d), F32)
        rw = rw.at[0:N_GROUPS].set(w_group[l].T).at[ROUTE_E0:ROUTE_E0 + N_EXPERTS].set(w_router[l].T)
        rwh = rw.astype(BF16)
        rwl = (rw - rwh.astype(F32)).astype(BF16)
        rb = jnp.zeros((ROUTE_ROWS,), F32)
        rb = rb.at[0:N_GROUPS].set(b_group[l]).at[ROUTE_E0:ROUTE_E0 + N_EXPERTS].set(b_router[l])
        rb = jnp.broadcast_to(rb[:, None], (ROUTE_ROWS, LANES))

        h2, xn2, rid, rwt = _xattn(h1, kv, xa_norm_g[l][None], w_xa_q[l].astype(BF16),
                                   w_xa_o[l].astype(BF16), moe_norm_g[l][None], rwh, rwl, rb, batch, seq)

        row_dst, block_e, n_used = _routing_tables(rid, t)
        y2 = _experts(block_e, n_used, row_dst, xn2, w_gate[l], w_up[l], w_down[l])
        h = _combine(h2, y2, rwt[:TOP_K].T, final_norm_g[None], final_norm=(l == depth - 1))
    return h.reshape(batch, seq, d)
```

```python
import functools

import numpy as np
import jax
import jax.numpy as jnp
from jax import lax
from jax.experimental import pallas as pl
from jax.experimental.pallas import tpu as pltpu

F32 = jnp.float32
BF16 = jnp.bfloat16

D_MODEL = 1024
CONV_WIDTH = 512
CONV_K = 3
RET_HEADS = 8
RET_DK = 64
RET_DV = 128
RET_CHUNK = 128
QK_WIDTH = RET_HEADS * RET_DK
V_WIDTH = RET_HEADS * RET_DV
ROPE_BASE = 10000.0
IN_WIDTH = 3 * CONV_WIDTH + 2 * QK_WIDTH + 2 * V_WIDTH + 2 * D_MODEL
OFF_XIN = 0
OFF_BG = OFF_XIN + CONV_WIDTH
OFF_CG = OFF_BG + CONV_WIDTH
OFF_Q = OFF_CG + CONV_WIDTH
OFF_K = OFF_Q + QK_WIDTH
OFF_V = OFF_K + QK_WIDTH
OFF_G = OFF_V + V_WIDTH
OFF_GATE_C = OFF_G + V_WIDTH
OFF_GATE_R = OFF_GATE_C + D_MODEL
MEM_LEN = 256
XA_HEADS = 4
XA_HEAD_DIM = D_MODEL // XA_HEADS
N_GROUPS = 4
EXPERTS_PER_GROUP = 8
N_EXPERTS = N_GROUPS * EXPERTS_PER_GROUP
TOP_K = 2
EXPERT_HIDDEN = D_MODEL // 2
EPS = 1e-6

LANES = 128
SUBLANES = 8
VMEM_LIMIT = 56 * 1024 * 1024

IN_TM = 2048
IN_TN = 512
MIX_TS = 256
XA_TS = 512
ROUTE_ROWS = 40
ROUTE_E0 = 8
MOE_M = 256
DISP_T = 256
COMB_T = 256


def _rms(x, g):
    ms = jnp.mean(x * x, axis=-1, keepdims=True)
    return x * lax.rsqrt(ms + EPS) * g


def _sigmoid(x):
    return 1.0 / (1.0 + jnp.exp(-x))


def _dot(a, b):
    return jnp.dot(a, b, preferred_element_type=F32)


def _dot_nt(a, b):
    return lax.dot_general(a, b, (((1,), (1,)), ((), ())), preferred_element_type=F32)


def _dot_tn(a, b):
    return lax.dot_general(a, b, (((0,), (0,)), ((), ())), preferred_element_type=F32)


def _inproj_kernel(x_ref, g_ref, w_ref, o_ref, xn_ref):
    @pl.when(pl.program_id(1) == 0)
    def _():
        xn_ref[...] = _rms(x_ref[...], g_ref[...]).astype(BF16)

    o_ref[...] = _dot(xn_ref[...], w_ref[...]).astype(BF16)


def _in_proj(x2, g, w_bf):
    t = x2.shape[0]
    return pl.pallas_call(
        _inproj_kernel,
        out_shape=jax.ShapeDtypeStruct((t, IN_WIDTH), BF16),
        grid=(t // IN_TM, IN_WIDTH // IN_TN),
        in_specs=[
            pl.BlockSpec((IN_TM, D_MODEL), lambda i, j: (i, 0)),
            pl.BlockSpec((1, D_MODEL), lambda i, j: (0, 0)),
            pl.BlockSpec((D_MODEL, IN_TN), lambda i, j: (0, j)),
        ],
        out_specs=pl.BlockSpec((IN_TM, IN_TN), lambda i, j: (i, j)),
        scratch_shapes=[pltpu.VMEM((IN_TM, D_MODEL), BF16)],
        compiler_params=pltpu.CompilerParams(
            dimension_semantics=("arbitrary", "arbitrary"), vmem_limit_bytes=VMEM_LIMIT),
        name="in_proj",
    )(x2, g, w_bf)


def _retention_constants(seq):
    pos = np.arange(seq, dtype=np.float64)
    inv_freq = ROPE_BASE ** (-np.arange(0, RET_DK, 2, dtype=np.float64) / RET_DK)
    ang = pos[:, None] * inv_freq[None, :]
    cos, sin = np.cos(ang), np.sin(ang)
    cos_t = np.concatenate([cos, cos, cos, cos], axis=1)
    sin_t = np.concatenate([-sin, sin, -sin, sin], axis=1)
    log_g = np.log(1.0 - 2.0 ** (-5.0 - np.arange(RET_HEADS, dtype=np.float64)))
    idx = np.arange(RET_CHUNK, dtype=np.float64)
    diff = idx[:, None] - idx[None, :]
    decay = np.where(diff >= 0, np.exp(np.maximum(diff, 0.0)[None] * log_g[:, None, None]), 0.0)
    zeta = np.exp((RET_CHUNK - 1 - idx)[None, :] * log_g[:, None])
    xi = np.exp((idx + 1)[None, :] * log_g[:, None])
    zeta_t = np.repeat(zeta.T, RET_DK, axis=1)
    xi_t = np.repeat(xi.T, RET_DK, axis=1)
    chunk_decay = np.exp(RET_CHUNK * log_g)
    f = lambda a: jnp.asarray(a, dtype=F32)
    return f(cos_t), f(sin_t), f(xi_t), f(zeta_t), f(decay), [float(c) for c in chunk_decay]


def _mixer_kernel(chunk_decay, proj_ref, x_ref, cos_ref, sin_ref, xi_ref, zeta_ref, decay_ref,
                  convw_ref, wc_ref, wr_ref, wm_ref, o_ref, state_ref, tail_ref, yin_ref):
    ts = x_ref.shape[0]

    @pl.when(pl.program_id(1) == 0)
    def _():
        state_ref[...] = jnp.zeros_like(state_ref)
        tail_ref[...] = jnp.zeros_like(tail_ref)

    xin = proj_ref[:, OFF_XIN:OFF_XIN + CONV_WIDTH].astype(F32)
    bg = proj_ref[:, OFF_BG:OFF_BG + CONV_WIDTH].astype(F32)
    cg = proj_ref[:, OFF_CG:OFF_CG + CONV_WIDTH].astype(F32)
    u = cg * xin
    ue = jnp.concatenate([tail_ref[...], u], axis=0)
    u1 = pltpu.roll(ue, 1, 0)[SUBLANES:]
    u2 = pltpu.roll(ue, 2, 0)[SUBLANES:]
    tail_ref[...] = u[ts - SUBLANES:]
    c = convw_ref[2:3, :] * u + convw_ref[1:2, :] * u1 + convw_ref[0:1, :] * u2
    y_conv = _dot((bg * c).astype(BF16), wc_ref[...])

    lane = lax.broadcasted_iota(jnp.int32, (1, LANES), 1)
    low_half = (lane % RET_DK) < (RET_DK // 2)
    head_masks = [(lane // RET_DK) == j for j in range(LANES // RET_DK)]

    def rotary(t, cosv, sinv):
        outs = []
        for p in range(QK_WIDTH // LANES):
            tp = t[:, p * LANES:(p + 1) * LANES]
            fwd = pltpu.roll(tp, LANES - RET_DK // 2, 1)
            bwd = pltpu.roll(tp, RET_DK // 2, 1)
            outs.append(tp * cosv + jnp.where(low_half, fwd, bwd) * sinv)
        return jnp.concatenate(outs, axis=1)

    for ci in range(ts // RET_CHUNK):
        rows = slice(ci * RET_CHUNK, (ci + 1) * RET_CHUNK)
        cosv = cos_ref[rows, :]
        sinv = sin_ref[rows, :]
        qr = rotary(proj_ref[rows, OFF_Q:OFF_Q + QK_WIDTH].astype(F32), cosv, sinv)
        kr = rotary(proj_ref[rows, OFF_K:OFF_K + QK_WIDTH].astype(F32), cosv, sinv) * (RET_DK ** -0.5)
        qb = qr.astype(BF16)
        qx = (qr * xi_ref[...]).astype(BF16)
        kb = kr.astype(BF16)
        kz = (kr * zeta_ref[...]).astype(BF16)
        for h in range(RET_HEADS):
            lanes = slice((h // 2) * LANES, (h // 2 + 1) * LANES)
            m = head_masks[h % 2]
            zero = jnp.zeros((), BF16)
            qm = jnp.where(m, qb[:, lanes], zero)
            qxm = jnp.where(m, qx[:, lanes], zero)
            vh = proj_ref[rows, OFF_V + h * RET_DV:OFF_V + (h + 1) * RET_DV]
            scores = _dot_nt(qm, kb[:, lanes]) * decay_ref[h]
            st = state_ref[h]
            o = _dot(scores.astype(BF16), vh) + _dot(qxm, st.astype(BF16))
            state_ref[h] = chunk_decay[h] * st + _dot_tn(kz[:, lanes], vh)
            mu = jnp.mean(o, axis=-1, keepdims=True)
            dlt = o - mu
            var = jnp.mean(dlt * dlt, axis=-1, keepdims=True)
            g = proj_ref[rows, OFF_G + h * RET_DV:OFF_G + (h + 1) * RET_DV].astype(F32)
            yin_ref[rows, h * RET_DV:(h + 1) * RET_DV] = (
                g * _sigmoid(g) * (dlt * lax.rsqrt(var + EPS))).astype(BF16)

    y_ret = _dot(yin_ref[...], wr_ref[...])
    gate_c = proj_ref[:, OFF_GATE_C:OFF_GATE_C + D_MODEL].astype(F32)
    gate_r = proj_ref[:, OFF_GATE_R:OFF_GATE_R + D_MODEL].astype(F32)
    merged = _sigmoid(gate_c) * y_conv + _sigmoid(gate_r) * y_ret
    o_ref[...] = x_ref[...] + _dot(merged.astype(BF16), wm_ref[...])


def _mixer(proj, x2, conv_w, wc_bf, wr_bf, wm_bf, batch, seq):
    ts = MIX_TS
    ns = seq // ts
    cos_t, sin_t, xi_t, zeta_t, decay, chunk_decay = _retention_constants(seq)
    full = lambda shape: pl.BlockSpec(shape, lambda b, s: (0,) * len(shape))
    return pl.pallas_call(
        functools.partial(_mixer_kernel, chunk_decay),
        out_shape=jax.ShapeDtypeStruct((batch * seq, D_MODEL), F32),
        grid=(batch, ns),
        in_specs=[
            pl.BlockSpec((ts, IN_WIDTH), lambda b, s: (b * ns + s, 0)),
            pl.BlockSpec((ts, D_MODEL), lambda b, s: (b * ns + s, 0)),
            pl.BlockSpec((ts, LANES), lambda b, s: (s, 0)),
            pl.BlockSpec((ts, LANES), lambda b, s: (s, 0)),
            full((RET_CHUNK, QK_WIDTH)),
            full((RET_CHUNK, QK_WIDTH)),
            full((RET_HEADS, RET_CHUNK, RET_CHUNK)),
            full((CONV_K, CONV_WIDTH)),
            full((CONV_WIDTH, D_MODEL)),
            full((V_WIDTH, D_MODEL)),
            full((D_MODEL, D_MODEL)),
        ],
        out_specs=pl.BlockSpec((ts, D_MODEL), lambda b, s: (b * ns + s, 0)),
        scratch_shapes=[
            pltpu.VMEM((RET_HEADS, LANES, RET_DV), F32),
            pltpu.VMEM((SUBLANES, CONV_WIDTH), F32),
            pltpu.VMEM((ts, V_WIDTH), BF16),
        ],
        compiler_params=pltpu.CompilerParams(
            dimension_semantics=("arbitrary", "arbitrary"), vmem_limit_bytes=VMEM_LIMIT),
        name="mixer",
    )(proj, x2, cos_t, sin_t, xi_t, zeta_t, decay, conv_w, wc_bf, wr_bf, wm_bf)


def _memkv_kernel(m_ref, g_ref, w_ref, o_ref):
    o_ref[0] = _dot(_rms(m_ref[0], g_ref[...]).astype(BF16), w_ref[...]).astype(BF16)


def _mem_kv(mem, g, wkv_bf):
    b = mem.shape[0]
    return pl.pallas_call(
        _memkv_kernel,
        out_shape=jax.ShapeDtypeStruct((b, MEM_LEN, 2 * D_MODEL), BF16),
        grid=(b,),
        in_specs=[
            pl.BlockSpec((1, MEM_LEN, D_MODEL), lambda i: (i, 0, 0)),
            pl.BlockSpec((1, D_MODEL), lambda i: (0, 0)),
            pl.BlockSpec((D_MODEL, 2 * D_MODEL), lambda i: (0, 0)),
        ],
        out_specs=pl.BlockSpec((1, MEM_LEN, 2 * D_MODEL), lambda i: (i, 0, 0)),
        compiler_params=pltpu.CompilerParams(
            dimension_semantics=("arbitrary",), vmem_limit_bytes=VMEM_LIMIT),
        name="mem_kv",
    )(mem, g, wkv_bf)


def _xattn_kernel(h_ref, kv_ref, gx_ref, wq_ref, wo_ref, gm_ref, rwh_ref, rwl_ref, rb_ref,
                  h2_ref, xn2_ref, rid_ref, rwt_ref):
    ts = h_ref.shape[0]
    h = h_ref[...]
    q = _dot(_rms(h, gx_ref[...]).astype(BF16), wq_ref[...])
    outs = []
    for hd in range(XA_HEADS):
        cols = slice(hd * XA_HEAD_DIM, (hd + 1) * XA_HEAD_DIM)
        kh = kv_ref[0, :, cols]
        vh = kv_ref[0, :, D_MODEL + hd * XA_HEAD_DIM:D_MODEL + (hd + 1) * XA_HEAD_DIM]
        s = _dot_nt(q[:, cols].astype(BF16), kh) * (XA_HEAD_DIM ** -0.5)
        e = jnp.exp(s - jnp.max(s, axis=-1, keepdims=True))
        p = e / jnp.sum(e, axis=-1, keepdims=True)
        outs.append(_dot(p.astype(BF16), vh))
    h2 = h + _dot(jnp.concatenate(outs, axis=1).astype(BF16), wo_ref[...])
    h2_ref[...] = h2
    xn = _rms(h2, gm_ref[...])
    xn2_ref[...] = xn

    hi = xn.astype(BF16)
    lo = (xn - hi.astype(F32)).astype(BF16)
    lg = (_dot_nt(rwh_ref[...], hi) + _dot_nt(rwh_ref[...], lo) + _dot_nt(rwl_ref[...], hi)
          + rb_ref[:, 0:1])

    gl = lg[0:N_GROUPS]
    ge = jnp.exp(gl - jnp.max(gl, axis=0, keepdims=True))
    gp = ge / jnp.sum(ge, axis=0, keepdims=True)
    p_g = gp[0:1]
    g_idx = jnp.zeros((1, ts), jnp.int32)
    for i in range(1, N_GROUPS):
        better = gp[i:i + 1] > p_g
        g_idx = jnp.where(better, i, g_idx)
        p_g = jnp.where(better, gp[i:i + 1], p_g)
    sel = jnp.zeros((EXPERTS_PER_GROUP, ts), F32)
    for i in range(N_GROUPS):
        r0 = ROUTE_E0 + i * EXPERTS_PER_GROUP
        sel = jnp.where(g_idx == i, lg[r0:r0 + EXPERTS_PER_GROUP], sel)
    se = jnp.exp(sel - jnp.max(sel, axis=0, keepdims=True))
    sp = se / jnp.sum(se, axis=0, keepdims=True)
    ridx = lax.broadcasted_iota(jnp.int32, (EXPERTS_PER_GROUP, ts), 0)
    m1 = jnp.max(sp, axis=0, keepdims=True)
    i1 = jnp.min(jnp.where(sp == m1, ridx, EXPERTS_PER_GROUP), axis=0, keepdims=True)
    sp2 = jnp.where(ridx == i1, -1.0, sp)
    m2 = jnp.max(sp2, axis=0, keepdims=True)
    i2 = jnp.min(jnp.where(sp2 == m2, ridx, EXPERTS_PER_GROUP), axis=0, keepdims=True)
    den = m1 + m2
    rid_ref[...] = jnp.zeros_like(rid_ref)
    rwt_ref[...] = jnp.zeros_like(rwt_ref)
    rid_ref[0:1, :] = g_idx * EXPERTS_PER_GROUP + i1
    rid_ref[1:2, :] = g_idx * EXPERTS_PER_GROUP + i2
    rwt_ref[0:1, :] = p_g * m1 / den
    rwt_ref[1:2, :] = p_g * m2 / den


def _xattn(h1, kv, gx, wq_bf, wo_bf, gm, rwh, rwl, rb, batch, seq):
    ts = XA_TS
    ns = seq // ts
    t = batch * seq
    full = lambda shape: pl.BlockSpec(shape, lambda b, s: (0,) * len(shape))
    return pl.pallas_call(
        _xattn_kernel,
        out_shape=(
            jax.ShapeDtypeStruct((t, D_MODEL), F32),
            jax.ShapeDtypeStruct((t, D_MODEL), F32),
            jax.ShapeDtypeStruct((SUBLANES, t), jnp.int32),
            jax.ShapeDtypeStruct((SUBLANES, t), F32),
        ),
        grid=(batch, ns),
        in_specs=[
            pl.BlockSpec((ts, D_MODEL), lambda b, s: (b * ns + s, 0)),
            pl.BlockSpec((1, MEM_LEN, 2 * D_MODEL), lambda b, s: (b, 0, 0)),
            full((1, D_MODEL)),
            full((D_MODEL, D_MODEL)),
            full((D_MODEL, D_MODEL)),
            full((1, D_MODEL)),
            full((ROUTE_ROWS, D_MODEL)),
            full((ROUTE_ROWS, D_MODEL)),
            full((ROUTE_ROWS, LANES)),
        ],
        out_specs=(
            pl.BlockSpec((ts, D_MODEL), lambda b, s: (b * ns + s, 0)),
            pl.BlockSpec((ts, D_MODEL), lambda b, s: (b * ns + s, 0)),
            pl.BlockSpec((SUBLANES, ts), lambda b, s: (0, b * ns + s)),
            pl.BlockSpec((SUBLANES, ts), lambda b, s: (0, b * ns + s)),
        ),
        compiler_params=pltpu.CompilerParams(
            dimension_semantics=("arbitrary", "arbitrary"), vmem_limit_bytes=VMEM_LIMIT),
        name="xattn",
    )(h1, kv, gx, wq_bf, wo_bf, gm, rwh, rwl, rb)


def _row_copy(src_ref, src_row, dst_ref, dst_row, sem):
    return pltpu.make_async_copy(src_ref.at[pl.ds(src_row, 1)], dst_ref.at[pl.ds(dst_row, 1)], sem)


def _dispatch_kernel(last_ref, has_ref, nu_ref, pos_ref, xn_ref, xs_ref, zero_ref, sem, zsem):
    td = xn_ref.shape[0]
    n_blocks = xs_ref.shape[0] // MOE_M

    @pl.when(pl.program_id(0) == 0)
    def _():
        zero_ref[...] = jnp.zeros_like(zero_ref)

        def fill(start):
            start = pl.multiple_of(start, MOE_M)
            return pltpu.make_async_copy(zero_ref, xs_ref.at[pl.ds(start, MOE_M)], zsem)

        for e in range(N_EXPERTS):
            @pl.when(has_ref[e] > 0)
            def _():
                fill(last_ref[e]).start()
        for e in range(N_EXPERTS):
            @pl.when(has_ref[e] > 0)
            def _():
                fill(last_ref[e]).wait()

        def fill_tail(b, carry):
            fill(b * MOE_M).start()
            fill(b * MOE_M).wait()
            return carry

        lax.fori_loop(nu_ref[0], n_blocks, fill_tail, 0)

    for r in range(td):
        for k in range(TOP_K):
            _row_copy(xn_ref, r, xs_ref, pos_ref[0, k, r], sem).start()
    for k in range(TOP_K):
        pltpu.make_async_copy(xn_ref, xs_ref.at[pl.ds(0, td)], sem).wait()


def _dispatch(last_blk, has, n_used, pos3, xn2, n_rows):
    t = xn2.shape[0]
    td = DISP_T
    return pl.pallas_call(
        _dispatch_kernel,
        out_shape=jax.ShapeDtypeStruct((n_rows, D_MODEL), F32),
        grid_spec=pltpu.PrefetchScalarGridSpec(
            num_scalar_prefetch=3,
            grid=(t // td,),
            in_specs=[
                pl.BlockSpec((1, TOP_K, td), lambda i, *_: (i, 0, 0), memory_space=pltpu.SMEM),
                pl.BlockSpec((td, D_MODEL), lambda i, *_: (i, 0)),
            ],
            out_specs=pl.BlockSpec(memory_space=pl.ANY),
            scratch_shapes=[
                pltpu.VMEM((MOE_M, D_MODEL), F32),
                pltpu.SemaphoreType.DMA,
                pltpu.SemaphoreType.DMA,
            ],
        ),
        compiler_params=pltpu.CompilerParams(
            dimension_semantics=("arbitrary",), vmem_limit_bytes=VMEM_LIMIT),
        name="dispatch",
    )(last_blk, has, n_used, pos3, xn2)


def _expert_kernel(be_ref, nu_ref, xs_ref, wg_ref, wu_ref, wd_ref, ys_ref, wgb, wub, wdb):
    i = pl.program_id(0)

    @pl.when((i == 0) | (be_ref[i] != be_ref[jnp.maximum(i - 1, 0)]))
    def _():
        wgb[...] = wg_ref[0].astype(BF16)
        wub[...] = wu_ref[0].astype(BF16)
        wdb[...] = wd_ref[0].astype(BF16)

    @pl.when(i < nu_ref[0])
    def _():
        x = xs_ref[...].astype(BF16)
        g = _dot(x, wgb[...])
        u = _dot(x, wub[...])
        ys_ref[...] = _dot((g * _sigmoid(g) * u).astype(BF16), wdb[...])

    @pl.when(i >= nu_ref[0])
    def _():
        ys_ref[...] = jnp.zeros_like(ys_ref)


def _experts(block_e, n_used, xs, w_gate, w_up, w_down):
    n_rows = xs.shape[0]
    row_map = lambda i, be, nu: (jnp.minimum(i, nu[0] - 1), 0)
    out_map = lambda i, be, nu: (i, 0)
    w_map = lambda i, be, nu: (be[i], 0, 0)
    return pl.pallas_call(
        _expert_kernel,
        out_shape=jax.ShapeDtypeStruct((n_rows, D_MODEL), F32),
        grid_spec=pltpu.PrefetchScalarGridSpec(
            num_scalar_prefetch=2,
            grid=(n_rows // MOE_M,),
            in_specs=[
                pl.BlockSpec((MOE_M, D_MODEL), row_map),
                pl.BlockSpec((1, D_MODEL, EXPERT_HIDDEN), w_map),
                pl.BlockSpec((1, D_MODEL, EXPERT_HIDDEN), w_map),
                pl.BlockSpec((1, EXPERT_HIDDEN, D_MODEL), w_map),
            ],
            out_specs=pl.BlockSpec((MOE_M, D_MODEL), out_map),
            scratch_shapes=[
                pltpu.VMEM((D_MODEL, EXPERT_HIDDEN), BF16),
                pltpu.VMEM((D_MODEL, EXPERT_HIDDEN), BF16),
                pltpu.VMEM((EXPERT_HIDDEN, D_MODEL), BF16),
            ],
        ),
        compiler_params=pltpu.CompilerParams(
            dimension_semantics=("arbitrary",), vmem_limit_bytes=VMEM_LIMIT),
        name="experts",
    )(block_e, n_used, xs, w_gate, w_up, w_down)


def _combine_kernel(final_norm, pos_ref, h2_ref, wt_ref, g_ref, ys_ref, o_ref, buf_ref, sem):
    tc = h2_ref.shape[0]
    for r in range(tc):
        for k in range(TOP_K):
            _row_copy(ys_ref, pos_ref[0, k, r], buf_ref.at[k], r, sem).start()
    for k in range(TOP_K):
        pltpu.make_async_copy(ys_ref.at[pl.ds(0, tc)], buf_ref.at[k], sem).wait()
    y = h2_ref[...] + wt_ref[:, 0:1] * buf_ref[0] + wt_ref[:, 1:2] * buf_ref[1]
    o_ref[...] = _rms(y, g_ref[...]) if final_norm else y


def _combine(pos3, h2, wt, g, ys, final_norm):
    t = h2.shape[0]
    tc = COMB_T
    return pl.pallas_call(
        functools.partial(_combine_kernel, final_norm),
        out_shape=jax.ShapeDtypeStruct((t, D_MODEL), F32),
        grid=(t // tc,),
        in_specs=[
            pl.BlockSpec((1, TOP_K, tc), lambda i: (i, 0, 0), memory_space=pltpu.SMEM),
            pl.BlockSpec((tc, D_MODEL), lambda i: (i, 0)),
            pl.BlockSpec((tc, TOP_K), lambda i: (i, 0)),
            pl.BlockSpec((1, D_MODEL), lambda i: (0, 0)),
            pl.BlockSpec(memory_space=pl.ANY),
        ],
        out_specs=pl.BlockSpec((tc, D_MODEL), lambda i: (i, 0)),
        scratch_shapes=[
            pltpu.VMEM((TOP_K, tc, D_MODEL), F32),
            pltpu.SemaphoreType.DMA,
        ],
        compiler_params=pltpu.CompilerParams(
            dimension_semantics=("arbitrary",), vmem_limit_bytes=VMEM_LIMIT),
        name="combine",
    )(pos3, h2, wt, g, ys)


def _routing_tables(rid, n_tok):
    e_flat = rid[:TOP_K].reshape(-1)
    onehot = (e_flat[None, :] == jnp.arange(N_EXPERTS, dtype=jnp.int32)[:, None]).astype(jnp.int32)
    csum = jnp.cumsum(onehot, axis=1)
    counts = csum[:, -1]
    rank = jnp.sum(onehot * csum, axis=0) - 1
    padded = (counts + MOE_M - 1) // MOE_M * MOE_M
    pend = jnp.cumsum(padded)
    pstart = pend - padded
    pos = jnp.sum(onehot * pstart[:, None], axis=0) + rank
    n_blocks = (TOP_K * n_tok + N_EXPERTS * (MOE_M - 1) + MOE_M - 1) // MOE_M
    n_used = pend[-1] // MOE_M
    blk = jnp.minimum(jnp.arange(n_blocks, dtype=jnp.int32), n_used - 1)
    block_e = jnp.sum((pend[None, :] <= (blk * MOE_M)[:, None]).astype(jnp.int32), axis=1)
    block_e = jnp.minimum(block_e, N_EXPERTS - 1)
    last_blk = jnp.maximum(pend - MOE_M, 0).astype(jnp.int32)
    has = (counts > 0).astype(jnp.int32)
    return (pos.reshape(TOP_K, n_tok).astype(jnp.int32), block_e.astype(jnp.int32),
            n_used.reshape(1).astype(jnp.int32), last_blk, has, n_blocks * MOE_M)


def _tile_pos(pos, tile):
    k, t = pos.shape
    return pos.reshape(k, t // tile, tile).transpose(1, 0, 2)


def kernel(x, mem, mix_norm_g, w_in, conv_w, w_conv_out, w_ret_out, w_mix_out, xa_norm_g, mem_norm_g,
           w_xa_q, w_xa_kv, w_xa_o, moe_norm_g, w_group, b_group, w_router, b_router, w_gate, w_up,
           w_down, final_norm_g):
    batch, seq, d = x.shape
    depth = w_in.shape[0]
    t = batch * seq
    h = x.reshape(t, d)
    for l in range(depth):
        proj = _in_proj(h, mix_norm_g[l][None], w_in[l].astype(BF16))
        h1 = _mixer(proj, h, conv_w[l], w_conv_out[l].astype(BF16), w_ret_out[l].astype(BF16),
                    w_mix_out[l].astype(BF16), batch, seq)
        kv = _mem_kv(mem, mem_norm_g[l][None], w_xa_kv[l].astype(BF16))

        rw = jnp.zeros((ROUTE_ROWS, d), F32)
        rw = rw.at[0:N_GROUPS].set(w_group[l].T).at[ROUTE_E0:ROUTE_E0 + N_EXPERTS].set(w_router[l].T)
        rwh = rw.astype(BF16)
        rwl = (rw - rwh.astype(F32)).astype(BF16)
        rb = jnp.zeros((ROUTE_ROWS,), F32)
        rb = rb.at[0:N_GROUPS].set(b_group[l]).at[ROUTE_E0:ROUTE_E0 + N_EXPERTS].set(b_router[l])
        rb = jnp.broadcast_to(rb[:, None], (ROUTE_ROWS, LANES))

        h2, xn2, rid, rwt = _xattn(h1, kv, xa_norm_g[l][None], w_xa_q[l].astype(BF16),
                                   w_xa_o[l].astype(BF16), moe_norm_g[l][None], rwh, rwl, rb, batch, seq)

        pos, block_e, n_used, last_blk, has, n_rows = _routing_tables(rid, t)
        xs = _dispatch(last_blk, has, n_used, _tile_pos(pos, DISP_T), xn2, n_rows)
        ys = _experts(block_e, n_used, xs, w_gate[l], w_up[l], w_down[l])
        h = _combine(_tile_pos(pos, COMB_T), h2, rwt[:TOP_K].T, final_norm_g[None], ys,
                     final_norm=(l == depth - 1))
    return h.reshape(batch, seq, d)
```

```python
import functools

import numpy as np
import jax
import jax.numpy as jnp
from jax import lax
from jax.experimental import pallas as pl
from jax.experimental.pallas import tpu as pltpu

F32 = jnp.float32
BF16 = jnp.bfloat16

D_MODEL = 1024
CONV_WIDTH = 512
CONV_K = 3
RET_HEADS = 8
RET_DK = 64
RET_DV = 128
RET_CHUNK = 128
QK_WIDTH = RET_HEADS * RET_DK
V_WIDTH = RET_HEADS * RET_DV
ROPE_BASE = 10000.0
IN_WIDTH = 3 * CONV_WIDTH + 2 * QK_WIDTH + 2 * V_WIDTH + 2 * D_MODEL
OFF_XIN = 0
OFF_BG = OFF_XIN + CONV_WIDTH
OFF_CG = OFF_BG + CONV_WIDTH
OFF_Q = OFF_CG + CONV_WIDTH
OFF_K = OFF_Q + QK_WIDTH
OFF_V = OFF_K + QK_WIDTH
OFF_G = OFF_V + V_WIDTH
OFF_GATE_C = OFF_G + V_WIDTH
OFF_GATE_R = OFF_GATE_C + D_MODEL
MEM_LEN = 256
XA_HEADS = 4
XA_HEAD_DIM = D_MODEL // XA_HEADS
N_GROUPS = 4
EXPERTS_PER_GROUP = 8
N_EXPERTS = N_GROUPS * EXPERTS_PER_GROUP
TOP_K = 2
EXPERT_HIDDEN = D_MODEL // 2
EPS = 1e-6

LANES = 128
SUBLANES = 8
VMEM_LIMIT = 56 * 1024 * 1024

IN_TM = 2048
IN_TN = 1664
MIX_TS = 512
XA_TS = 512
ROUTE_ROWS = 40
ROUTE_E0 = 8
MOE_M = 256
DISP_T = 256
COMB_T = 256


def _rms(x, g):
    ms = jnp.mean(x * x, axis=-1, keepdims=True)
    return x * lax.rsqrt(ms + EPS) * g


def _sigmoid(x):
    return 1.0 / (1.0 + jnp.exp(-x))


def _dot(a, b):
    return jnp.dot(a, b, preferred_element_type=F32)


def _dot_nt(a, b):
    return lax.dot_general(a, b, (((1,), (1,)), ((), ())), preferred_element_type=F32)


def _dot_tn(a, b):
    return lax.dot_general(a, b, (((0,), (0,)), ((), ())), preferred_element_type=F32)


def _inproj_kernel(x_ref, g_ref, w_ref, o_ref, xn_ref):
    @pl.when(pl.program_id(1) == 0)
    def _():
        xn_ref[...] = _rms(x_ref[...], g_ref[...]).astype(BF16)

    o_ref[...] = _dot(xn_ref[...], w_ref[...]).astype(BF16)


def _in_proj(x2, g, w_bf):
    t = x2.shape[0]
    return pl.pallas_call(
        _inproj_kernel,
        out_shape=jax.ShapeDtypeStruct((t, IN_WIDTH), BF16),
        grid=(t // IN_TM, IN_WIDTH // IN_TN),
        in_specs=[
            pl.BlockSpec((IN_TM, D_MODEL), lambda i, j: (i, 0)),
            pl.BlockSpec((1, D_MODEL), lambda i, j: (0, 0)),
            pl.BlockSpec((D_MODEL, IN_TN), lambda i, j: (0, j)),
        ],
        out_specs=pl.BlockSpec((IN_TM, IN_TN), lambda i, j: (i, j)),
        scratch_shapes=[pltpu.VMEM((IN_TM, D_MODEL), BF16)],
        compiler_params=pltpu.CompilerParams(
            dimension_semantics=("arbitrary", "arbitrary"), vmem_limit_bytes=VMEM_LIMIT),
        name="in_proj",
    )(x2, g, w_bf)


def _retention_constants(seq):
    pos = np.arange(seq, dtype=np.float64)
    inv_freq = ROPE_BASE ** (-np.arange(0, RET_DK, 2, dtype=np.float64) / RET_DK)
    ang = pos[:, None] * inv_freq[None, :]
    cos, sin = np.cos(ang), np.sin(ang)
    cos_t = np.concatenate([cos, cos, cos, cos], axis=1)
    sin_t = np.concatenate([-sin, sin, -sin, sin], axis=1)
    log_g = np.log(1.0 - 2.0 ** (-5.0 - np.arange(RET_HEADS, dtype=np.float64)))
    idx = np.arange(RET_CHUNK, dtype=np.float64)
    diff = idx[:, None] - idx[None, :]
    decay = np.where(diff >= 0, np.exp(np.maximum(diff, 0.0)[None] * log_g[:, None, None]), 0.0)
    zeta = np.exp((RET_CHUNK - 1 - idx)[None, :] * log_g[:, None])
    xi = np.exp((idx + 1)[None, :] * log_g[:, None])
    zeta_t = np.repeat(zeta.T, RET_DK, axis=1)
    xi_t = np.repeat(xi.T, RET_DK, axis=1)
    chunk_decay = np.exp(RET_CHUNK * log_g)
    f = lambda a: jnp.asarray(a, dtype=F32)
    return f(cos_t), f(sin_t), f(xi_t), f(zeta_t), f(decay), [float(c) for c in chunk_decay]


def _mixer_kernel(chunk_decay, proj_ref, x_ref, cos_ref, sin_ref, xi_ref, zeta_ref, decay_ref,
                  convw_ref, wc_ref, wr_ref, wm_ref, o_ref, state_ref, tail_ref, yin_ref):
    ts = x_ref.shape[0]

    @pl.when(pl.program_id(1) == 0)
    def _():
        state_ref[...] = jnp.zeros_like(state_ref)
        tail_ref[...] = jnp.zeros_like(tail_ref)

    xin = proj_ref[:, OFF_XIN:OFF_XIN + CONV_WIDTH].astype(F32)
    bg = proj_ref[:, OFF_BG:OFF_BG + CONV_WIDTH].astype(F32)
    cg = proj_ref[:, OFF_CG:OFF_CG + CONV_WIDTH].astype(F32)
    u = cg * xin
    ue = jnp.concatenate([tail_ref[...], u], axis=0)
    u1 = pltpu.roll(ue, 1, 0)[SUBLANES:]
    u2 = pltpu.roll(ue, 2, 0)[SUBLANES:]
    tail_ref[...] = u[ts - SUBLANES:]
    c = convw_ref[2:3, :] * u + convw_ref[1:2, :] * u1 + convw_ref[0:1, :] * u2
    y_conv = _dot((bg * c).astype(BF16), wc_ref[...])

    lane = lax.broadcasted_iota(jnp.int32, (1, LANES), 1)
    low_half = (lane % RET_DK) < (RET_DK // 2)
    head_masks = [(lane // RET_DK) == j for j in range(LANES // RET_DK)]

    def rotary(t, cosv, sinv):
        outs = []
        for p in range(QK_WIDTH // LANES):
            tp = t[:, p * LANES:(p + 1) * LANES]
            fwd = pltpu.roll(tp, LANES - RET_DK // 2, 1)
            bwd = pltpu.roll(tp, RET_DK // 2, 1)
            outs.append(tp * cosv + jnp.where(low_half, fwd, bwd) * sinv)
        return jnp.concatenate(outs, axis=1)

    for ci in range(ts // RET_CHUNK):
        rows = slice(ci * RET_CHUNK, (ci + 1) * RET_CHUNK)
        cosv = cos_ref[rows, :]
        sinv = sin_ref[rows, :]
        qr = rotary(proj_ref[rows, OFF_Q:OFF_Q + QK_WIDTH].astype(F32), cosv, sinv)
        kr = rotary(proj_ref[rows, OFF_K:OFF_K + QK_WIDTH].astype(F32), cosv, sinv) * (RET_DK ** -0.5)
        qb = qr.astype(BF16)
        qx = (qr * xi_ref[...]).astype(BF16)
        kb = kr.astype(BF16)
        kz = (kr * zeta_ref[...]).astype(BF16)
        for h in range(RET_HEADS):
            lanes = slice((h // 2) * LANES, (h // 2 + 1) * LANES)
            m = head_masks[h % 2]
            zero = jnp.zeros((), BF16)
            qm = jnp.where(m, qb[:, lanes], zero)
            qxm = jnp.where(m, qx[:, lanes], zero)
            vh = proj_ref[rows, OFF_V + h * RET_DV:OFF_V + (h + 1) * RET_DV]
            scores = _dot_nt(qm, kb[:, lanes]) * decay_ref[h]
            st = state_ref[h]
            o = _dot(scores.astype(BF16), vh) + _dot(qxm, st.astype(BF16))
            state_ref[h] = chunk_decay[h] * st + _dot_tn(kz[:, lanes], vh)
            mu = jnp.mean(o, axis=-1, keepdims=True)
            dlt = o - mu
            var = jnp.mean(dlt * dlt, axis=-1, keepdims=True)
            g = proj_ref[rows, OFF_G + h * RET_DV:OFF_G + (h + 1) * RET_DV].astype(F32)
            yin_ref[rows, h * RET_DV:(h + 1) * RET_DV] = (
                g * _sigmoid(g) * (dlt * lax.rsqrt(var + EPS))).astype(BF16)

    y_ret = _dot(yin_ref[...], wr_ref[...])
    gate_c = proj_ref[:, OFF_GATE_C:OFF_GATE_C + D_MODEL].astype(F32)
    gate_r = proj_ref[:, OFF_GATE_R:OFF_GATE_R + D_MODEL].astype(F32)
    merged = _sigmoid(gate_c) * y_conv + _sigmoid(gate_r) * y_ret
    o_ref[...] = x_ref[...] + _dot(merged.astype(BF16), wm_ref[...])


def _mixer(proj, x2, conv_w, wc_bf, wr_bf, wm_bf, batch, seq):
    ts = MIX_TS
    ns = seq // ts
    cos_t, sin_t, xi_t, zeta_t, decay, chunk_decay = _retention_constants(seq)
    full = lambda shape: pl.BlockSpec(shape, lambda b, s: (0,) * len(shape))
    return pl.pallas_call(
        functools.partial(_mixer_kernel, chunk_decay),
        out_shape=jax.ShapeDtypeStruct((batch * seq, D_MODEL), F32),
        grid=(batch, ns),
        in_specs=[
            pl.BlockSpec((ts, IN_WIDTH), lambda b, s: (b * ns + s, 0)),
            pl.BlockSpec((ts, D_MODEL), lambda b, s: (b * ns + s, 0)),
            pl.BlockSpec((ts, LANES), lambda b, s: (s, 0)),
            pl.BlockSpec((ts, LANES), lambda b, s: (s, 0)),
            full((RET_CHUNK, QK_WIDTH)),
            full((RET_CHUNK, QK_WIDTH)),
            full((RET_HEADS, RET_CHUNK, RET_CHUNK)),
            full((CONV_K, CONV_WIDTH)),
            full((CONV_WIDTH, D_MODEL)),
            full((V_WIDTH, D_MODEL)),
            full((D_MODEL, D_MODEL)),
        ],
        out_specs=pl.BlockSpec((ts, D_MODEL), lambda b, s: (b * ns + s, 0)),
        scratch_shapes=[
            pltpu.VMEM((RET_HEADS, LANES, RET_DV), F32),
            pltpu.VMEM((SUBLANES, CONV_WIDTH), F32),
            pltpu.VMEM((ts, V_WIDTH), BF16),
        ],
        compiler_params=pltpu.CompilerParams(
            dimension_semantics=("arbitrary", "arbitrary"), vmem_limit_bytes=VMEM_LIMIT),
        name="mixer",
    )(proj, x2, cos_t, sin_t, xi_t, zeta_t, decay, conv_w, wc_bf, wr_bf, wm_bf)


def _memkv_kernel(m_ref, g_ref, w_ref, o_ref):
    o_ref[0] = _dot(_rms(m_ref[0], g_ref[...]).astype(BF16), w_ref[...]).astype(BF16)


def _mem_kv(mem, g, wkv_bf):
    b = mem.shape[0]
    return pl.pallas_call(
        _memkv_kernel,
        out_shape=jax.ShapeDtypeStruct((b, MEM_LEN, 2 * D_MODEL), BF16),
        grid=(b,),
        in_specs=[
            pl.BlockSpec((1, MEM_LEN, D_MODEL), lambda i: (i, 0, 0)),
            pl.BlockSpec((1, D_MODEL), lambda i: (0, 0)),
            pl.BlockSpec((D_MODEL, 2 * D_MODEL), lambda i: (0, 0)),
        ],
        out_specs=pl.BlockSpec((1, MEM_LEN, 2 * D_MODEL), lambda i: (i, 0, 0)),
        compiler_params=pltpu.CompilerParams(
            dimension_semantics=("arbitrary",), vmem_limit_bytes=VMEM_LIMIT),
        name="mem_kv",
    )(mem, g, wkv_bf)


def _xattn_kernel(h_ref, kv_ref, gx_ref, wq_ref, wo_ref, gm_ref, rwh_ref, rwl_ref, rb_ref,
                  h2_ref, xn2_ref, rid_ref, rwt_ref):
    ts = h_ref.shape[0]
    h = h_ref[...]
    q = _dot(_rms(h, gx_ref[...]).astype(BF16), wq_ref[...])
    outs = []
    for hd in range(XA_HEADS):
        cols = slice(hd * XA_HEAD_DIM, (hd + 1) * XA_HEAD_DIM)
        kh = kv_ref[0, :, cols]
        vh = kv_ref[0, :, D_MODEL + hd * XA_HEAD_DIM:D_MODEL + (hd + 1) * XA_HEAD_DIM]
        s = _dot_nt(q[:, cols].astype(BF16), kh) * (XA_HEAD_DIM ** -0.5)
        e = jnp.exp(s - jnp.max(s, axis=-1, keepdims=True))
        p = e / jnp.sum(e, axis=-1, keepdims=True)
        outs.append(_dot(p.astype(BF16), vh))
    h2 = h + _dot(jnp.concatenate(outs, axis=1).astype(BF16), wo_ref[...])
    h2_ref[...] = h2
    xn = _rms(h2, gm_ref[...])
    xn2_ref[...] = xn

    hi = xn.astype(BF16)
    lo = (xn - hi.astype(F32)).astype(BF16)
    lg = (_dot_nt(rwh_ref[...], hi) + _dot_nt(rwh_ref[...], lo) + _dot_nt(rwl_ref[...], hi)
          + rb_ref[:, 0:1])

    gl = lg[0:N_GROUPS]
    ge = jnp.exp(gl - jnp.max(gl, axis=0, keepdims=True))
    gp = ge / jnp.sum(ge, axis=0, keepdims=True)
    p_g = gp[0:1]
    g_idx = jnp.zeros((1, ts), jnp.int32)
    for i in range(1, N_GROUPS):
        better = gp[i:i + 1] > p_g
        g_idx = jnp.where(better, i, g_idx)
        p_g = jnp.where(better, gp[i:i + 1], p_g)
    sel = jnp.zeros((EXPERTS_PER_GROUP, ts), F32)
    for i in range(N_GROUPS):
        r0 = ROUTE_E0 + i * EXPERTS_PER_GROUP
        sel = jnp.where(g_idx == i, lg[r0:r0 + EXPERTS_PER_GROUP], sel)
    se = jnp.exp(sel - jnp.max(sel, axis=0, keepdims=True))
    sp = se / jnp.sum(se, axis=0, keepdims=True)
    ridx = lax.broadcasted_iota(jnp.int32, (EXPERTS_PER_GROUP, ts), 0)
    m1 = jnp.max(sp, axis=0, keepdims=True)
    i1 = jnp.min(jnp.where(sp == m1, ridx, EXPERTS_PER_GROUP), axis=0, keepdims=True)
    sp2 = jnp.where(ridx == i1, -1.0, sp)
    m2 = jnp.max(sp2, axis=0, keepdims=True)
    i2 = jnp.min(jnp.where(sp2 == m2, ridx, EXPERTS_PER_GROUP), axis=0, keepdims=True)
    den = m1 + m2
    rid_ref[...] = jnp.zeros_like(rid_ref)
    rwt_ref[...] = jnp.zeros_like(rwt_ref)
    rid_ref[0:1, :] = g_idx * EXPERTS_PER_GROUP + i1
    rid_ref[1:2, :] = g_idx * EXPERTS_PER_GROUP + i2
    rwt_ref[0:1, :] = p_g * m1 / den
    rwt_ref[1:2, :] = p_g * m2 / den


def _xattn(h1, kv, gx, wq_bf, wo_bf, gm, rwh, rwl, rb, batch, seq):
    ts = XA_TS
    ns = seq // ts
    t = batch * seq
    full = lambda shape: pl.BlockSpec(shape, lambda b, s: (0,) * len(shape))
    return pl.pallas_call(
        _xattn_kernel,
        out_shape=(
            jax.ShapeDtypeStruct((t, D_MODEL), F32),
            jax.ShapeDtypeStruct((t, D_MODEL), F32),
            jax.ShapeDtypeStruct((SUBLANES, t), jnp.int32),
            jax.ShapeDtypeStruct((SUBLANES, t), F32),
        ),
        grid=(batch, ns),
        in_specs=[
            pl.BlockSpec((ts, D_MODEL), lambda b, s: (b * ns + s, 0)),
            pl.BlockSpec((1, MEM_LEN, 2 * D_MODEL), lambda b, s: (b, 0, 0)),
            full((1, D_MODEL)),
            full((D_MODEL, D_MODEL)),
            full((D_MODEL, D_MODEL)),
            full((1, D_MODEL)),
            full((ROUTE_ROWS, D_MODEL)),
            full((ROUTE_ROWS, D_MODEL)),
            full((ROUTE_ROWS, LANES)),
        ],
        out_specs=(
            pl.BlockSpec((ts, D_MODEL), lambda b, s: (b * ns + s, 0)),
            pl.BlockSpec((ts, D_MODEL), lambda b, s: (b * ns + s, 0)),
            pl.BlockSpec((SUBLANES, ts), lambda b, s: (0, b * ns + s)),
            pl.BlockSpec((SUBLANES, ts), lambda b, s: (0, b * ns + s)),
        ),
        compiler_params=pltpu.CompilerParams(
            dimension_semantics=("arbitrary", "arbitrary"), vmem_limit_bytes=VMEM_LIMIT),
        name="xattn",
    )(h1, kv, gx, wq_bf, wo_bf, gm, rwh, rwl, rb)


def _row_copy(src_ref, src_row, dst_ref, dst_row, sem):
    return pltpu.make_async_copy(src_ref.at[pl.ds(src_row, 1)], dst_ref.at[pl.ds(dst_row, 1)], sem)


def _dispatch_kernel(last_ref, has_ref, nu_ref, pos_ref, xn_ref, xs_ref, zero_ref, sem, zsem):
    td = xn_ref.shape[0]
    n_blocks = xs_ref.shape[0] // MOE_M

    @pl.when(pl.program_id(0) == 0)
    def _():
        zero_ref[...] = jnp.zeros_like(zero_ref)

        def fill(start):
            start = pl.multiple_of(start, MOE_M)
            return pltpu.make_async_copy(zero_ref, xs_ref.at[pl.ds(start, MOE_M)], zsem)

        for e in range(N_EXPERTS):
            @pl.when(has_ref[e] > 0)
            def _():
                fill(last_ref[e]).start()
        for e in range(N_EXPERTS):
            @pl.when(has_ref[e] > 0)
            def _():
                fill(last_ref[e]).wait()

        def fill_tail(b, carry):
            fill(b * MOE_M).start()
            fill(b * MOE_M).wait()
            return carry

        lax.fori_loop(nu_ref[0], n_blocks, fill_tail, 0)

    for r in range(td):
        for k in range(TOP_K):
            _row_copy(xn_ref, r, xs_ref, pos_ref[0, k, r], sem).start(priority=k)
    for k in range(TOP_K):
        pltpu.make_async_copy(xn_ref, xs_ref.at[pl.ds(0, td)], sem).wait()


def _dispatch(last_blk, has, n_used, pos3, xn2, n_rows):
    t = xn2.shape[0]
    td = DISP_T
    return pl.pallas_call(
        _dispatch_kernel,
        out_shape=jax.ShapeDtypeStruct((n_rows, D_MODEL), F32),
        grid_spec=pltpu.PrefetchScalarGridSpec(
            num_scalar_prefetch=3,
            grid=(t // td,),
            in_specs=[
                pl.BlockSpec((1, TOP_K, td), lambda i, *_: (i, 0, 0), memory_space=pltpu.SMEM),
                pl.BlockSpec((td, D_MODEL), lambda i, *_: (i, 0)),
            ],
            out_specs=pl.BlockSpec(memory_space=pl.ANY),
            scratch_shapes=[
                pltpu.VMEM((MOE_M, D_MODEL), F32),
                pltpu.SemaphoreType.DMA,
                pltpu.SemaphoreType.DMA,
            ],
        ),
        compiler_params=pltpu.CompilerParams(
            dimension_semantics=("arbitrary",), vmem_limit_bytes=VMEM_LIMIT),
        name="dispatch",
    )(last_blk, has, n_used, pos3, xn2)


def _expert_kernel(be_ref, nu_ref, xs_ref, wg_ref, wu_ref, wd_ref, ys_ref, wgb, wub, wdb):
    i = pl.program_id(0)

    @pl.when((i == 0) | (be_ref[i] != be_ref[jnp.maximum(i - 1, 0)]))
    def _():
        wgb[...] = wg_ref[0].astype(BF16)
        wub[...] = wu_ref[0].astype(BF16)
        wdb[...] = wd_ref[0].astype(BF16)

    @pl.when(i < nu_ref[0])
    def _():
        x = xs_ref[...].astype(BF16)
        g = _dot(x, wgb[...])
        u = _dot(x, wub[...])
        ys_ref[...] = _dot((g * _sigmoid(g) * u).astype(BF16), wdb[...])

    @pl.when(i >= nu_ref[0])
    def _():
        ys_ref[...] = jnp.zeros_like(ys_ref)


def _experts(block_e, n_used, xs, w_gate, w_up, w_down):
    n_rows = xs.shape[0]
    row_map = lambda i, be, nu: (jnp.minimum(i, nu[0] - 1), 0)
    out_map = lambda i, be, nu: (i, 0)
    w_map = lambda i, be, nu: (be[i], 0, 0)
    return pl.pallas_call(
        _expert_kernel,
        out_shape=jax.ShapeDtypeStruct((n_rows, D_MODEL), F32),
        grid_spec=pltpu.PrefetchScalarGridSpec(
            num_scalar_prefetch=2,
            grid=(n_rows // MOE_M,),
            in_specs=[
                pl.BlockSpec((MOE_M, D_MODEL), row_map),
                pl.BlockSpec((1, D_MODEL, EXPERT_HIDDEN), w_map),
                pl.BlockSpec((1, D_MODEL, EXPERT_HIDDEN), w_map),
                pl.BlockSpec((1, EXPERT_HIDDEN, D_MODEL), w_map),
            ],
            out_specs=pl.BlockSpec((MOE_M, D_MODEL), out_map),
            scratch_shapes=[
                pltpu.VMEM((D_MODEL, EXPERT_HIDDEN), BF16),
                pltpu.VMEM((D_MODEL, EXPERT_HIDDEN), BF16),
                pltpu.VMEM((EXPERT_HIDDEN, D_MODEL), BF16),
            ],
        ),
        compiler_params=pltpu.CompilerParams(
            dimension_semantics=("arbitrary",), vmem_limit_bytes=VMEM_LIMIT),
        name="experts",
    )(block_e, n_used, xs, w_gate, w_up, w_down)


def _combine_kernel(final_norm, pos_ref, h2_ref, wt_ref, g_ref, ys_ref, o_ref, buf_ref, sem):
    tc = h2_ref.shape[0]
    for r in range(tc):
        for k in range(TOP_K):
            _row_copy(ys_ref, pos_ref[0, k, r], buf_ref.at[k], r, sem).start(priority=k)
    for k in range(TOP_K):
        pltpu.make_async_copy(ys_ref.at[pl.ds(0, tc)], buf_ref.at[k], sem).wait()
    y = h2_ref[...] + wt_ref[:, 0:1] * buf_ref[0] + wt_ref[:, 1:2] * buf_ref[1]
    o_ref[...] = _rms(y, g_ref[...]) if final_norm else y


def _combine(pos3, h2, wt, g, ys, final_norm):
    t = h2.shape[0]
    tc = COMB_T
    return pl.pallas_call(
        functools.partial(_combine_kernel, final_norm),
        out_shape=jax.ShapeDtypeStruct((t, D_MODEL), F32),
        grid=(t // tc,),
        in_specs=[
            pl.BlockSpec((1, TOP_K, tc), lambda i: (i, 0, 0), memory_space=pltpu.SMEM),
            pl.BlockSpec((tc, D_MODEL), lambda i: (i, 0)),
            pl.BlockSpec((tc, TOP_K), lambda i: (i, 0)),
            pl.BlockSpec((1, D_MODEL), lambda i: (0, 0)),
            pl.BlockSpec(memory_space=pl.ANY),
        ],
        out_specs=pl.BlockSpec((tc, D_MODEL), lambda i: (i, 0)),
        scratch_shapes=[
            pltpu.VMEM((TOP_K, tc, D_MODEL), F32),
            pltpu.SemaphoreType.DMA,
        ],
        compiler_params=pltpu.CompilerParams(
            dimension_semantics=("arbitrary",), vmem_limit_bytes=VMEM_LIMIT),
        name="combine",
    )(pos3, h2, wt, g, ys)


def _routing_tables(rid, n_tok):
    e_flat = rid[:TOP_K].reshape(-1)
    onehot = (e_flat[None, :] == jnp.arange(N_EXPERTS, dtype=jnp.int32)[:, None]).astype(jnp.int32)
    csum = jnp.cumsum(onehot, axis=1)
    counts = csum[:, -1]
    rank = jnp.sum(onehot * csum, axis=0) - 1
    padded = (counts + MOE_M - 1) // MOE_M * MOE_M
    pend = jnp.cumsum(padded)
    pstart = pend - padded
    pos = jnp.sum(onehot * pstart[:, None], axis=0) + rank
    n_blocks = (TOP_K * n_tok + N_EXPERTS * (MOE_M - 1) + MOE_M - 1) // MOE_M
    n_used = pend[-1] // MOE_M
    blk = jnp.minimum(jnp.arange(n_blocks, dtype=jnp.int32), n_used - 1)
    block_e = jnp.sum((pend[None, :] <= (blk * MOE_M)[:, None]).astype(jnp.int32), axis=1)
    block_e = jnp.minimum(block_e, N_EXPERTS - 1)
    last_blk = jnp.maximum(pend - MOE_M, 0).astype(jnp.int32)
    has = (counts > 0).astype(jnp.int32)
    return (pos.reshape(TOP_K, n_tok).astype(jnp.int32), block_e.astype(jnp.int32),
            n_used.reshape(1).astype(jnp.int32), last_blk, has, n_blocks * MOE_M)


def _tile_pos(pos, tile):
    k, t = pos.shape
    return pos.reshape(k, t // tile, tile).transpose(1, 0, 2)


def kernel(x, mem, mix_norm_g, w_in, conv_w, w_conv_out, w_ret_out, w_mix_out, xa_norm_g, mem_norm_g,
           w_xa_q, w_xa_kv, w_xa_o, moe_norm_g, w_group, b_group, w_router, b_router, w_gate, w_up,
           w_down, final_norm_g):
    batch, seq, d = x.shape
    depth = w_in.shape[0]
    t = batch * seq
    h = x.reshape(t, d)
    for l in range(depth):
        proj = _in_proj(h, mix_norm_g[l][None], w_in[l].astype(BF16))
        h1 = _mixer(proj, h, conv_w[l], w_conv_out[l].astype(BF16), w_ret_out[l].astype(BF16),
                    w_mix_out[l].astype(BF16), batch, seq)
        kv = _mem_kv(mem, mem_norm_g[l][None], w_xa_kv[l].astype(BF16))

        rw = jnp.zeros((ROUTE_ROWS, d), F32)
        rw = rw.at[0:N_GROUPS].set(w_group[l].T).at[ROUTE_E0:ROUTE_E0 + N_EXPERTS].set(w_router[l].T)
        rwh = rw.astype(BF16)
        rwl = (rw - rwh.astype(F32)).astype(BF16)
        rb = jnp.zeros((ROUTE_ROWS,), F32)
        rb = rb.at[0:N_GROUPS].set(b_group[l]).at[ROUTE_E0:ROUTE_E0 + N_EXPERTS].set(b_router[l])
        rb = jnp.broadcast_to(rb[:, None], (ROUTE_ROWS, LANES))

        h2, xn2, rid, rwt = _xattn(h1, kv, xa_norm_g[l][None], w_xa_q[l].astype(BF16),
                                   w_xa_o[l].astype(BF16), moe_norm_g[l][None], rwh, rwl, rb, batch, seq)

        pos, block_e, n_used, last_blk, has, n_rows = _routing_tables(rid, t)
        xs = _dispatch(last_blk, has, n_used, _tile_pos(pos, DISP_T), xn2, n_rows)
        ys = _experts(block_e, n_used, xs, w_gate[l], w_up[l], w_down[l])
        h = _combine(_tile_pos(pos, COMB_T), h2, rwt[:TOP_K].T, final_norm_g[None], ys,
                     final_norm=(l == depth - 1))
    return h.reshape(batch, seq, d)
```

```python
import functools

import numpy as np
import jax
import jax.numpy as jnp
from jax import lax
from jax.experimental import pallas as pl
from jax.experimental.pallas import tpu as pltpu

F32 = jnp.float32
BF16 = jnp.bfloat16

D_MODEL = 1024
CONV_WIDTH = 512
CONV_K = 3
RET_HEADS = 8
RET_DK = 64
RET_DV = 128
RET_CHUNK = 128
QK_WIDTH = RET_HEADS * RET_DK
V_WIDTH = RET_HEADS * RET_DV
ROPE_BASE = 10000.0
IN_WIDTH = 3 * CONV_WIDTH + 2 * QK_WIDTH + 2 * V_WIDTH + 2 * D_MODEL
OFF_XIN = 0
OFF_BG = OFF_XIN + CONV_WIDTH
OFF_CG = OFF_BG + CONV_WIDTH
OFF_Q = OFF_CG + CONV_WIDTH
OFF_K = OFF_Q + QK_WIDTH
OFF_V = OFF_K + QK_WIDTH
OFF_G = OFF_V + V_WIDTH
OFF_GATE_C = OFF_G + V_WIDTH
OFF_GATE_R = OFF_GATE_C + D_MODEL
MEM_LEN = 256
XA_HEADS = 4
XA_HEAD_DIM = D_MODEL // XA_HEADS
N_GROUPS = 4
EXPERTS_PER_GROUP = 8
N_EXPERTS = N_GROUPS * EXPERTS_PER_GROUP
TOP_K = 2
EXPERT_HIDDEN = D_MODEL // 2
EPS = 1e-6

LANES = 128
SUBLANES = 8
VMEM_LIMIT = 56 * 1024 * 1024

IN_TM = 2048
IN_TN = 1664
MIX_TS = 512
XA_TS = 512
ROUTE_ROWS = 40
ROUTE_E0 = 8
MOE_M = 256
DISP_T = 256
COMB_T = 256


def _rms(x, g):
    ms = jnp.mean(x * x, axis=-1, keepdims=True)
    return x * lax.rsqrt(ms + EPS) * g


def _sigmoid(x):
    return 1.0 / (1.0 + jnp.exp(-x))


def _dot(a, b):
    return jnp.dot(a, b, preferred_element_type=F32)


def _dot_nt(a, b):
    return lax.dot_general(a, b, (((1,), (1,)), ((), ())), preferred_element_type=F32)


def _dot_tn(a, b):
    return lax.dot_general(a, b, (((0,), (0,)), ((), ())), preferred_element_type=F32)


def _inproj_kernel(x_ref, g_ref, w_ref, o_ref, xn_ref):
    @pl.when(pl.program_id(1) == 0)
    def _():
        xn_ref[...] = _rms(x_ref[...], g_ref[...]).astype(BF16)

    o_ref[...] = _dot(xn_ref[...], w_ref[...]).astype(BF16)


def _in_proj(x2, g, w_bf):
    t = x2.shape[0]
    return pl.pallas_call(
        _inproj_kernel,
        out_shape=jax.ShapeDtypeStruct((t, IN_WIDTH), BF16),
        grid=(t // IN_TM, IN_WIDTH // IN_TN),
        in_specs=[
            pl.BlockSpec((IN_TM, D_MODEL), lambda i, j: (i, 0)),
            pl.BlockSpec((1, D_MODEL), lambda i, j: (0, 0)),
            pl.BlockSpec((D_MODEL, IN_TN), lambda i, j: (0, j)),
        ],
        out_specs=pl.BlockSpec((IN_TM, IN_TN), lambda i, j: (i, j)),
        scratch_shapes=[pltpu.VMEM((IN_TM, D_MODEL), BF16)],
        compiler_params=pltpu.CompilerParams(
            dimension_semantics=("arbitrary", "arbitrary"), vmem_limit_bytes=VMEM_LIMIT),
        name="in_proj",
    )(x2, g, w_bf)


def _retention_constants(seq):
    pos = np.arange(seq, dtype=np.float64)
    inv_freq = ROPE_BASE ** (-np.arange(0, RET_DK, 2, dtype=np.float64) / RET_DK)
    ang = pos[:, None] * inv_freq[None, :]
    cos, sin = np.cos(ang), np.sin(ang)
    cos_t = np.concatenate([cos, cos, cos, cos], axis=1)
    sin_t = np.concatenate([-sin, sin, -sin, sin], axis=1)
    log_g = np.log(1.0 - 2.0 ** (-5.0 - np.arange(RET_HEADS, dtype=np.float64)))
    idx = np.arange(RET_CHUNK, dtype=np.float64)
    diff = idx[:, None] - idx[None, :]
    decay = np.where(diff >= 0, np.exp(np.maximum(diff, 0.0)[None] * log_g[:, None, None]), 0.0)
    zeta = np.exp((RET_CHUNK - 1 - idx)[None, :] * log_g[:, None])
    xi = np.exp((idx + 1)[None, :] * log_g[:, None])
    zeta_t = np.repeat(zeta.T, RET_DK, axis=1)
    xi_t = np.repeat(xi.T, RET_DK, axis=1)
    chunk_decay = np.exp(RET_CHUNK * log_g)
    f = lambda a: jnp.asarray(a, dtype=F32)
    return f(cos_t), f(sin_t), f(xi_t), f(zeta_t), f(decay), [float(c) for c in chunk_decay]


def _mixer_kernel(chunk_decay, proj_ref, x_ref, cos_ref, sin_ref, xi_ref, zeta_ref, decay_ref,
                  convw_ref, wc_ref, wr_ref, wm_ref, o_ref, state_ref, tail_ref, yin_ref):
    ts = x_ref.shape[0]

    @pl.when(pl.program_id(1) == 0)
    def _():
        state_ref[...] = jnp.zeros_like(state_ref)
        tail_ref[...] = jnp.zeros_like(tail_ref)

    xin = proj_ref[:, OFF_XIN:OFF_XIN + CONV_WIDTH].astype(F32)
    bg = proj_ref[:, OFF_BG:OFF_BG + CONV_WIDTH].astype(F32)
    cg = proj_ref[:, OFF_CG:OFF_CG + CONV_WIDTH].astype(F32)
    u = cg * xin
    ue = jnp.concatenate([tail_ref[...], u], axis=0)
    u1 = pltpu.roll(ue, 1, 0)[SUBLANES:]
    u2 = pltpu.roll(ue, 2, 0)[SUBLANES:]
    tail_ref[...] = u[ts - SUBLANES:]
    c = convw_ref[2:3, :] * u + convw_ref[1:2, :] * u1 + convw_ref[0:1, :] * u2
    y_conv = _dot((bg * c).astype(BF16), wc_ref[...])

    lane = lax.broadcasted_iota(jnp.int32, (1, LANES), 1)
    low_half = (lane % RET_DK) < (RET_DK // 2)
    head_masks = [(lane // RET_DK) == j for j in range(LANES // RET_DK)]

    def rotary(t, cosv, sinv):
        outs = []
        for p in range(QK_WIDTH // LANES):
            tp = t[:, p * LANES:(p + 1) * LANES]
            fwd = pltpu.roll(tp, LANES - RET_DK // 2, 1)
            bwd = pltpu.roll(tp, RET_DK // 2, 1)
            outs.append(tp * cosv + jnp.where(low_half, fwd, bwd) * sinv)
        return jnp.concatenate(outs, axis=1)

    n_chunks = ts // RET_CHUNK
    chunk_rows = [slice(ci * RET_CHUNK, (ci + 1) * RET_CHUNK) for ci in range(n_chunks)]
    pairs = [(ci, h) for ci in range(n_chunks) for h in range(RET_HEADS)]
    zero = jnp.zeros((), BF16)
    qm, qxm, kb, kz = {}, {}, {}, {}
    for ci, rows in enumerate(chunk_rows):
        cosv = cos_ref[rows, :]
        sinv = sin_ref[rows, :]
        qr = rotary(proj_ref[rows, OFF_Q:OFF_Q + QK_WIDTH].astype(F32), cosv, sinv)
        kr = rotary(proj_ref[rows, OFF_K:OFF_K + QK_WIDTH].astype(F32), cosv, sinv) * (RET_DK ** -0.5)
        qb = qr.astype(BF16)
        qx = (qr * xi_ref[...]).astype(BF16)
        kb[ci] = kr.astype(BF16)
        kz[ci] = (kr * zeta_ref[...]).astype(BF16)
        for h in range(RET_HEADS):
            lanes = slice((h // 2) * LANES, (h // 2 + 1) * LANES)
            qm[ci, h] = jnp.where(head_masks[h % 2], qb[:, lanes], zero)
            qxm[ci, h] = jnp.where(head_masks[h % 2], qx[:, lanes], zero)

    def pair_lanes(h):
        return slice((h // 2) * LANES, (h // 2 + 1) * LANES)

    def v_of(ci, h):
        return proj_ref[chunk_rows[ci], OFF_V + h * RET_DV:OFF_V + (h + 1) * RET_DV]

    scores = {(ci, h): _dot_nt(qm[ci, h], kb[ci][:, pair_lanes(h)]) for ci, h in pairs}
    kv = {(ci, h): _dot_tn(kz[ci][:, pair_lanes(h)], v_of(ci, h)) for ci, h in pairs}
    probs = {(ci, h): (scores[ci, h] * decay_ref[h]).astype(BF16) for ci, h in pairs}
    inner = {(ci, h): _dot(probs[ci, h], v_of(ci, h)) for ci, h in pairs}
    for h in range(RET_HEADS):
        st = state_ref[h]
        for ci in range(n_chunks):
            o = inner[ci, h] + _dot(qxm[ci, h], st.astype(BF16))
            st = chunk_decay[h] * st + kv[ci, h]
            mu = jnp.mean(o, axis=-1, keepdims=True)
            dlt = o - mu
            var = jnp.mean(dlt * dlt, axis=-1, keepdims=True)
            g = proj_ref[chunk_rows[ci], OFF_G + h * RET_DV:OFF_G + (h + 1) * RET_DV].astype(F32)
            yin_ref[chunk_rows[ci], h * RET_DV:(h + 1) * RET_DV] = (
                g * _sigmoid(g) * (dlt * lax.rsqrt(var + EPS))).astype(BF16)
        state_ref[h] = st

    y_ret = _dot(yin_ref[...], wr_ref[...])
    gate_c = proj_ref[:, OFF_GATE_C:OFF_GATE_C + D_MODEL].astype(F32)
    gate_r = proj_ref[:, OFF_GATE_R:OFF_GATE_R + D_MODEL].astype(F32)
    merged = _sigmoid(gate_c) * y_conv + _sigmoid(gate_r) * y_ret
    o_ref[...] = x_ref[...] + _dot(merged.astype(BF16), wm_ref[...])


def _mixer(proj, x2, conv_w, wc_bf, wr_bf, wm_bf, batch, seq):
    ts = MIX_TS
    ns = seq // ts
    cos_t, sin_t, xi_t, zeta_t, decay, chunk_decay = _retention_constants(seq)
    full = lambda shape: pl.BlockSpec(shape, lambda b, s: (0,) * len(shape))
    return pl.pallas_call(
        functools.partial(_mixer_kernel, chunk_decay),
        out_shape=jax.ShapeDtypeStruct((batch * seq, D_MODEL), F32),
        grid=(batch, ns),
        in_specs=[
            pl.BlockSpec((ts, IN_WIDTH), lambda b, s: (b * ns + s, 0)),
            pl.BlockSpec((ts, D_MODEL), lambda b, s: (b * ns + s, 0)),
            pl.BlockSpec((ts, LANES), lambda b, s: (s, 0)),
            pl.BlockSpec((ts, LANES), lambda b, s: (s, 0)),
            full((RET_CHUNK, QK_WIDTH)),
            full((RET_CHUNK, QK_WIDTH)),
            full((RET_HEADS, RET_CHUNK, RET_CHUNK)),
            full((CONV_K, CONV_WIDTH)),
            full((CONV_WIDTH, D_MODEL)),
            full((V_WIDTH, D_MODEL)),
            full((D_MODEL, D_MODEL)),
        ],
        out_specs=pl.BlockSpec((ts, D_MODEL), lambda b, s: (b * ns + s, 0)),
        scratch_shapes=[
            pltpu.VMEM((RET_HEADS, LANES, RET_DV), F32),
            pltpu.VMEM((SUBLANES, CONV_WIDTH), F32),
            pltpu.VMEM((ts, V_WIDTH), BF16),
        ],
        compiler_params=pltpu.CompilerParams(
            dimension_semantics=("arbitrary", "arbitrary"), vmem_limit_bytes=VMEM_LIMIT),
        name="mixer",
    )(proj, x2, cos_t, sin_t, xi_t, zeta_t, decay, conv_w, wc_bf, wr_bf, wm_bf)


def _memkv_kernel(m_ref, g_ref, w_ref, o_ref):
    o_ref[0] = _dot(_rms(m_ref[0], g_ref[...]).astype(BF16), w_ref[...]).astype(BF16)


def _mem_kv(mem, g, wkv_bf):
    b = mem.shape[0]
    return pl.pallas_call(
        _memkv_kernel,
        out_shape=jax.ShapeDtypeStruct((b, MEM_LEN, 2 * D_MODEL), BF16),
        grid=(b,),
        in_specs=[
            pl.BlockSpec((1, MEM_LEN, D_MODEL), lambda i: (i, 0, 0)),
            pl.BlockSpec((1, D_MODEL), lambda i: (0, 0)),
            pl.BlockSpec((D_MODEL, 2 * D_MODEL), lambda i: (0, 0)),
        ],
        out_specs=pl.BlockSpec((1, MEM_LEN, 2 * D_MODEL), lambda i: (i, 0, 0)),
        compiler_params=pltpu.CompilerParams(
            dimension_semantics=("arbitrary",), vmem_limit_bytes=VMEM_LIMIT),
        name="mem_kv",
    )(mem, g, wkv_bf)


def _xattn_kernel(h_ref, kv_ref, gx_ref, wq_ref, wo_ref, gm_ref, rwh_ref, rwl_ref, rb_ref,
                  h2_ref, xn2_ref, rid_ref, rwt_ref):
    ts = h_ref.shape[0]
    h = h_ref[...]
    q = _dot(_rms(h, gx_ref[...]).astype(BF16), wq_ref[...])
    outs = []
    for hd in range(XA_HEADS):
        cols = slice(hd * XA_HEAD_DIM, (hd + 1) * XA_HEAD_DIM)
        kh = kv_ref[0, :, cols]
        vh = kv_ref[0, :, D_MODEL + hd * XA_HEAD_DIM:D_MODEL + (hd + 1) * XA_HEAD_DIM]
        s = _dot_nt(q[:, cols].astype(BF16), kh) * (XA_HEAD_DIM ** -0.5)
        e = jnp.exp(s - jnp.max(s, axis=-1, keepdims=True))
        p = e / jnp.sum(e, axis=-1, keepdims=True)
        outs.append(_dot(p.astype(BF16), vh))
    h2 = h + _dot(jnp.concatenate(outs, axis=1).astype(BF16), wo_ref[...])
    h2_ref[...] = h2
    xn = _rms(h2, gm_ref[...])
    xn2_ref[...] = xn

    hi = xn.astype(BF16)
    lo = (xn - hi.astype(F32)).astype(BF16)
    lg = (_dot_nt(rwh_ref[...], hi) + _dot_nt(rwh_ref[...], lo) + _dot_nt(rwl_ref[...], hi)
          + rb_ref[:, 0:1])

    gl = lg[0:N_GROUPS]
    ge = jnp.exp(gl - jnp.max(gl, axis=0, keepdims=True))
    gp = ge / jnp.sum(ge, axis=0, keepdims=True)
    p_g = gp[0:1]
    g_idx = jnp.zeros((1, ts), jnp.int32)
    for i in range(1, N_GROUPS):
        better = gp[i:i + 1] > p_g
        g_idx = jnp.where(better, i, g_idx)
        p_g = jnp.where(better, gp[i:i + 1], p_g)
    sel = jnp.zeros((EXPERTS_PER_GROUP, ts), F32)
    for i in range(N_GROUPS):
        r0 = ROUTE_E0 + i * EXPERTS_PER_GROUP
        sel = jnp.where(g_idx == i, lg[r0:r0 + EXPERTS_PER_GROUP], sel)
    se = jnp.exp(sel - jnp.max(sel, axis=0, keepdims=True))
    sp = se / jnp.sum(se, axis=0, keepdims=True)
    ridx = lax.broadcasted_iota(jnp.int32, (EXPERTS_PER_GROUP, ts), 0)
    m1 = jnp.max(sp, axis=0, keepdims=True)
    i1 = jnp.min(jnp.where(sp == m1, ridx, EXPERTS_PER_GROUP), axis=0, keepdims=True)
    sp2 = jnp.where(ridx == i1, -1.0, sp)
    m2 = jnp.max(sp2, axis=0, keepdims=True)
    i2 = jnp.min(jnp.where(sp2 == m2, ridx, EXPERTS_PER_GROUP), axis=0, keepdims=True)
    den = m1 + m2
    rid_ref[...] = jnp.zeros_like(rid_ref)
    rwt_ref[...] = jnp.zeros_like(rwt_ref)
    rid_ref[0:1, :] = g_idx * EXPERTS_PER_GROUP + i1
    rid_ref[1:2, :] = g_idx * EXPERTS_PER_GROUP + i2
    rwt_ref[0:1, :] = p_g * m1 / den
    rwt_ref[1:2, :] = p_g * m2 / den


def _xattn(h1, kv, gx, wq_bf, wo_bf, gm, rwh, rwl, rb, batch, seq):
    ts = XA_TS
    ns = seq // ts
    t = batch * seq
    full = lambda shape: pl.BlockSpec(shape, lambda b, s: (0,) * len(shape))
    return pl.pallas_call(
        _xattn_kernel,
        out_shape=(
            jax.ShapeDtypeStruct((t, D_MODEL), F32),
            jax.ShapeDtypeStruct((t, D_MODEL), F32),
            jax.ShapeDtypeStruct((SUBLANES, t), jnp.int32),
            jax.ShapeDtypeStruct((SUBLANES, t), F32),
        ),
        grid=(batch, ns),
        in_specs=[
            pl.BlockSpec((ts, D_MODEL), lambda b, s: (b * ns + s, 0)),
            pl.BlockSpec((1, MEM_LEN, 2 * D_MODEL), lambda b, s: (b, 0, 0)),
            full((1, D_MODEL)),
            full((D_MODEL, D_MODEL)),
            full((D_MODEL, D_MODEL)),
            full((1, D_MODEL)),
            full((ROUTE_ROWS, D_MODEL)),
            full((ROUTE_ROWS, D_MODEL)),
            full((ROUTE_ROWS, LANES)),
        ],
        out_specs=(
            pl.BlockSpec((ts, D_MODEL), lambda b, s: (b * ns + s, 0)),
            pl.BlockSpec((ts, D_MODEL), lambda b, s: (b * ns + s, 0)),
            pl.BlockSpec((SUBLANES, ts), lambda b, s: (0, b * ns + s)),
            pl.BlockSpec((SUBLANES, ts), lambda b, s: (0, b * ns + s)),
        ),
        compiler_params=pltpu.CompilerParams(
            dimension_semantics=("arbitrary", "arbitrary"), vmem_limit_bytes=VMEM_LIMIT),
        name="xattn",
    )(h1, kv, gx, wq_bf, wo_bf, gm, rwh, rwl, rb)


def _row_copy(src_ref, src_row, dst_ref, dst_row, sem):
    return pltpu.make_async_copy(src_ref.at[pl.ds(src_row, 1)], dst_ref.at[pl.ds(dst_row, 1)], sem)


def _dispatch_kernel(last_ref, has_ref, nu_ref, pos_ref, xn_ref, xs_ref, zero_ref, sem, zsem):
    td = xn_ref.shape[0]
    n_blocks = xs_ref.shape[0] // MOE_M

    @pl.when(pl.program_id(0) == 0)
    def _():
        zero_ref[...] = jnp.zeros_like(zero_ref)

        def fill(start):
            start = pl.multiple_of(start, MOE_M)
            return pltpu.make_async_copy(zero_ref, xs_ref.at[pl.ds(start, MOE_M)], zsem)

        for e in range(N_EXPERTS):
            @pl.when(has_ref[e] > 0)
            def _():
                fill(last_ref[e]).start()
        for e in range(N_EXPERTS):
            @pl.when(has_ref[e] > 0)
            def _():
                fill(last_ref[e]).wait()

        def fill_tail(b, carry):
            fill(b * MOE_M).start()
            fill(b * MOE_M).wait()
            return carry

        lax.fori_loop(nu_ref[0], n_blocks, fill_tail, 0)

    for r in range(td):
        for k in range(TOP_K):
            _row_copy(xn_ref, r, xs_ref, pos_ref[0, k, r], sem).start(priority=k)
    for k in range(TOP_K):
        pltpu.make_async_copy(xn_ref, xs_ref.at[pl.ds(0, td)], sem).wait()


def _dispatch(last_blk, has, n_used, pos3, xn2, n_rows):
    t = xn2.shape[0]
    td = DISP_T
    return pl.pallas_call(
        _dispatch_kernel,
        out_shape=jax.ShapeDtypeStruct((n_rows, D_MODEL), F32),
        grid_spec=pltpu.PrefetchScalarGridSpec(
            num_scalar_prefetch=3,
            grid=(t // td,),
            in_specs=[
                pl.BlockSpec((1, TOP_K, td), lambda i, *_: (i, 0, 0), memory_space=pltpu.SMEM),
                pl.BlockSpec((td, D_MODEL), lambda i, *_: (i, 0)),
            ],
            out_specs=pl.BlockSpec(memory_space=pl.ANY),
            scratch_shapes=[
                pltpu.VMEM((MOE_M, D_MODEL), F32),
                pltpu.SemaphoreType.DMA,
                pltpu.SemaphoreType.DMA,
            ],
        ),
        compiler_params=pltpu.CompilerParams(
            dimension_semantics=("arbitrary",), vmem_limit_bytes=VMEM_LIMIT),
        name="dispatch",
    )(last_blk, has, n_used, pos3, xn2)


def _expert_kernel(be_ref, nu_ref, xs_ref, wg_ref, wu_ref, wd_ref, ys_ref, wgb, wub, wdb):
    i = pl.program_id(0)

    @pl.when((i == 0) | (be_ref[i] != be_ref[jnp.maximum(i - 1, 0)]))
    def _():
        wgb[...] = wg_ref[0].astype(BF16)
        wub[...] = wu_ref[0].astype(BF16)
        wdb[...] = wd_ref[0].astype(BF16)

    @pl.when(i < nu_ref[0])
    def _():
        x = xs_ref[...].astype(BF16)
        g = _dot(x, wgb[...])
        u = _dot(x, wub[...])
        ys_ref[...] = _dot((g * _sigmoid(g) * u).astype(BF16), wdb[...])

    @pl.when(i >= nu_ref[0])
    def _():
        ys_ref[...] = jnp.zeros_like(ys_ref)


def _experts(block_e, n_used, xs, w_gate, w_up, w_down):
    n_rows = xs.shape[0]
    row_map = lambda i, be, nu: (jnp.minimum(i, nu[0] - 1), 0)
    out_map = lambda i, be, nu: (i, 0)
    w_map = lambda i, be, nu: (be[i], 0, 0)
    return pl.pallas_call(
        _expert_kernel,
        out_shape=jax.ShapeDtypeStruct((n_rows, D_MODEL), F32),
        grid_spec=pltpu.PrefetchScalarGridSpec(
            num_scalar_prefetch=2,
            grid=(n_rows // MOE_M,),
            in_specs=[
                pl.BlockSpec((MOE_M, D_MODEL), row_map),
                pl.BlockSpec((1, D_MODEL, EXPERT_HIDDEN), w_map),
                pl.BlockSpec((1, D_MODEL, EXPERT_HIDDEN), w_map),
                pl.BlockSpec((1, EXPERT_HIDDEN, D_MODEL), w_map),
            ],
            out_specs=pl.BlockSpec((MOE_M, D_MODEL), out_map),
            scratch_shapes=[
                pltpu.VMEM((D_MODEL, EXPERT_HIDDEN), BF16),
                pltpu.VMEM((D_MODEL, EXPERT_HIDDEN), BF16),
                pltpu.VMEM((EXPERT_HIDDEN, D_MODEL), BF16),
            ],
        ),
        compiler_params=pltpu.CompilerParams(
            dimension_semantics=("arbitrary",), vmem_limit_bytes=VMEM_LIMIT),
        name="experts",
    )(block_e, n_used, xs, w_gate, w_up, w_down)


def _combine_kernel(final_norm, pos_ref, h2_ref, wt_ref, g_ref, ys_ref, o_ref, buf_ref, sem):
    tc = h2_ref.shape[0]
    for r in range(tc):
        for k in range(TOP_K):
            _row_copy(ys_ref, pos_ref[0, k, r], buf_ref.at[k], r, sem).start(priority=k)
    for k in range(TOP_K):
        pltpu.make_async_copy(ys_ref.at[pl.ds(0, tc)], buf_ref.at[k], sem).wait()
    y = h2_ref[...] + wt_ref[:, 0:1] * buf_ref[0] + wt_ref[:, 1:2] * buf_ref[1]
    o_ref[...] = _rms(y, g_ref[...]) if final_norm else y


def _combine(pos3, h2, wt, g, ys, final_norm):
    t = h2.shape[0]
    tc = COMB_T
    return pl.pallas_call(
        functools.partial(_combine_kernel, final_norm),
        out_shape=jax.ShapeDtypeStruct((t, D_MODEL), F32),
        grid=(t // tc,),
        in_specs=[
            pl.BlockSpec((1, TOP_K, tc), lambda i: (i, 0, 0), memory_space=pltpu.SMEM),
            pl.BlockSpec((tc, D_MODEL), lambda i: (i, 0)),
            pl.BlockSpec((tc, TOP_K), lambda i: (i, 0)),
            pl.BlockSpec((1, D_MODEL), lambda i: (0, 0)),
            pl.BlockSpec(memory_space=pl.ANY),
        ],
        out_specs=pl.BlockSpec((tc, D_MODEL), lambda i: (i, 0)),
        scratch_shapes=[
            pltpu.VMEM((TOP_K, tc, D_MODEL), F32),
            pltpu.SemaphoreType.DMA,
        ],
        compiler_params=pltpu.CompilerParams(
            dimension_semantics=("arbitrary",), vmem_limit_bytes=VMEM_LIMIT),
        name="combine",
    )(pos3, h2, wt, g, ys)


def _routing_tables(rid, n_tok):
    e_flat = rid[:TOP_K].reshape(-1)
    onehot = (e_flat[None, :] == jnp.arange(N_EXPERTS, dtype=jnp.int32)[:, None]).astype(jnp.int32)
    csum = jnp.cumsum(onehot, axis=1)
    counts = csum[:, -1]
    rank = jnp.sum(onehot * csum, axis=0) - 1
    padded = (counts + MOE_M - 1) // MOE_M * MOE_M
    pend = jnp.cumsum(padded)
    pstart = pend - padded
    pos = jnp.sum(onehot * pstart[:, None], axis=0) + rank
    n_blocks = (TOP_K * n_tok + N_EXPERTS * (MOE_M - 1) + MOE_M - 1) // MOE_M
    n_used = pend[-1] // MOE_M
    blk = jnp.minimum(jnp.arange(n_blocks, dtype=jnp.int32), n_used - 1)
    block_e = jnp.sum((pend[None, :] <= (blk * MOE_M)[:, None]).astype(jnp.int32), axis=1)
    block_e = jnp.minimum(block_e, N_EXPERTS - 1)
    last_blk = jnp.maximum(pend - MOE_M, 0).astype(jnp.int32)
    has = (counts > 0).astype(jnp.int32)
    return (pos.reshape(TOP_K, n_tok).astype(jnp.int32), block_e.astype(jnp.int32),
            n_used.reshape(1).astype(jnp.int32), last_blk, has, n_blocks * MOE_M)


def _tile_pos(pos, tile):
    k, t = pos.shape
    return pos.reshape(k, t // tile, tile).transpose(1, 0, 2)


def kernel(x, mem, mix_norm_g, w_in, conv_w, w_conv_out, w_ret_out, w_mix_out, xa_norm_g, mem_norm_g,
           w_xa_q, w_xa_kv, w_xa_o, moe_norm_g, w_group, b_group, w_router, b_router, w_gate, w_up,
           w_down, final_norm_g):
    batch, seq, d = x.shape
    depth = w_in.shape[0]
    t = batch * seq
    h = x.reshape(t, d)
    for l in range(depth):
        proj = _in_proj(h, mix_norm_g[l][None], w_in[l].astype(BF16))
        h1 = _mixer(proj, h, conv_w[l], w_conv_out[l].astype(BF16), w_ret_out[l].astype(BF16),
                    w_mix_out[l].astype(BF16), batch, seq)
        kv = _mem_kv(mem, mem_norm_g[l][None], w_xa_kv[l].astype(BF16))

        rw = jnp.zeros((ROUTE_ROWS, d), F32)
        rw = rw.at[0:N_GROUPS].set(w_group[l].T).at[ROUTE_E0:ROUTE_E0 + N_EXPERTS].set(w_router[l].T)
        rwh = rw.astype(BF16)
        rwl = (rw - rwh.astype(F32)).astype(BF16)
        rb = jnp.zeros((ROUTE_ROWS,), F32)
        rb = rb.at[0:N_GROUPS].set(b_group[l]).at[ROUTE_E0:ROUTE_E0 + N_EXPERTS].set(b_router[l])
        rb = jnp.broadcast_to(rb[:, None], (ROUTE_ROWS, LANES))

        h2, xn2, rid, rwt = _xattn(h1, kv, xa_norm_g[l][None], w_xa_q[l].astype(BF16),
                                   w_xa_o[l].astype(BF16), moe_norm_g[l][None], rwh, rwl, rb, batch, seq)

        pos, block_e, n_used, last_blk, has, n_rows = _routing_tables(rid, t)
        xs = _dispatch(last_blk, has, n_used, _tile_pos(pos, DISP_T), xn2, n_rows)
        ys = _experts(block_e, n_used, xs, w_gate[l], w_up[l], w_down[l])
        h = _combine(_tile_pos(pos, COMB_T), h2, rwt[:TOP_K].T, final_norm_g[None], ys,
                     final_norm=(l == depth - 1))
    return h.reshape(batch, seq, d)
```

```python
import functools

import numpy as np
import jax
import jax.numpy as jnp
from jax import lax
from jax.experimental import pallas as pl
from jax.experimental.pallas import tpu as pltpu

F32 = jnp.float32
BF16 = jnp.bfloat16

D_MODEL = 1024
CONV_WIDTH = 512
CONV_K = 3
RET_HEADS = 8
RET_DK = 64
RET_DV = 128
RET_CHUNK = 128
QK_WIDTH = RET_HEADS * RET_DK
V_WIDTH = RET_HEADS * RET_DV
ROPE_BASE = 10000.0
IN_WIDTH = 3 * CONV_WIDTH + 2 * QK_WIDTH + 2 * V_WIDTH + 2 * D_MODEL
OFF_XIN = 0
OFF_BG = OFF_XIN + CONV_WIDTH
OFF_CG = OFF_BG + CONV_WIDTH
OFF_Q = OFF_CG + CONV_WIDTH
OFF_K = OFF_Q + QK_WIDTH
OFF_V = OFF_K + QK_WIDTH
OFF_G = OFF_V + V_WIDTH
OFF_GATE_C = OFF_G + V_WIDTH
OFF_GATE_R = OFF_GATE_C + D_MODEL
MEM_LEN = 256
XA_HEADS = 4
XA_HEAD_DIM = D_MODEL // XA_HEADS
N_GROUPS = 4
EXPERTS_PER_GROUP = 8
N_EXPERTS = N_GROUPS * EXPERTS_PER_GROUP
TOP_K = 2
EXPERT_HIDDEN = D_MODEL // 2
EPS = 1e-6

LANES = 128
SUBLANES = 8
VMEM_LIMIT = 56 * 1024 * 1024

IN_TM = 2048
IN_TN = 1664
MIX_TS = 512
XA_TS = 512
ROUTE_ROWS = 40
ROUTE_E0 = 8
MOE_M = 256
DISP_T = 256
COMB_T = 256


def _rms(x, g):
    ms = jnp.mean(x * x, axis=-1, keepdims=True)
    return x * lax.rsqrt(ms + EPS) * g


def _sigmoid(x):
    return 1.0 / (1.0 + jnp.exp(-x))


def _dot(a, b):
    return jnp.dot(a, b, preferred_element_type=F32)


def _dot_nt(a, b):
    return lax.dot_general(a, b, (((1,), (1,)), ((), ())), preferred_element_type=F32)


def _dot_tn(a, b):
    return lax.dot_general(a, b, (((0,), (0,)), ((), ())), preferred_element_type=F32)


ROW_TILE = D_MODEL // LANES


def _rows_to_tiles(ref, x):
    n = x.shape[0]
    for c in range(ROW_TILE):
        ref[pl.ds(c, n, stride=ROW_TILE), :] = x[:, c * LANES:(c + 1) * LANES]


def _tiles_to_rows(ref, n):
    return jnp.concatenate([ref[pl.ds(c, n, stride=ROW_TILE), :] for c in range(ROW_TILE)], axis=1)


def _inproj_kernel(x_ref, g_ref, w_ref, o_ref, xn_ref):
    @pl.when(pl.program_id(1) == 0)
    def _():
        xn_ref[...] = _rms(x_ref[...], g_ref[...]).astype(BF16)

    o_ref[...] = _dot(xn_ref[...], w_ref[...]).astype(BF16)


def _in_proj(x2, g, w_bf):
    t = x2.shape[0]
    return pl.pallas_call(
        _inproj_kernel,
        out_shape=jax.ShapeDtypeStruct((t, IN_WIDTH), BF16),
        grid=(t // IN_TM, IN_WIDTH // IN_TN),
        in_specs=[
            pl.BlockSpec((IN_TM, D_MODEL), lambda i, j: (i, 0)),
            pl.BlockSpec((1, D_MODEL), lambda i, j: (0, 0)),
            pl.BlockSpec((D_MODEL, IN_TN), lambda i, j: (0, j)),
        ],
        out_specs=pl.BlockSpec((IN_TM, IN_TN), lambda i, j: (i, j)),
        scratch_shapes=[pltpu.VMEM((IN_TM, D_MODEL), BF16)],
        compiler_params=pltpu.CompilerParams(
            dimension_semantics=("arbitrary", "arbitrary"), vmem_limit_bytes=VMEM_LIMIT),
        name="in_proj",
    )(x2, g, w_bf)


def _retention_constants(seq):
    pos = np.arange(seq, dtype=np.float64)
    inv_freq = ROPE_BASE ** (-np.arange(0, RET_DK, 2, dtype=np.float64) / RET_DK)
    ang = pos[:, None] * inv_freq[None, :]
    cos, sin = np.cos(ang), np.sin(ang)
    cos_t = np.concatenate([cos, cos, cos, cos], axis=1)
    sin_t = np.concatenate([-sin, sin, -sin, sin], axis=1)
    log_g = np.log(1.0 - 2.0 ** (-5.0 - np.arange(RET_HEADS, dtype=np.float64)))
    idx = np.arange(RET_CHUNK, dtype=np.float64)
    diff = idx[:, None] - idx[None, :]
    decay = np.where(diff >= 0, np.exp(np.maximum(diff, 0.0)[None] * log_g[:, None, None]), 0.0)
    zeta = np.exp((RET_CHUNK - 1 - idx)[None, :] * log_g[:, None])
    xi = np.exp((idx + 1)[None, :] * log_g[:, None])
    zeta_t = np.repeat(zeta.T, RET_DK, axis=1)
    xi_t = np.repeat(xi.T, RET_DK, axis=1)
    chunk_decay = np.exp(RET_CHUNK * log_g)
    f = lambda a: jnp.asarray(a, dtype=F32)
    return f(cos_t), f(sin_t), f(xi_t), f(zeta_t), f(decay), [float(c) for c in chunk_decay]


def _mixer_kernel(chunk_decay, proj_ref, x_ref, cos_ref, sin_ref, xi_ref, zeta_ref, decay_ref,
                  convw_ref, wc_ref, wr_ref, wm_ref, o_ref, state_ref, tail_ref, yin_ref):
    ts = x_ref.shape[0]

    @pl.when(pl.program_id(1) == 0)
    def _():
        state_ref[...] = jnp.zeros_like(state_ref)
        tail_ref[...] = jnp.zeros_like(tail_ref)

    xin = proj_ref[:, OFF_XIN:OFF_XIN + CONV_WIDTH].astype(F32)
    bg = proj_ref[:, OFF_BG:OFF_BG + CONV_WIDTH].astype(F32)
    cg = proj_ref[:, OFF_CG:OFF_CG + CONV_WIDTH].astype(F32)
    u = cg * xin
    ue = jnp.concatenate([tail_ref[...], u], axis=0)
    u1 = pltpu.roll(ue, 1, 0)[SUBLANES:]
    u2 = pltpu.roll(ue, 2, 0)[SUBLANES:]
    tail_ref[...] = u[ts - SUBLANES:]
    c = convw_ref[2:3, :] * u + convw_ref[1:2, :] * u1 + convw_ref[0:1, :] * u2
    y_conv = _dot((bg * c).astype(BF16), wc_ref[...])

    lane = lax.broadcasted_iota(jnp.int32, (1, LANES), 1)
    low_half = (lane % RET_DK) < (RET_DK // 2)
    head_masks = [(lane // RET_DK) == j for j in range(LANES // RET_DK)]

    def rotary(t, cosv, sinv):
        outs = []
        for p in range(QK_WIDTH // LANES):
            tp = t[:, p * LANES:(p + 1) * LANES]
            fwd = pltpu.roll(tp, LANES - RET_DK // 2, 1)
            bwd = pltpu.roll(tp, RET_DK // 2, 1)
            outs.append(tp * cosv + jnp.where(low_half, fwd, bwd) * sinv)
        return jnp.concatenate(outs, axis=1)

    n_chunks = ts // RET_CHUNK
    chunk_rows = [slice(ci * RET_CHUNK, (ci + 1) * RET_CHUNK) for ci in range(n_chunks)]
    pairs = [(ci, h) for ci in range(n_chunks) for h in range(RET_HEADS)]
    zero = jnp.zeros((), BF16)
    qm, qxm, kb, kz = {}, {}, {}, {}
    for ci, rows in enumerate(chunk_rows):
        cosv = cos_ref[rows, :]
        sinv = sin_ref[rows, :]
        qr = rotary(proj_ref[rows, OFF_Q:OFF_Q + QK_WIDTH].astype(F32), cosv, sinv)
        kr = rotary(proj_ref[rows, OFF_K:OFF_K + QK_WIDTH].astype(F32), cosv, sinv) * (RET_DK ** -0.5)
        qb = qr.astype(BF16)
        qx = (qr * xi_ref[...]).astype(BF16)
        kb[ci] = kr.astype(BF16)
        kz[ci] = (kr * zeta_ref[...]).astype(BF16)
        for h in range(RET_HEADS):
            lanes = slice((h // 2) * LANES, (h // 2 + 1) * LANES)
            qm[ci, h] = jnp.where(head_masks[h % 2], qb[:, lanes], zero)
            qxm[ci, h] = jnp.where(head_masks[h % 2], qx[:, lanes], zero)

    def pair_lanes(h):
        return slice((h // 2) * LANES, (h // 2 + 1) * LANES)

    def v_of(ci, h):
        return proj_ref[chunk_rows[ci], OFF_V + h * RET_DV:OFF_V + (h + 1) * RET_DV]

    scores = {(ci, h): _dot_nt(qm[ci, h], kb[ci][:, pair_lanes(h)]) for ci, h in pairs}
    kv = {(ci, h): _dot_tn(kz[ci][:, pair_lanes(h)], v_of(ci, h)) for ci, h in pairs}
    probs = {(ci, h): (scores[ci, h] * decay_ref[h]).astype(BF16) for ci, h in pairs}
    inner = {(ci, h): _dot(probs[ci, h], v_of(ci, h)) for ci, h in pairs}
    for h in range(RET_HEADS):
        st = state_ref[h]
        for ci in range(n_chunks):
            o = inner[ci, h] + _dot(qxm[ci, h], st.astype(BF16))
            st = chunk_decay[h] * st + kv[ci, h]
            mu = jnp.mean(o, axis=-1, keepdims=True)
            dlt = o - mu
            var = jnp.mean(dlt * dlt, axis=-1, keepdims=True)
            g = proj_ref[chunk_rows[ci], OFF_G + h * RET_DV:OFF_G + (h + 1) * RET_DV].astype(F32)
            yin_ref[chunk_rows[ci], h * RET_DV:(h + 1) * RET_DV] = (
                g * _sigmoid(g) * (dlt * lax.rsqrt(var + EPS))).astype(BF16)
        state_ref[h] = st

    y_ret = _dot(yin_ref[...], wr_ref[...])
    gate_c = proj_ref[:, OFF_GATE_C:OFF_GATE_C + D_MODEL].astype(F32)
    gate_r = proj_ref[:, OFF_GATE_R:OFF_GATE_R + D_MODEL].astype(F32)
    merged = _sigmoid(gate_c) * y_conv + _sigmoid(gate_r) * y_ret
    o_ref[...] = x_ref[...] + _dot(merged.astype(BF16), wm_ref[...])


def _mixer(proj, x2, conv_w, wc_bf, wr_bf, wm_bf, batch, seq):
    ts = MIX_TS
    ns = seq // ts
    cos_t, sin_t, xi_t, zeta_t, decay, chunk_decay = _retention_constants(seq)
    full = lambda shape: pl.BlockSpec(shape, lambda b, s: (0,) * len(shape))
    return pl.pallas_call(
        functools.partial(_mixer_kernel, chunk_decay),
        out_shape=jax.ShapeDtypeStruct((batch * seq, D_MODEL), F32),
        grid=(batch, ns),
        in_specs=[
            pl.BlockSpec((ts, IN_WIDTH), lambda b, s: (b * ns + s, 0)),
            pl.BlockSpec((ts, D_MODEL), lambda b, s: (b * ns + s, 0)),
            pl.BlockSpec((ts, LANES), lambda b, s: (s, 0)),
            pl.BlockSpec((ts, LANES), lambda b, s: (s, 0)),
            full((RET_CHUNK, QK_WIDTH)),
            full((RET_CHUNK, QK_WIDTH)),
            full((RET_HEADS, RET_CHUNK, RET_CHUNK)),
            full((CONV_K, CONV_WIDTH)),
            full((CONV_WIDTH, D_MODEL)),
            full((V_WIDTH, D_MODEL)),
            full((D_MODEL, D_MODEL)),
        ],
        out_specs=pl.BlockSpec((ts, D_MODEL), lambda b, s: (b * ns + s, 0)),
        scratch_shapes=[
            pltpu.VMEM((RET_HEADS, LANES, RET_DV), F32),
            pltpu.VMEM((SUBLANES, CONV_WIDTH), F32),
            pltpu.VMEM((ts, V_WIDTH), BF16),
        ],
        compiler_params=pltpu.CompilerParams(
            dimension_semantics=("arbitrary", "arbitrary"), vmem_limit_bytes=VMEM_LIMIT),
        name="mixer",
    )(proj, x2, cos_t, sin_t, xi_t, zeta_t, decay, conv_w, wc_bf, wr_bf, wm_bf)


def _memkv_kernel(m_ref, g_ref, w_ref, o_ref):
    o_ref[0] = _dot(_rms(m_ref[0], g_ref[...]).astype(BF16), w_ref[...]).astype(BF16)


def _mem_kv(mem, g, wkv_bf):
    b = mem.shape[0]
    return pl.pallas_call(
        _memkv_kernel,
        out_shape=jax.ShapeDtypeStruct((b, MEM_LEN, 2 * D_MODEL), BF16),
        grid=(b,),
        in_specs=[
            pl.BlockSpec((1, MEM_LEN, D_MODEL), lambda i: (i, 0, 0)),
            pl.BlockSpec((1, D_MODEL), lambda i: (0, 0)),
            pl.BlockSpec((D_MODEL, 2 * D_MODEL), lambda i: (0, 0)),
        ],
        out_specs=pl.BlockSpec((1, MEM_LEN, 2 * D_MODEL), lambda i: (i, 0, 0)),
        compiler_params=pltpu.CompilerParams(
            dimension_semantics=("arbitrary",), vmem_limit_bytes=VMEM_LIMIT),
        name="mem_kv",
    )(mem, g, wkv_bf)


def _xattn_kernel(h_ref, kv_ref, gx_ref, wq_ref, wo_ref, gm_ref, rwh_ref, rwl_ref, rb_ref,
                  h2_ref, xn2_ref, rid_ref, rwt_ref):
    ts = h_ref.shape[0]
    h = h_ref[...]
    q = _dot(_rms(h, gx_ref[...]).astype(BF16), wq_ref[...])
    outs = []
    for hd in range(XA_HEADS):
        cols = slice(hd * XA_HEAD_DIM, (hd + 1) * XA_HEAD_DIM)
        kh = kv_ref[0, :, cols]
        vh = kv_ref[0, :, D_MODEL + hd * XA_HEAD_DIM:D_MODEL + (hd + 1) * XA_HEAD_DIM]
        s = _dot_nt(q[:, cols].astype(BF16), kh) * (XA_HEAD_DIM ** -0.5)
        e = jnp.exp(s - jnp.max(s, axis=-1, keepdims=True))
        p = e / jnp.sum(e, axis=-1, keepdims=True)
        outs.append(_dot(p.astype(BF16), vh))
    h2 = h + _dot(jnp.concatenate(outs, axis=1).astype(BF16), wo_ref[...])
    h2_ref[...] = h2
    xn = _rms(h2, gm_ref[...])
    _rows_to_tiles(xn2_ref, xn)

    hi = xn.astype(BF16)
    lo = (xn - hi.astype(F32)).astype(BF16)
    lg = (_dot_nt(rwh_ref[...], hi) + _dot_nt(rwh_ref[...], lo) + _dot_nt(rwl_ref[...], hi)
          + rb_ref[:, 0:1])

    gl = lg[0:N_GROUPS]
    ge = jnp.exp(gl - jnp.max(gl, axis=0, keepdims=True))
    gp = ge / jnp.sum(ge, axis=0, keepdims=True)
    p_g = gp[0:1]
    g_idx = jnp.zeros((1, ts), jnp.int32)
    for i in range(1, N_GROUPS):
        better = gp[i:i + 1] > p_g
        g_idx = jnp.where(better, i, g_idx)
        p_g = jnp.where(better, gp[i:i + 1], p_g)
    sel = jnp.zeros((EXPERTS_PER_GROUP, ts), F32)
    for i in range(N_GROUPS):
        r0 = ROUTE_E0 + i * EXPERTS_PER_GROUP
        sel = jnp.where(g_idx == i, lg[r0:r0 + EXPERTS_PER_GROUP], sel)
    se = jnp.exp(sel - jnp.max(sel, axis=0, keepdims=True))
    sp = se / jnp.sum(se, axis=0, keepdims=True)
    ridx = lax.broadcasted_iota(jnp.int32, (EXPERTS_PER_GROUP, ts), 0)
    m1 = jnp.max(sp, axis=0, keepdims=True)
    i1 = jnp.min(jnp.where(sp == m1, ridx, EXPERTS_PER_GROUP), axis=0, keepdims=True)
    sp2 = jnp.where(ridx == i1, -1.0, sp)
    m2 = jnp.max(sp2, axis=0, keepdims=True)
    i2 = jnp.min(jnp.where(sp2 == m2, ridx, EXPERTS_PER_GROUP), axis=0, keepdims=True)
    den = m1 + m2
    rid_ref[...] = jnp.zeros_like(rid_ref)
    rwt_ref[...] = jnp.zeros_like(rwt_ref)
    rid_ref[0:1, :] = g_idx * EXPERTS_PER_GROUP + i1
    rid_ref[1:2, :] = g_idx * EXPERTS_PER_GROUP + i2
    rwt_ref[0:1, :] = p_g * m1 / den
    rwt_ref[1:2, :] = p_g * m2 / den


def _xattn(h1, kv, gx, wq_bf, wo_bf, gm, rwh, rwl, rb, batch, seq):
    ts = XA_TS
    ns = seq // ts
    t = batch * seq
    full = lambda shape: pl.BlockSpec(shape, lambda b, s: (0,) * len(shape))
    return pl.pallas_call(
        _xattn_kernel,
        out_shape=(
            jax.ShapeDtypeStruct((t, D_MODEL), F32),
            jax.ShapeDtypeStruct((t * ROW_TILE, LANES), F32),
            jax.ShapeDtypeStruct((SUBLANES, t), jnp.int32),
            jax.ShapeDtypeStruct((SUBLANES, t), F32),
        ),
        grid=(batch, ns),
        in_specs=[
            pl.BlockSpec((ts, D_MODEL), lambda b, s: (b * ns + s, 0)),
            pl.BlockSpec((1, MEM_LEN, 2 * D_MODEL), lambda b, s: (b, 0, 0)),
            full((1, D_MODEL)),
            full((D_MODEL, D_MODEL)),
            full((D_MODEL, D_MODEL)),
            full((1, D_MODEL)),
            full((ROUTE_ROWS, D_MODEL)),
            full((ROUTE_ROWS, D_MODEL)),
            full((ROUTE_ROWS, LANES)),
        ],
        out_specs=(
            pl.BlockSpec((ts, D_MODEL), lambda b, s: (b * ns + s, 0)),
            pl.BlockSpec((ts * ROW_TILE, LANES), lambda b, s: (b * ns + s, 0)),
            pl.BlockSpec((SUBLANES, ts), lambda b, s: (0, b * ns + s)),
            pl.BlockSpec((SUBLANES, ts), lambda b, s: (0, b * ns + s)),
        ),
        compiler_params=pltpu.CompilerParams(
            dimension_semantics=("arbitrary", "arbitrary"), vmem_limit_bytes=VMEM_LIMIT),
        name="xattn",
    )(h1, kv, gx, wq_bf, wo_bf, gm, rwh, rwl, rb)


def _tile_of(ref, row):
    start = row * ROW_TILE
    if not isinstance(start, int):
        start = pl.multiple_of(start, ROW_TILE)
    return ref.at[pl.ds(start, ROW_TILE)]


def _row_copy(src_ref, src_row, dst_ref, dst_row, sem):
    return pltpu.make_async_copy(_tile_of(src_ref, src_row), _tile_of(dst_ref, dst_row), sem)


def _dispatch_kernel(last_ref, has_ref, nu_ref, pos_ref, xn_ref, xs_ref, zero_ref, sem, zsem):
    td = xn_ref.shape[0] // ROW_TILE
    blk = MOE_M * ROW_TILE
    n_blocks = xs_ref.shape[0] // blk

    @pl.when(pl.program_id(0) == 0)
    def _():
        zero_ref[...] = jnp.zeros_like(zero_ref)

        def fill(start):
            start = pl.multiple_of(start * ROW_TILE, blk)
            return pltpu.make_async_copy(zero_ref, xs_ref.at[pl.ds(start, blk)], zsem)

        for e in range(N_EXPERTS):
            @pl.when(has_ref[e] > 0)
            def _():
                fill(last_ref[e]).start()
        for e in range(N_EXPERTS):
            @pl.when(has_ref[e] > 0)
            def _():
                fill(last_ref[e]).wait()

        def fill_tail(b, carry):
            fill(b * MOE_M).start()
            fill(b * MOE_M).wait()
            return carry

        lax.fori_loop(nu_ref[0], n_blocks, fill_tail, 0)

    for r in range(td):
        for k in range(TOP_K):
            _row_copy(xn_ref, r, xs_ref, pos_ref[0, k, r], sem).start(priority=k)
    for k in range(TOP_K):
        pltpu.make_async_copy(xn_ref, xs_ref.at[pl.ds(0, td * ROW_TILE)], sem).wait()


def _dispatch(last_blk, has, n_used, pos3, xn2, n_rows):
    t = xn2.shape[0] // ROW_TILE
    td = DISP_T
    return pl.pallas_call(
        _dispatch_kernel,
        out_shape=jax.ShapeDtypeStruct((n_rows * ROW_TILE, LANES), F32),
        grid_spec=pltpu.PrefetchScalarGridSpec(
            num_scalar_prefetch=3,
            grid=(t // td,),
            in_specs=[
                pl.BlockSpec((1, TOP_K, td), lambda i, *_: (i, 0, 0), memory_space=pltpu.SMEM),
                pl.BlockSpec((td * ROW_TILE, LANES), lambda i, *_: (i, 0)),
            ],
            out_specs=pl.BlockSpec(memory_space=pl.ANY),
            scratch_shapes=[
                pltpu.VMEM((MOE_M * ROW_TILE, LANES), F32),
                pltpu.SemaphoreType.DMA,
                pltpu.SemaphoreType.DMA,
            ],
        ),
        compiler_params=pltpu.CompilerParams(
            dimension_semantics=("arbitrary",), vmem_limit_bytes=VMEM_LIMIT),
        name="dispatch",
    )(last_blk, has, n_used, pos3, xn2)


def _expert_kernel(be_ref, nu_ref, xs_ref, wg_ref, wu_ref, wd_ref, ys_ref, wgb, wub, wdb):
    i = pl.program_id(0)

    @pl.when((i == 0) | (be_ref[i] != be_ref[jnp.maximum(i - 1, 0)]))
    def _():
        wgb[...] = wg_ref[0].astype(BF16)
        wub[...] = wu_ref[0].astype(BF16)
        wdb[...] = wd_ref[0].astype(BF16)

    @pl.when(i < nu_ref[0])
    def _():
        x = _tiles_to_rows(xs_ref, MOE_M).astype(BF16)
        g = _dot(x, wgb[...])
        u = _dot(x, wub[...])
        _rows_to_tiles(ys_ref, _dot((g * _sigmoid(g) * u).astype(BF16), wdb[...]))

    @pl.when(i >= nu_ref[0])
    def _():
        ys_ref[...] = jnp.zeros_like(ys_ref)


def _experts(block_e, n_used, xs, w_gate, w_up, w_down):
    n_rows = xs.shape[0] // ROW_TILE
    row_map = lambda i, be, nu: (jnp.minimum(i, nu[0] - 1), 0)
    out_map = lambda i, be, nu: (i, 0)
    w_map = lambda i, be, nu: (be[i], 0, 0)
    return pl.pallas_call(
        _expert_kernel,
        out_shape=jax.ShapeDtypeStruct((n_rows * ROW_TILE, LANES), F32),
        grid_spec=pltpu.PrefetchScalarGridSpec(
            num_scalar_prefetch=2,
            grid=(n_rows // MOE_M,),
            in_specs=[
                pl.BlockSpec((MOE_M * ROW_TILE, LANES), row_map),
                pl.BlockSpec((1, D_MODEL, EXPERT_HIDDEN), w_map),
                pl.BlockSpec((1, D_MODEL, EXPERT_HIDDEN), w_map),
                pl.BlockSpec((1, EXPERT_HIDDEN, D_MODEL), w_map),
            ],
            out_specs=pl.BlockSpec((MOE_M * ROW_TILE, LANES), out_map),
            scratch_shapes=[
                pltpu.VMEM((D_MODEL, EXPERT_HIDDEN), BF16),
                pltpu.VMEM((D_MODEL, EXPERT_HIDDEN), BF16),
                pltpu.VMEM((EXPERT_HIDDEN, D_MODEL), BF16),
            ],
        ),
        compiler_params=pltpu.CompilerParams(
            dimension_semantics=("arbitrary",), vmem_limit_bytes=VMEM_LIMIT),
        name="experts",
    )(block_e, n_used, xs, w_gate, w_up, w_down)


def _combine_kernel(final_norm, pos_ref, h2_ref, wt_ref, g_ref, ys_ref, o_ref, buf0_ref, buf1_ref, sem):
    tc = h2_ref.shape[0]
    bufs = (buf0_ref, buf1_ref)
    for r in range(tc):
        for k in range(TOP_K):
            _row_copy(ys_ref, pos_ref[0, k, r], bufs[k], r, sem).start(priority=k)
    for k in range(TOP_K):
        pltpu.make_async_copy(ys_ref.at[pl.ds(0, tc * ROW_TILE)], bufs[k], sem).wait()
    y = (h2_ref[...] + wt_ref[:, 0:1] * _tiles_to_rows(buf0_ref, tc)
         + wt_ref[:, 1:2] * _tiles_to_rows(buf1_ref, tc))
    o_ref[...] = _rms(y, g_ref[...]) if final_norm else y


def _combine(pos3, h2, wt, g, ys, final_norm):
    t = h2.shape[0]
    tc = COMB_T
    return pl.pallas_call(
        functools.partial(_combine_kernel, final_norm),
        out_shape=jax.ShapeDtypeStruct((t, D_MODEL), F32),
        grid=(t // tc,),
        in_specs=[
            pl.BlockSpec((1, TOP_K, tc), lambda i: (i, 0, 0), memory_space=pltpu.SMEM),
            pl.BlockSpec((tc, D_MODEL), lambda i: (i, 0)),
            pl.BlockSpec((tc, TOP_K), lambda i: (i, 0)),
            pl.BlockSpec((1, D_MODEL), lambda i: (0, 0)),
            pl.BlockSpec(memory_space=pl.ANY),
        ],
        out_specs=pl.BlockSpec((tc, D_MODEL), lambda i: (i, 0)),
        scratch_shapes=[
            pltpu.VMEM((tc * ROW_TILE, LANES), F32),
            pltpu.VMEM((tc * ROW_TILE, LANES), F32),
            pltpu.SemaphoreType.DMA,
        ],
        compiler_params=pltpu.CompilerParams(
            dimension_semantics=("arbitrary",), vmem_limit_bytes=VMEM_LIMIT),
        name="combine",
    )(pos3, h2, wt, g, ys)


def _routing_tables(rid, n_tok):
    e_flat = rid[:TOP_K].reshape(-1)
    onehot = (e_flat[None, :] == jnp.arange(N_EXPERTS, dtype=jnp.int32)[:, None]).astype(jnp.int32)
    csum = jnp.cumsum(onehot, axis=1)
    counts = csum[:, -1]
    rank = jnp.sum(onehot * csum, axis=0) - 1
    padded = (counts + MOE_M - 1) // MOE_M * MOE_M
    pend = jnp.cumsum(padded)
    pstart = pend - padded
    pos = jnp.sum(onehot * pstart[:, None], axis=0) + rank
    n_blocks = (TOP_K * n_tok + N_EXPERTS * (MOE_M - 1) + MOE_M - 1) // MOE_M
    n_used = pend[-1] // MOE_M
    blk = jnp.minimum(jnp.arange(n_blocks, dtype=jnp.int32), n_used - 1)
    block_e = jnp.sum((pend[None, :] <= (blk * MOE_M)[:, None]).astype(jnp.int32), axis=1)
    block_e = jnp.minimum(block_e, N_EXPERTS - 1)
    last_blk = jnp.maximum(pend - MOE_M, 0).astype(jnp.int32)
    has = (counts > 0).astype(jnp.int32)
    return (pos.reshape(TOP_K, n_tok).astype(jnp.int32), block_e.astype(jnp.int32),
            n_used.reshape(1).astype(jnp.int32), last_blk, has, n_blocks * MOE_M)


def _tile_pos(pos, tile):
    k, t = pos.shape
    return pos.reshape(k, t // tile, tile).transpose(1, 0, 2)


def kernel(x, mem, mix_norm_g, w_in, conv_w, w_conv_out, w_ret_out, w_mix_out, xa_norm_g, mem_norm_g,
           w_xa_q, w_xa_kv, w_xa_o, moe_norm_g, w_group, b_group, w_router, b_router, w_gate, w_up,
           w_down, final_norm_g):
    batch, seq, d = x.shape
    depth = w_in.shape[0]
    t = batch * seq
    h = x.reshape(t, d)
    for l in range(depth):
        proj = _in_proj(h, mix_norm_g[l][None], w_in[l].astype(BF16))
        h1 = _mixer(proj, h, conv_w[l], w_conv_out[l].astype(BF16), w_ret_out[l].astype(BF16),
                    w_mix_out[l].astype(BF16), batch, seq)
        kv = _mem_kv(mem, mem_norm_g[l][None], w_xa_kv[l].astype(BF16))

        rw = jnp.zeros((ROUTE_ROWS, d), F32)
        rw = rw.at[0:N_GROUPS].set(w_group[l].T).at[ROUTE_E0:ROUTE_E0 + N_EXPERTS].set(w_router[l].T)
        rwh = rw.astype(BF16)
        rwl = (rw - rwh.astype(F32)).astype(BF16)
        rb = jnp.zeros((ROUTE_ROWS,), F32)
        rb = rb.at[0:N_GROUPS].set(b_group[l]).at[ROUTE_E0:ROUTE_E0 + N_EXPERTS].set(b_router[l])
        rb = jnp.broadcast_to(rb[:, None], (ROUTE_ROWS, LANES))

        h2, xn2, rid, rwt = _xattn(h1, kv, xa_norm_g[l][None], w_xa_q[l].astype(BF16),
                                   w_xa_o[l].astype(BF16), moe_norm_g[l][None], rwh, rwl, rb, batch, seq)

        pos, block_e, n_used, last_blk, has, n_rows = _routing_tables(rid, t)
        xs = _dispatch(last_blk, has, n_used, _tile_pos(pos, DISP_T), xn2, n_rows)
        ys = _experts(block_e, n_used, xs, w_gate[l], w_up[l], w_down[l])
        h = _combine(_tile_pos(pos, COMB_T), h2, rwt[:TOP_K].T, final_norm_g[None], ys,
                     final_norm=(l == depth - 1))
    return h.reshape(batch, seq, d)
```

```python
import functools

import numpy as np
import jax
import jax.numpy as jnp
from jax import lax
from jax.experimental import pallas as pl
from jax.experimental.pallas import tpu as pltpu

F32 = jnp.float32
BF16 = jnp.bfloat16

D_MODEL = 1024
CONV_WIDTH = 512
CONV_K = 3
RET_HEADS = 8
RET_DK = 64
RET_DV = 128
RET_CHUNK = 128
QK_WIDTH = RET_HEADS * RET_DK
V_WIDTH = RET_HEADS * RET_DV
ROPE_BASE = 10000.0
IN_WIDTH = 3 * CONV_WIDTH + 2 * QK_WIDTH + 2 * V_WIDTH + 2 * D_MODEL
OFF_XIN = 0
OFF_BG = OFF_XIN + CONV_WIDTH
OFF_CG = OFF_BG + CONV_WIDTH
OFF_Q = OFF_CG + CONV_WIDTH
OFF_K = OFF_Q + QK_WIDTH
OFF_V = OFF_K + QK_WIDTH
OFF_G = OFF_V + V_WIDTH
OFF_GATE_C = OFF_G + V_WIDTH
OFF_GATE_R = OFF_GATE_C + D_MODEL
MEM_LEN = 256
XA_HEADS = 4
XA_HEAD_DIM = D_MODEL // XA_HEADS
N_GROUPS = 4
EXPERTS_PER_GROUP = 8
N_EXPERTS = N_GROUPS * EXPERTS_PER_GROUP
TOP_K = 2
EXPERT_HIDDEN = D_MODEL // 2
EPS = 1e-6

LANES = 128
SUBLANES = 8
VMEM_LIMIT = 56 * 1024 * 1024

IN_TM = 2048
IN_TN = 1664
MIX_TS = 512
MIX_OUT_ROWS = 256
XA_TS = 512
ROUTE_ROWS = 40
ROUTE_E0 = 8
MOE_M = 512
DISP_T = 256
COMB_T = 256


def _rms(x, g):
    ms = jnp.mean(x * x, axis=-1, keepdims=True)
    return x * lax.rsqrt(ms + EPS) * g


def _sigmoid(x):
    return 1.0 / (1.0 + jnp.exp(-x))


def _dot(a, b):
    return jnp.dot(a, b, preferred_element_type=F32)


def _dot_nt(a, b):
    return lax.dot_general(a, b, (((1,), (1,)), ((), ())), preferred_element_type=F32)


def _dot_tn(a, b):
    return lax.dot_general(a, b, (((0,), (0,)), ((), ())), preferred_element_type=F32)


ROW_TILE = D_MODEL // LANES


def _rows_to_tiles(ref, x):
    n = x.shape[0]
    for c in range(ROW_TILE):
        ref[pl.ds(c, n, stride=ROW_TILE), :] = x[:, c * LANES:(c + 1) * LANES]


def _tiles_to_rows(ref, n):
    return jnp.concatenate([ref[pl.ds(c, n, stride=ROW_TILE), :] for c in range(ROW_TILE)], axis=1)


def _inproj_kernel(x_ref, g_ref, w_ref, o_ref, xn_ref):
    @pl.when(pl.program_id(1) == 0)
    def _():
        xn_ref[...] = _rms(x_ref[...], g_ref[...]).astype(BF16)

    o_ref[...] = _dot(xn_ref[...], w_ref[...].astype(BF16)).astype(BF16)


def _in_proj(x2, g, w_in):
    t = x2.shape[0]
    return pl.pallas_call(
        _inproj_kernel,
        out_shape=jax.ShapeDtypeStruct((t, IN_WIDTH), BF16),
        grid=(t // IN_TM, IN_WIDTH // IN_TN),
        in_specs=[
            pl.BlockSpec((IN_TM, D_MODEL), lambda i, j: (i, 0)),
            pl.BlockSpec((1, D_MODEL), lambda i, j: (0, 0)),
            pl.BlockSpec((D_MODEL, IN_TN), lambda i, j: (0, j)),
        ],
        out_specs=pl.BlockSpec((IN_TM, IN_TN), lambda i, j: (i, j)),
        scratch_shapes=[pltpu.VMEM((IN_TM, D_MODEL), BF16)],
        compiler_params=pltpu.CompilerParams(
            dimension_semantics=("arbitrary", "arbitrary"), vmem_limit_bytes=VMEM_LIMIT),
        name="in_proj",
    )(x2, g, w_in)


def _retention_constants(seq):
    pos = np.arange(seq, dtype=np.float64)
    inv_freq = ROPE_BASE ** (-np.arange(0, RET_DK, 2, dtype=np.float64) / RET_DK)
    ang = pos[:, None] * inv_freq[None, :]
    cos, sin = np.cos(ang), np.sin(ang)
    cos_t = np.concatenate([cos, cos, cos, cos], axis=1)
    sin_t = np.concatenate([-sin, sin, -sin, sin], axis=1)
    log_g = np.log(1.0 - 2.0 ** (-5.0 - np.arange(RET_HEADS, dtype=np.float64)))
    idx = np.arange(RET_CHUNK, dtype=np.float64)
    diff = idx[:, None] - idx[None, :]
    decay = np.where(diff >= 0, np.exp(np.maximum(diff, 0.0)[None] * log_g[:, None, None]), 0.0)
    zeta = np.exp((RET_CHUNK - 1 - idx)[None, :] * log_g[:, None])
    xi = np.exp((idx + 1)[None, :] * log_g[:, None])
    zeta_t = np.repeat(zeta.T, RET_DK, axis=1)
    xi_t = np.repeat(xi.T, RET_DK, axis=1)
    chunk_decay = np.exp(RET_CHUNK * log_g)
    f = lambda a: jnp.asarray(a, dtype=F32)
    return f(cos_t), f(sin_t), f(xi_t), f(zeta_t), f(decay), [float(c) for c in chunk_decay]


def _mixer_kernel(chunk_decay, proj_ref, x_ref, cos_ref, sin_ref, xi_ref, zeta_ref, decay_ref,
                  convw_ref, wc_ref, wr_ref, wm_ref, o_ref, state_ref, tail_ref, yin_ref):
    ts = x_ref.shape[0]

    @pl.when(pl.program_id(1) == 0)
    def _():
        state_ref[...] = jnp.zeros_like(state_ref)
        tail_ref[...] = jnp.zeros_like(tail_ref)

    xin = proj_ref[:, OFF_XIN:OFF_XIN + CONV_WIDTH].astype(F32)
    bg = proj_ref[:, OFF_BG:OFF_BG + CONV_WIDTH].astype(F32)
    cg = proj_ref[:, OFF_CG:OFF_CG + CONV_WIDTH].astype(F32)
    u = cg * xin
    ue = jnp.concatenate([tail_ref[...], u], axis=0)
    u1 = pltpu.roll(ue, 1, 0)[SUBLANES:]
    u2 = pltpu.roll(ue, 2, 0)[SUBLANES:]
    tail_ref[...] = u[ts - SUBLANES:]
    c = convw_ref[2:3, :] * u + convw_ref[1:2, :] * u1 + convw_ref[0:1, :] * u2
    y_conv = _dot((bg * c).astype(BF16), wc_ref[...])

    lane = lax.broadcasted_iota(jnp.int32, (1, LANES), 1)
    low_half = (lane % RET_DK) < (RET_DK // 2)
    head_masks = [(lane // RET_DK) == j for j in range(LANES // RET_DK)]

    def rotary(t, cosv, sinv):
        outs = []
        for p in range(QK_WIDTH // LANES):
            tp = t[:, p * LANES:(p + 1) * LANES]
            fwd = pltpu.roll(tp, LANES - RET_DK // 2, 1)
            bwd = pltpu.roll(tp, RET_DK // 2, 1)
            outs.append(tp * cosv + jnp.where(low_half, fwd, bwd) * sinv)
        return jnp.concatenate(outs, axis=1)

    n_chunks = ts // RET_CHUNK
    chunk_rows = [slice(ci * RET_CHUNK, (ci + 1) * RET_CHUNK) for ci in range(n_chunks)]
    pairs = [(ci, h) for ci in range(n_chunks) for h in range(RET_HEADS)]
    zero = jnp.zeros((), BF16)
    qm, qxm, kb, kz = {}, {}, {}, {}
    for ci, rows in enumerate(chunk_rows):
        cosv = cos_ref[rows, :]
        sinv = sin_ref[rows, :]
        qr = rotary(proj_ref[rows, OFF_Q:OFF_Q + QK_WIDTH].astype(F32), cosv, sinv)
        kr = rotary(proj_ref[rows, OFF_K:OFF_K + QK_WIDTH].astype(F32), cosv, sinv) * (RET_DK ** -0.5)
        qb = qr.astype(BF16)
        qx = (qr * xi_ref[...]).astype(BF16)
        kb[ci] = kr.astype(BF16)
        kz[ci] = (kr * zeta_ref[...]).astype(BF16)
        for h in range(RET_HEADS):
            lanes = slice((h // 2) * LANES, (h // 2 + 1) * LANES)
            qm[ci, h] = jnp.where(head_masks[h % 2], qb[:, lanes], zero)
            qxm[ci, h] = jnp.where(head_masks[h % 2], qx[:, lanes], zero)

    def pair_lanes(h):
        return slice((h // 2) * LANES, (h // 2 + 1) * LANES)

    def v_of(ci, h):
        return proj_ref[chunk_rows[ci], OFF_V + h * RET_DV:OFF_V + (h + 1) * RET_DV]

    scores = {(ci, h): _dot_nt(qm[ci, h], kb[ci][:, pair_lanes(h)]) for ci, h in pairs}
    kv = {(ci, h): _dot_tn(kz[ci][:, pair_lanes(h)], v_of(ci, h)) for ci, h in pairs}
    probs = {(ci, h): (scores[ci, h] * decay_ref[h]).astype(BF16) for ci, h in pairs}
    inner = {(ci, h): _dot(probs[ci, h], v_of(ci, h)) for ci, h in pairs}
    st_before = {}
    for h in range(RET_HEADS):
        st = state_ref[h]
        for ci in range(n_chunks):
            st_before[ci, h] = st.astype(BF16)
            st = chunk_decay[h] * st + kv[ci, h]
        state_ref[h] = st
    o = {(ci, h): inner[ci, h] + _dot(qxm[ci, h], st_before[ci, h]) for ci, h in pairs}
    mu = {p: jnp.mean(o[p], axis=-1, keepdims=True) for p in pairs}
    dlt = {p: o[p] - mu[p] for p in pairs}
    var = {p: jnp.mean(dlt[p] * dlt[p], axis=-1, keepdims=True) for p in pairs}
    for ci, h in pairs:
        g = proj_ref[chunk_rows[ci], OFF_G + h * RET_DV:OFF_G + (h + 1) * RET_DV].astype(F32)
        yin_ref[chunk_rows[ci], h * RET_DV:(h + 1) * RET_DV] = (
            g * _sigmoid(g) * (dlt[ci, h] * lax.rsqrt(var[ci, h] + EPS))).astype(BF16)

    for r0 in range(0, ts, MIX_OUT_ROWS):
        rows = slice(r0, r0 + MIX_OUT_ROWS)
        y_ret = _dot(yin_ref[rows, :], wr_ref[...])
        gate_c = proj_ref[rows, OFF_GATE_C:OFF_GATE_C + D_MODEL].astype(F32)
        gate_r = proj_ref[rows, OFF_GATE_R:OFF_GATE_R + D_MODEL].astype(F32)
        merged = _sigmoid(gate_c) * y_conv[rows] + _sigmoid(gate_r) * y_ret
        o_ref[rows, :] = x_ref[rows, :] + _dot(merged.astype(BF16), wm_ref[...])


def _mixer(proj, x2, conv_w, wc_bf, wr_bf, wm_bf, batch, seq):
    ts = MIX_TS
    ns = seq // ts
    cos_t, sin_t, xi_t, zeta_t, decay, chunk_decay = _retention_constants(seq)
    full = lambda shape: pl.BlockSpec(shape, lambda b, s: (0,) * len(shape))
    return pl.pallas_call(
        functools.partial(_mixer_kernel, chunk_decay),
        out_shape=jax.ShapeDtypeStruct((batch * seq, D_MODEL), F32),
        grid=(batch, ns),
        in_specs=[
            pl.BlockSpec((ts, IN_WIDTH), lambda b, s: (b * ns + s, 0)),
            pl.BlockSpec((ts, D_MODEL), lambda b, s: (b * ns + s, 0)),
            pl.BlockSpec((ts, LANES), lambda b, s: (s, 0)),
            pl.BlockSpec((ts, LANES), lambda b, s: (s, 0)),
            full((RET_CHUNK, QK_WIDTH)),
            full((RET_CHUNK, QK_WIDTH)),
            full((RET_HEADS, RET_CHUNK, RET_CHUNK)),
            full((CONV_K, CONV_WIDTH)),
            full((CONV_WIDTH, D_MODEL)),
            full((V_WIDTH, D_MODEL)),
            full((D_MODEL, D_MODEL)),
        ],
        out_specs=pl.BlockSpec((ts, D_MODEL), lambda b, s: (b * ns + s, 0)),
        scratch_shapes=[
            pltpu.VMEM((RET_HEADS, LANES, RET_DV), F32),
            pltpu.VMEM((SUBLANES, CONV_WIDTH), F32),
            pltpu.VMEM((ts, V_WIDTH), BF16),
        ],
        compiler_params=pltpu.CompilerParams(
            dimension_semantics=("arbitrary", "arbitrary"), vmem_limit_bytes=VMEM_LIMIT),
        name="mixer",
    )(proj, x2, cos_t, sin_t, xi_t, zeta_t, decay, conv_w, wc_bf, wr_bf, wm_bf)


def _memkv_kernel(m_ref, g_ref, w_ref, o_ref):
    o_ref[0] = _dot(_rms(m_ref[0], g_ref[...]).astype(BF16), w_ref[...]).astype(BF16)


def _mem_kv(mem, g, wkv_bf):
    b = mem.shape[0]
    return pl.pallas_call(
        _memkv_kernel,
        out_shape=jax.ShapeDtypeStruct((b, MEM_LEN, 2 * D_MODEL), BF16),
        grid=(b,),
        in_specs=[
            pl.BlockSpec((1, MEM_LEN, D_MODEL), lambda i: (i, 0, 0)),
            pl.BlockSpec((1, D_MODEL), lambda i: (0, 0)),
            pl.BlockSpec((D_MODEL, 2 * D_MODEL), lambda i: (0, 0)),
        ],
        out_specs=pl.BlockSpec((1, MEM_LEN, 2 * D_MODEL), lambda i: (i, 0, 0)),
        compiler_params=pltpu.CompilerParams(
            dimension_semantics=("arbitrary",), vmem_limit_bytes=VMEM_LIMIT),
        name="mem_kv",
    )(mem, g, wkv_bf)


def _xattn_kernel(h_ref, kv_ref, gx_ref, wq_ref, wo_ref, gm_ref, rwh_ref, rwl_ref, rb_ref,
                  h2_ref, xn2_ref, rid_ref, rwt_ref):
    ts = h_ref.shape[0]
    h = h_ref[...]
    q = _dot(_rms(h, gx_ref[...]).astype(BF16), wq_ref[...])
    qb = q.astype(BF16)
    head_cols = [slice(hd * XA_HEAD_DIM, (hd + 1) * XA_HEAD_DIM) for hd in range(XA_HEADS)]
    scores = [_dot_nt(qb[:, cols], kv_ref[0, :, cols]) * (XA_HEAD_DIM ** -0.5) for cols in head_cols]
    probs = []
    for s in scores:
        e = jnp.exp(s - jnp.max(s, axis=-1, keepdims=True))
        probs.append((e / jnp.sum(e, axis=-1, keepdims=True)).astype(BF16))
    outs = [_dot(p, kv_ref[0, :, D_MODEL + hd * XA_HEAD_DIM:D_MODEL + (hd + 1) * XA_HEAD_DIM])
            for hd, p in enumerate(probs)]
    h2 = h + _dot(jnp.concatenate(outs, axis=1).astype(BF16), wo_ref[...])
    h2_ref[...] = h2
    xn = _rms(h2, gm_ref[...])
    _rows_to_tiles(xn2_ref, xn)

    hi = xn.astype(BF16)
    lo = (xn - hi.astype(F32)).astype(BF16)
    lg = (_dot_nt(rwh_ref[...], hi) + _dot_nt(rwh_ref[...], lo) + _dot_nt(rwl_ref[...], hi)
          + rb_ref[:, 0:1])

    gl = lg[0:N_GROUPS]
    ge = jnp.exp(gl - jnp.max(gl, axis=0, keepdims=True))
    gp = ge / jnp.sum(ge, axis=0, keepdims=True)
    p_g = gp[0:1]
    g_idx = jnp.zeros((1, ts), jnp.int32)
    for i in range(1, N_GROUPS):
        better = gp[i:i + 1] > p_g
        g_idx = jnp.where(better, i, g_idx)
        p_g = jnp.where(better, gp[i:i + 1], p_g)
    sel = jnp.zeros((EXPERTS_PER_GROUP, ts), F32)
    for i in range(N_GROUPS):
        r0 = ROUTE_E0 + i * EXPERTS_PER_GROUP
        sel = jnp.where(g_idx == i, lg[r0:r0 + EXPERTS_PER_GROUP], sel)
    se = jnp.exp(sel - jnp.max(sel, axis=0, keepdims=True))
    sp = se / jnp.sum(se, axis=0, keepdims=True)
    ridx = lax.broadcasted_iota(jnp.int32, (EXPERTS_PER_GROUP, ts), 0)
    m1 = jnp.max(sp, axis=0, keepdims=True)
    i1 = jnp.min(jnp.where(sp == m1, ridx, EXPERTS_PER_GROUP), axis=0, keepdims=True)
    sp2 = jnp.where(ridx == i1, -1.0, sp)
    m2 = jnp.max(sp2, axis=0, keepdims=True)
    i2 = jnp.min(jnp.where(sp2 == m2, ridx, EXPERTS_PER_GROUP), axis=0, keepdims=True)
    den = m1 + m2
    rid_ref[...] = jnp.zeros_like(rid_ref)
    rwt_ref[...] = jnp.zeros_like(rwt_ref)
    rid_ref[0:1, :] = g_idx * EXPERTS_PER_GROUP + i1
    rid_ref[1:2, :] = g_idx * EXPERTS_PER_GROUP + i2
    rwt_ref[0:1, :] = p_g * m1 / den
    rwt_ref[1:2, :] = p_g * m2 / den


def _xattn(h1, kv, gx, wq_bf, wo_bf, gm, rwh, rwl, rb, batch, seq):
    ts = XA_TS
    ns = seq // ts
    t = batch * seq
    full = lambda shape: pl.BlockSpec(shape, lambda b, s: (0,) * len(shape))
    return pl.pallas_call(
        _xattn_kernel,
        out_shape=(
            jax.ShapeDtypeStruct((t, D_MODEL), F32),
            jax.ShapeDtypeStruct((t * ROW_TILE, LANES), F32),
            jax.ShapeDtypeStruct((SUBLANES, t), jnp.int32),
            jax.ShapeDtypeStruct((SUBLANES, t), F32),
        ),
        grid=(batch, ns),
        in_specs=[
            pl.BlockSpec((ts, D_MODEL), lambda b, s: (b * ns + s, 0)),
            pl.BlockSpec((1, MEM_LEN, 2 * D_MODEL), lambda b, s: (b, 0, 0)),
            full((1, D_MODEL)),
            full((D_MODEL, D_MODEL)),
            full((D_MODEL, D_MODEL)),
            full((1, D_MODEL)),
            full((ROUTE_ROWS, D_MODEL)),
            full((ROUTE_ROWS, D_MODEL)),
            full((ROUTE_ROWS, LANES)),
        ],
        out_specs=(
            pl.BlockSpec((ts, D_MODEL), lambda b, s: (b * ns + s, 0)),
            pl.BlockSpec((ts * ROW_TILE, LANES), lambda b, s: (b * ns + s, 0)),
            pl.BlockSpec((SUBLANES, ts), lambda b, s: (0, b * ns + s)),
            pl.BlockSpec((SUBLANES, ts), lambda b, s: (0, b * ns + s)),
        ),
        compiler_params=pltpu.CompilerParams(
            dimension_semantics=("arbitrary", "arbitrary"), vmem_limit_bytes=VMEM_LIMIT),
        name="xattn",
    )(h1, kv, gx, wq_bf, wo_bf, gm, rwh, rwl, rb)


def _tile_of(ref, row):
    start = row * ROW_TILE
    if not isinstance(start, int):
        start = pl.multiple_of(start, ROW_TILE)
    return ref.at[pl.ds(start, ROW_TILE)]


def _row_copy(src_ref, src_row, dst_ref, dst_row, sem):
    return pltpu.make_async_copy(_tile_of(src_ref, src_row), _tile_of(dst_ref, dst_row), sem)


def _dispatch_kernel(last_ref, has_ref, nu_ref, pos_ref, xn_ref, xs_ref, zero_ref, sem, zsem):
    td = xn_ref.shape[0] // ROW_TILE
    blk = MOE_M * ROW_TILE
    n_blocks = xs_ref.shape[0] // blk

    @pl.when(pl.program_id(0) == 0)
    def _():
        zero_ref[...] = jnp.zeros_like(zero_ref)

        def fill(start):
            start = pl.multiple_of(start * ROW_TILE, blk)
            return pltpu.make_async_copy(zero_ref, xs_ref.at[pl.ds(start, blk)], zsem)

        for e in range(N_EXPERTS):
            @pl.when(has_ref[e] > 0)
            def _():
                fill(last_ref[e]).start()
        for e in range(N_EXPERTS):
            @pl.when(has_ref[e] > 0)
            def _():
                fill(last_ref[e]).wait()

        def fill_tail(b, carry):
            fill(b * MOE_M).start()
            fill(b * MOE_M).wait()
            return carry

        lax.fori_loop(nu_ref[0], n_blocks, fill_tail, 0)

    for r in range(td):
        for k in range(TOP_K):
            _row_copy(xn_ref, r, xs_ref, pos_ref[0, k, r], sem).start(priority=k)
    for k in range(TOP_K):
        pltpu.make_async_copy(xn_ref, xs_ref.at[pl.ds(0, td * ROW_TILE)], sem).wait()


def _dispatch(last_blk, has, n_used, pos3, xn2, n_rows):
    t = xn2.shape[0] // ROW_TILE
    td = DISP_T
    return pl.pallas_call(
        _dispatch_kernel,
        out_shape=jax.ShapeDtypeStruct((n_rows * ROW_TILE, LANES), F32),
        grid_spec=pltpu.PrefetchScalarGridSpec(
            num_scalar_prefetch=3,
            grid=(t // td,),
            in_specs=[
                pl.BlockSpec((1, TOP_K, td), lambda i, *_: (i, 0, 0), memory_space=pltpu.SMEM),
                pl.BlockSpec((td * ROW_TILE, LANES), lambda i, *_: (i, 0)),
            ],
            out_specs=pl.BlockSpec(memory_space=pl.ANY),
            scratch_shapes=[
                pltpu.VMEM((MOE_M * ROW_TILE, LANES), F32),
                pltpu.SemaphoreType.DMA,
                pltpu.SemaphoreType.DMA,
            ],
        ),
        compiler_params=pltpu.CompilerParams(
            dimension_semantics=("arbitrary",), vmem_limit_bytes=VMEM_LIMIT),
        name="dispatch",
    )(last_blk, has, n_used, pos3, xn2)


def _expert_kernel(be_ref, nu_ref, xs_ref, wg_ref, wu_ref, wd_ref, ys_ref, wgb, wub, wdb):
    i = pl.program_id(0)

    @pl.when((i == 0) | (be_ref[i] != be_ref[jnp.maximum(i - 1, 0)]))
    def _():
        wgb[...] = wg_ref[0].astype(BF16)
        wub[...] = wu_ref[0].astype(BF16)
        wdb[...] = wd_ref[0].astype(BF16)

    @pl.when(i < nu_ref[0])
    def _():
        x = _tiles_to_rows(xs_ref, MOE_M).astype(BF16)
        g = _dot(x, wgb[...])
        u = _dot(x, wub[...])
        _rows_to_tiles(ys_ref, _dot((g * _sigmoid(g) * u).astype(BF16), wdb[...]))

    @pl.when(i >= nu_ref[0])
    def _():
        ys_ref[...] = jnp.zeros_like(ys_ref)


def _experts(block_e, n_used, xs, w_gate, w_up, w_down):
    n_rows = xs.shape[0] // ROW_TILE
    row_map = lambda i, be, nu: (jnp.minimum(i, nu[0] - 1), 0)
    out_map = lambda i, be, nu: (i, 0)
    w_map = lambda i, be, nu: (be[i], 0, 0)
    return pl.pallas_call(
        _expert_kernel,
        out_shape=jax.ShapeDtypeStruct((n_rows * ROW_TILE, LANES), F32),
        grid_spec=pltpu.PrefetchScalarGridSpec(
            num_scalar_prefetch=2,
            grid=(n_rows // MOE_M,),
            in_specs=[
                pl.BlockSpec((MOE_M * ROW_TILE, LANES), row_map),
                pl.BlockSpec((1, D_MODEL, EXPERT_HIDDEN), w_map),
                pl.BlockSpec((1, D_MODEL, EXPERT_HIDDEN), w_map),
                pl.BlockSpec((1, EXPERT_HIDDEN, D_MODEL), w_map),
            ],
            out_specs=pl.BlockSpec((MOE_M * ROW_TILE, LANES), out_map),
            scratch_shapes=[
                pltpu.VMEM((D_MODEL, EXPERT_HIDDEN), BF16),
                pltpu.VMEM((D_MODEL, EXPERT_HIDDEN), BF16),
                pltpu.VMEM((EXPERT_HIDDEN, D_MODEL), BF16),
            ],
        ),
        compiler_params=pltpu.CompilerParams(
            dimension_semantics=("arbitrary",), vmem_limit_bytes=VMEM_LIMIT),
        name="experts",
    )(block_e, n_used, xs, w_gate, w_up, w_down)


def _combine_kernel(final_norm, pos_ref, h2_ref, wt_ref, g_ref, ys_ref, o_ref, buf0_ref, buf1_ref, sem):
    tc = h2_ref.shape[0]
    bufs = (buf0_ref, buf1_ref)
    for r in range(tc):
        for k in range(TOP_K):
            _row_copy(ys_ref, pos_ref[0, k, r], bufs[k], r, sem).start(priority=k)
    for k in range(TOP_K):
        pltpu.make_async_copy(ys_ref.at[pl.ds(0, tc * ROW_TILE)], bufs[k], sem).wait()
    y = (h2_ref[...] + wt_ref[:, 0:1] * _tiles_to_rows(buf0_ref, tc)
         + wt_ref[:, 1:2] * _tiles_to_rows(buf1_ref, tc))
    o_ref[...] = _rms(y, g_ref[...]) if final_norm else y


def _combine(pos3, h2, wt, g, ys, final_norm):
    t = h2.shape[0]
    tc = COMB_T
    return pl.pallas_call(
        functools.partial(_combine_kernel, final_norm),
        out_shape=jax.ShapeDtypeStruct((t, D_MODEL), F32),
        grid=(t // tc,),
        in_specs=[
            pl.BlockSpec((1, TOP_K, tc), lambda i: (i, 0, 0), memory_space=pltpu.SMEM),
            pl.BlockSpec((tc, D_MODEL), lambda i: (i, 0)),
            pl.BlockSpec((tc, TOP_K), lambda i: (i, 0)),
            pl.BlockSpec((1, D_MODEL), lambda i: (0, 0)),
            pl.BlockSpec(memory_space=pl.ANY),
        ],
        out_specs=pl.BlockSpec((tc, D_MODEL), lambda i: (i, 0)),
        scratch_shapes=[
            pltpu.VMEM((tc * ROW_TILE, LANES), F32),
            pltpu.VMEM((tc * ROW_TILE, LANES), F32),
            pltpu.SemaphoreType.DMA,
        ],
        compiler_params=pltpu.CompilerParams(
            dimension_semantics=("arbitrary",), vmem_limit_bytes=VMEM_LIMIT),
        name="combine",
    )(pos3, h2, wt, g, ys)


def _routing_tables(rid, n_tok):
    e_flat = rid[:TOP_K].reshape(-1)
    onehot = (e_flat[None, :] == jnp.arange(N_EXPERTS, dtype=jnp.int32)[:, None]).astype(jnp.int32)
    csum = jnp.cumsum(onehot, axis=1)
    counts = csum[:, -1]
    rank = jnp.sum(onehot * csum, axis=0) - 1
    padded = (counts + MOE_M - 1) // MOE_M * MOE_M
    pend = jnp.cumsum(padded)
    pstart = pend - padded
    pos = jnp.sum(onehot * pstart[:, None], axis=0) + rank
    n_blocks = (TOP_K * n_tok + N_EXPERTS * (MOE_M - 1) + MOE_M - 1) // MOE_M
    n_used = pend[-1] // MOE_M
    blk = jnp.minimum(jnp.arange(n_blocks, dtype=jnp.int32), n_used - 1)
    block_e = jnp.sum((pend[None, :] <= (blk * MOE_M)[:, None]).astype(jnp.int32), axis=1)
    block_e = jnp.minimum(block_e, N_EXPERTS - 1)
    last_blk = jnp.maximum(pend - MOE_M, 0).astype(jnp.int32)
    has = (counts > 0).astype(jnp.int32)
    return (pos.reshape(TOP_K, n_tok).astype(jnp.int32), block_e.astype(jnp.int32),
            n_used.reshape(1).astype(jnp.int32), last_blk, has, n_blocks * MOE_M)


def _tile_pos(pos, tile):
    k, t = pos.shape
    return pos.reshape(k, t // tile, tile).transpose(1, 0, 2)


def kernel(x, mem, mix_norm_g, w_in, conv_w, w_conv_out, w_ret_out, w_mix_out, xa_norm_g, mem_norm_g,
           w_xa_q, w_xa_kv, w_xa_o, moe_norm_g, w_group, b_group, w_router, b_router, w_gate, w_up,
           w_down, final_norm_g):
    batch, seq, d = x.shape
    depth = w_in.shape[0]
    t = batch * seq
    h = x.reshape(t, d)
    for l in range(depth):
        proj = _in_proj(h, mix_norm_g[l][None], w_in[l])
        h1 = _mixer(proj, h, conv_w[l], w_conv_out[l].astype(BF16), w_ret_out[l].astype(BF16),
                    w_mix_out[l].astype(BF16), batch, seq)
        kv = _mem_kv(mem, mem_norm_g[l][None], w_xa_kv[l].astype(BF16))

        rw = jnp.zeros((ROUTE_ROWS, d), F32)
        rw = rw.at[0:N_GROUPS].set(w_group[l].T).at[ROUTE_E0:ROUTE_E0 + N_EXPERTS].set(w_router[l].T)
        rwh = rw.astype(BF16)
        rwl = (rw - rwh.astype(F32)).astype(BF16)
        rb = jnp.zeros((ROUTE_ROWS,), F32)
        rb = rb.at[0:N_GROUPS].set(b_group[l]).at[ROUTE_E0:ROUTE_E0 + N_EXPERTS].set(b_router[l])
        rb = jnp.broadcast_to(rb[:, None], (ROUTE_ROWS, LANES))

        h2, xn2, rid, rwt = _xattn(h1, kv, xa_norm_g[l][None], w_xa_q[l].astype(BF16),
                                   w_xa_o[l].astype(BF16), moe_norm_g[l][None], rwh, rwl, rb, batch, seq)

        pos, block_e, n_used, last_blk, has, n_rows = _routing_tables(rid, t)
        xs = _dispatch(last_blk, has, n_used, _tile_pos(pos, DISP_T), xn2, n_rows)
        ys = _experts(block_e, n_used, xs, w_gate[l], w_up[l], w_down[l])
        h = _combine(_tile_pos(pos, COMB_T), h2, rwt[:TOP_K].T, final_norm_g[None], ys,
                     final_norm=(l == depth - 1))
    return h.reshape(batch, seq, d)
```

```python
import functools

import numpy as np
import jax
import jax.numpy as jnp
from jax import lax
from jax.experimental import pallas as pl
from jax.experimental.pallas import tpu as pltpu

F32 = jnp.float32
BF16 = jnp.bfloat16

D_MODEL = 1024
CONV_WIDTH = 512
CONV_K = 3
RET_HEADS = 8
RET_DK = 64
RET_DV = 128
RET_CHUNK = 128
QK_WIDTH = RET_HEADS * RET_DK
V_WIDTH = RET_HEADS * RET_DV
ROPE_BASE = 10000.0
IN_WIDTH = 3 * CONV_WIDTH + 2 * QK_WIDTH + 2 * V_WIDTH + 2 * D_MODEL
OFF_XIN = 0
OFF_BG = OFF_XIN + CONV_WIDTH
OFF_CG = OFF_BG + CONV_WIDTH
OFF_Q = OFF_CG + CONV_WIDTH
OFF_K = OFF_Q + QK_WIDTH
OFF_V = OFF_K + QK_WIDTH
OFF_G = OFF_V + V_WIDTH
OFF_GATE_C = OFF_G + V_WIDTH
OFF_GATE_R = OFF_GATE_C + D_MODEL
MEM_LEN = 256
XA_HEADS = 4
XA_HEAD_DIM = D_MODEL // XA_HEADS
N_GROUPS = 4
EXPERTS_PER_GROUP = 8
N_EXPERTS = N_GROUPS * EXPERTS_PER_GROUP
TOP_K = 2
EXPERT_HIDDEN = D_MODEL // 2
EPS = 1e-6

LANES = 128
SUBLANES = 8
VMEM_LIMIT = 56 * 1024 * 1024

IN_TM = 2048
IN_TN = 1664
MIX_TS = 512
MIX_OUT_ROWS = 256
XA_TS = 512
ROUTE_ROWS = 40
ROUTE_E0 = 8
MOE_M = 512
DISP_T = 1024
COMB_T = 256


def _rms(x, g):
    ms = jnp.mean(x * x, axis=-1, keepdims=True)
    return x * lax.rsqrt(ms + EPS) * g


def _sigmoid(x):
    return 1.0 / (1.0 + jnp.exp(-x))


def _dot(a, b):
    return jnp.dot(a, b, preferred_element_type=F32)


def _dot_nt(a, b):
    return lax.dot_general(a, b, (((1,), (1,)), ((), ())), preferred_element_type=F32)


def _dot_tn(a, b):
    return lax.dot_general(a, b, (((0,), (0,)), ((), ())), preferred_element_type=F32)


ROW_TILE = D_MODEL // LANES


def _rows_to_tiles(ref, x):
    n = x.shape[0]
    for c in range(ROW_TILE):
        ref[pl.ds(c, n, stride=ROW_TILE), :] = x[:, c * LANES:(c + 1) * LANES]


def _tiles_to_rows(ref, n):
    return jnp.concatenate([ref[pl.ds(c, n, stride=ROW_TILE), :] for c in range(ROW_TILE)], axis=1)


def _inproj_kernel(x_ref, g_ref, w_ref, o_ref, xn_ref):
    @pl.when(pl.program_id(1) == 0)
    def _():
        xn_ref[...] = _rms(x_ref[...], g_ref[...]).astype(BF16)

    o_ref[...] = _dot(xn_ref[...], w_ref[...].astype(BF16)).astype(BF16)


def _in_proj(x2, g, w_in):
    t = x2.shape[0]
    return pl.pallas_call(
        _inproj_kernel,
        out_shape=jax.ShapeDtypeStruct((t, IN_WIDTH), BF16),
        grid=(t // IN_TM, IN_WIDTH // IN_TN),
        in_specs=[
            pl.BlockSpec((IN_TM, D_MODEL), lambda i, j: (i, 0)),
            pl.BlockSpec((1, D_MODEL), lambda i, j: (0, 0)),
            pl.BlockSpec((D_MODEL, IN_TN), lambda i, j: (0, j)),
        ],
        out_specs=pl.BlockSpec((IN_TM, IN_TN), lambda i, j: (i, j)),
        scratch_shapes=[pltpu.VMEM((IN_TM, D_MODEL), BF16)],
        compiler_params=pltpu.CompilerParams(
            dimension_semantics=("arbitrary", "arbitrary"), vmem_limit_bytes=VMEM_LIMIT),
        name="in_proj",
    )(x2, g, w_in)


def _retention_constants(seq):
    pos = np.arange(seq, dtype=np.float64)
    inv_freq = ROPE_BASE ** (-np.arange(0, RET_DK, 2, dtype=np.float64) / RET_DK)
    ang = pos[:, None] * inv_freq[None, :]
    cos, sin = np.cos(ang), np.sin(ang)
    cos_t = np.concatenate([cos, cos, cos, cos], axis=1)
    sin_t = np.concatenate([-sin, sin, -sin, sin], axis=1)
    log_g = np.log(1.0 - 2.0 ** (-5.0 - np.arange(RET_HEADS, dtype=np.float64)))
    idx = np.arange(RET_CHUNK, dtype=np.float64)
    diff = idx[:, None] - idx[None, :]
    decay = np.where(diff >= 0, np.exp(np.maximum(diff, 0.0)[None] * log_g[:, None, None]), 0.0)
    zeta = np.exp((RET_CHUNK - 1 - idx)[None, :] * log_g[:, None])
    xi = np.exp((idx + 1)[None, :] * log_g[:, None])
    zeta_t = np.repeat(zeta.T, RET_DK, axis=1)
    xi_t = np.repeat(xi.T, RET_DK, axis=1)
    chunk_decay = np.exp(RET_CHUNK * log_g)
    f = lambda a: jnp.asarray(a, dtype=F32)
    return f(cos_t), f(sin_t), f(xi_t), f(zeta_t), f(decay), [float(c) for c in chunk_decay]


def _mixer_kernel(chunk_decay, proj_ref, x_ref, cos_ref, sin_ref, xi_ref, zeta_ref, decay_ref,
                  convw_ref, wc_ref, wr_ref, wm_ref, o_ref, state_ref, tail_ref, yin_ref):
    ts = x_ref.shape[0]

    @pl.when(pl.program_id(1) == 0)
    def _():
        state_ref[...] = jnp.zeros_like(state_ref)
        tail_ref[...] = jnp.zeros_like(tail_ref)

    xin = proj_ref[:, OFF_XIN:OFF_XIN + CONV_WIDTH].astype(F32)
    bg = proj_ref[:, OFF_BG:OFF_BG + CONV_WIDTH].astype(F32)
    cg = proj_ref[:, OFF_CG:OFF_CG + CONV_WIDTH].astype(F32)
    u = cg * xin
    ue = jnp.concatenate([tail_ref[...], u], axis=0)
    u1 = pltpu.roll(ue, 1, 0)[SUBLANES:]
    u2 = pltpu.roll(ue, 2, 0)[SUBLANES:]
    tail_ref[...] = u[ts - SUBLANES:]
    c = convw_ref[2:3, :] * u + convw_ref[1:2, :] * u1 + convw_ref[0:1, :] * u2
    y_conv = _dot((bg * c).astype(BF16), wc_ref[...])

    lane = lax.broadcasted_iota(jnp.int32, (1, LANES), 1)
    low_half = (lane % RET_DK) < (RET_DK // 2)
    head_masks = [(lane // RET_DK) == j for j in range(LANES // RET_DK)]

    def rotary(t, cosv, sinv):
        outs = []
        for p in range(QK_WIDTH // LANES):
            tp = t[:, p * LANES:(p + 1) * LANES]
            fwd = pltpu.roll(tp, LANES - RET_DK // 2, 1)
            bwd = pltpu.roll(tp, RET_DK // 2, 1)
            outs.append(tp * cosv + jnp.where(low_half, fwd, bwd) * sinv)
        return jnp.concatenate(outs, axis=1)

    n_chunks = ts // RET_CHUNK
    chunk_rows = [slice(ci * RET_CHUNK, (ci + 1) * RET_CHUNK) for ci in range(n_chunks)]
    pairs = [(ci, h) for ci in range(n_chunks) for h in range(RET_HEADS)]
    zero = jnp.zeros((), BF16)
    qm, qxm, kb, kz = {}, {}, {}, {}
    for ci, rows in enumerate(chunk_rows):
        cosv = cos_ref[rows, :]
        sinv = sin_ref[rows, :]
        qr = rotary(proj_ref[rows, OFF_Q:OFF_Q + QK_WIDTH].astype(F32), cosv, sinv)
        kr = rotary(proj_ref[rows, OFF_K:OFF_K + QK_WIDTH].astype(F32), cosv, sinv) * (RET_DK ** -0.5)
        qb = qr.astype(BF16)
        qx = (qr * xi_ref[...]).astype(BF16)
        kb[ci] = kr.astype(BF16)
        kz[ci] = (kr * zeta_ref[...]).astype(BF16)
        for h in range(RET_HEADS):
            lanes = slice((h // 2) * LANES, (h // 2 + 1) * LANES)
            qm[ci, h] = jnp.where(head_masks[h % 2], qb[:, lanes], zero)
            qxm[ci, h] = jnp.where(head_masks[h % 2], qx[:, lanes], zero)

    def pair_lanes(h):
        return slice((h // 2) * LANES, (h // 2 + 1) * LANES)

    def v_of(ci, h):
        return proj_ref[chunk_rows[ci], OFF_V + h * RET_DV:OFF_V + (h + 1) * RET_DV]

    scores = {(ci, h): _dot_nt(qm[ci, h], kb[ci][:, pair_lanes(h)]) for ci, h in pairs}
    kv = {(ci, h): _dot_tn(kz[ci][:, pair_lanes(h)], v_of(ci, h)) for ci, h in pairs}
    probs = {(ci, h): (scores[ci, h] * decay_ref[h]).astype(BF16) for ci, h in pairs}
    inner = {(ci, h): _dot(probs[ci, h], v_of(ci, h)) for ci, h in pairs}
    st_before = {}
    for h in range(RET_HEADS):
        st = state_ref[h]
        for ci in range(n_chunks):
            st_before[ci, h] = st.astype(BF16)
            st = chunk_decay[h] * st + kv[ci, h]
        state_ref[h] = st
    o = {(ci, h): inner[ci, h] + _dot(qxm[ci, h], st_before[ci, h]) for ci, h in pairs}
    mu = {p: jnp.mean(o[p], axis=-1, keepdims=True) for p in pairs}
    dlt = {p: o[p] - mu[p] for p in pairs}
    var = {p: jnp.mean(dlt[p] * dlt[p], axis=-1, keepdims=True) for p in pairs}
    for ci, h in pairs:
        g = proj_ref[chunk_rows[ci], OFF_G + h * RET_DV:OFF_G + (h + 1) * RET_DV].astype(F32)
        yin_ref[chunk_rows[ci], h * RET_DV:(h + 1) * RET_DV] = (
            g * _sigmoid(g) * (dlt[ci, h] * lax.rsqrt(var[ci, h] + EPS))).astype(BF16)

    for r0 in range(0, ts, MIX_OUT_ROWS):
        rows = slice(r0, r0 + MIX_OUT_ROWS)
        y_ret = _dot(yin_ref[rows, :], wr_ref[...])
        gate_c = proj_ref[rows, OFF_GATE_C:OFF_GATE_C + D_MODEL].astype(F32)
        gate_r = proj_ref[rows, OFF_GATE_R:OFF_GATE_R + D_MODEL].astype(F32)
        merged = _sigmoid(gate_c) * y_conv[rows] + _sigmoid(gate_r) * y_ret
        o_ref[rows, :] = x_ref[rows, :] + _dot(merged.astype(BF16), wm_ref[...])


def _mixer(proj, x2, conv_w, wc_bf, wr_bf, wm_bf, batch, seq):
    ts = MIX_TS
    ns = seq // ts
    cos_t, sin_t, xi_t, zeta_t, decay, chunk_decay = _retention_constants(seq)
    full = lambda shape: pl.BlockSpec(shape, lambda b, s: (0,) * len(shape))
    return pl.pallas_call(
        functools.partial(_mixer_kernel, chunk_decay),
        out_shape=jax.ShapeDtypeStruct((batch * seq, D_MODEL), F32),
        grid=(batch, ns),
        in_specs=[
            pl.BlockSpec((ts, IN_WIDTH), lambda b, s: (b * ns + s, 0)),
            pl.BlockSpec((ts, D_MODEL), lambda b, s: (b * ns + s, 0)),
            pl.BlockSpec((ts, LANES), lambda b, s: (s, 0)),
            pl.BlockSpec((ts, LANES), lambda b, s: (s, 0)),
            full((RET_CHUNK, QK_WIDTH)),
            full((RET_CHUNK, QK_WIDTH)),
            full((RET_HEADS, RET_CHUNK, RET_CHUNK)),
            full((CONV_K, CONV_WIDTH)),
            full((CONV_WIDTH, D_MODEL)),
            full((V_WIDTH, D_MODEL)),
            full((D_MODEL, D_MODEL)),
        ],
        out_specs=pl.BlockSpec((ts, D_MODEL), lambda b, s: (b * ns + s, 0)),
        scratch_shapes=[
            pltpu.VMEM((RET_HEADS, LANES, RET_DV), F32),
            pltpu.VMEM((SUBLANES, CONV_WIDTH), F32),
            pltpu.VMEM((ts, V_WIDTH), BF16),
        ],
        compiler_params=pltpu.CompilerParams(
            dimension_semantics=("arbitrary", "arbitrary"), vmem_limit_bytes=VMEM_LIMIT),
        name="mixer",
    )(proj, x2, cos_t, sin_t, xi_t, zeta_t, decay, conv_w, wc_bf, wr_bf, wm_bf)


def _memkv_kernel(m_ref, g_ref, w_ref, o_ref):
    o_ref[0] = _dot(_rms(m_ref[0], g_ref[...]).astype(BF16), w_ref[...]).astype(BF16)


def _mem_kv(mem, g, wkv_bf):
    b = mem.shape[0]
    return pl.pallas_call(
        _memkv_kernel,
        out_shape=jax.ShapeDtypeStruct((b, MEM_LEN, 2 * D_MODEL), BF16),
        grid=(b,),
        in_specs=[
            pl.BlockSpec((1, MEM_LEN, D_MODEL), lambda i: (i, 0, 0)),
            pl.BlockSpec((1, D_MODEL), lambda i: (0, 0)),
            pl.BlockSpec((D_MODEL, 2 * D_MODEL), lambda i: (0, 0)),
        ],
        out_specs=pl.BlockSpec((1, MEM_LEN, 2 * D_MODEL), lambda i: (i, 0, 0)),
        compiler_params=pltpu.CompilerParams(
            dimension_semantics=("arbitrary",), vmem_limit_bytes=VMEM_LIMIT),
        name="mem_kv",
    )(mem, g, wkv_bf)


def _xattn_kernel(h_ref, kv_ref, gx_ref, wq_ref, wo_ref, gm_ref, rwh_ref, rwl_ref, rb_ref,
                  h2_ref, xn2_ref, rid_ref, rwt_ref):
    ts = h_ref.shape[0]
    h = h_ref[...]
    q = _dot(_rms(h, gx_ref[...]).astype(BF16), wq_ref[...])
    qb = q.astype(BF16)
    head_cols = [slice(hd * XA_HEAD_DIM, (hd + 1) * XA_HEAD_DIM) for hd in range(XA_HEADS)]
    scores = [_dot_nt(qb[:, cols], kv_ref[0, :, cols]) * (XA_HEAD_DIM ** -0.5) for cols in head_cols]
    probs = []
    for s in scores:
        e = jnp.exp(s - jnp.max(s, axis=-1, keepdims=True))
        probs.append((e / jnp.sum(e, axis=-1, keepdims=True)).astype(BF16))
    outs = [_dot(p, kv_ref[0, :, D_MODEL + hd * XA_HEAD_DIM:D_MODEL + (hd + 1) * XA_HEAD_DIM])
            for hd, p in enumerate(probs)]
    h2 = h + _dot(jnp.concatenate(outs, axis=1).astype(BF16), wo_ref[...])
    h2_ref[...] = h2
    xn = _rms(h2, gm_ref[...])
    _rows_to_tiles(xn2_ref, xn)

    hi = xn.astype(BF16)
    lo = (xn - hi.astype(F32)).astype(BF16)
    lg = (_dot_nt(rwh_ref[...], hi) + _dot_nt(rwh_ref[...], lo) + _dot_nt(rwl_ref[...], hi)
          + rb_ref[:, 0:1])

    gl = lg[0:N_GROUPS]
    ge = jnp.exp(gl - jnp.max(gl, axis=0, keepdims=True))
    gp = ge / jnp.sum(ge, axis=0, keepdims=True)
    p_g = gp[0:1]
    g_idx = jnp.zeros((1, ts), jnp.int32)
    for i in range(1, N_GROUPS):
        better = gp[i:i + 1] > p_g
        g_idx = jnp.where(better, i, g_idx)
        p_g = jnp.where(better, gp[i:i + 1], p_g)
    sel = jnp.zeros((EXPERTS_PER_GROUP, ts), F32)
    for i in range(N_GROUPS):
        r0 = ROUTE_E0 + i * EXPERTS_PER_GROUP
        sel = jnp.where(g_idx == i, lg[r0:r0 + EXPERTS_PER_GROUP], sel)
    se = jnp.exp(sel - jnp.max(sel, axis=0, keepdims=True))
    sp = se / jnp.sum(se, axis=0, keepdims=True)
    ridx = lax.broadcasted_iota(jnp.int32, (EXPERTS_PER_GROUP, ts), 0)
    m1 = jnp.max(sp, axis=0, keepdims=True)
    i1 = jnp.min(jnp.where(sp == m1, ridx, EXPERTS_PER_GROUP), axis=0, keepdims=True)
    sp2 = jnp.where(ridx == i1, -1.0, sp)
    m2 = jnp.max(sp2, axis=0, keepdims=True)
    i2 = jnp.min(jnp.where(sp2 == m2, ridx, EXPERTS_PER_GROUP), axis=0, keepdims=True)
    den = m1 + m2
    rid_ref[...] = jnp.zeros_like(rid_ref)
    rwt_ref[...] = jnp.zeros_like(rwt_ref)
    rid_ref[0:1, :] = g_idx * EXPERTS_PER_GROUP + i1
    rid_ref[1:2, :] = g_idx * EXPERTS_PER_GROUP + i2
    rwt_ref[0:1, :] = p_g * m1 / den
    rwt_ref[1:2, :] = p_g * m2 / den


def _xattn(h1, kv, gx, wq_bf, wo_bf, gm, rwh, rwl, rb, batch, seq):
    ts = XA_TS
    ns = seq // ts
    t = batch * seq
    full = lambda shape: pl.BlockSpec(shape, lambda b, s: (0,) * len(shape))
    return pl.pallas_call(
        _xattn_kernel,
        out_shape=(
            jax.ShapeDtypeStruct((t, D_MODEL), F32),
            jax.ShapeDtypeStruct((t * ROW_TILE, LANES), F32),
            jax.ShapeDtypeStruct((SUBLANES, t), jnp.int32),
            jax.ShapeDtypeStruct((SUBLANES, t), F32),
        ),
        grid=(batch, ns),
        in_specs=[
            pl.BlockSpec((ts, D_MODEL), lambda b, s: (b * ns + s, 0)),
            pl.BlockSpec((1, MEM_LEN, 2 * D_MODEL), lambda b, s: (b, 0, 0)),
            full((1, D_MODEL)),
            full((D_MODEL, D_MODEL)),
            full((D_MODEL, D_MODEL)),
            full((1, D_MODEL)),
            full((ROUTE_ROWS, D_MODEL)),
            full((ROUTE_ROWS, D_MODEL)),
            full((ROUTE_ROWS, LANES)),
        ],
        out_specs=(
            pl.BlockSpec((ts, D_MODEL), lambda b, s: (b * ns + s, 0)),
            pl.BlockSpec((ts * ROW_TILE, LANES), lambda b, s: (b * ns + s, 0)),
            pl.BlockSpec((SUBLANES, ts), lambda b, s: (0, b * ns + s)),
            pl.BlockSpec((SUBLANES, ts), lambda b, s: (0, b * ns + s)),
        ),
        compiler_params=pltpu.CompilerParams(
            dimension_semantics=("arbitrary", "arbitrary"), vmem_limit_bytes=VMEM_LIMIT),
        name="xattn",
    )(h1, kv, gx, wq_bf, wo_bf, gm, rwh, rwl, rb)


def _tile_of(ref, row):
    start = row * ROW_TILE
    if not isinstance(start, int):
        start = pl.multiple_of(start, ROW_TILE)
    return ref.at[pl.ds(start, ROW_TILE)]


def _row_copy(src_ref, src_row, dst_ref, dst_row, sem):
    return pltpu.make_async_copy(_tile_of(src_ref, src_row), _tile_of(dst_ref, dst_row), sem)


def _dispatch_kernel(last_ref, has_ref, nu_ref, pos_ref, xn_ref, xs_ref, zero_ref, sem, zsem):
    td = xn_ref.shape[0] // ROW_TILE
    blk = MOE_M * ROW_TILE
    n_blocks = xs_ref.shape[0] // blk

    @pl.when(pl.program_id(0) == 0)
    def _():
        zero_ref[...] = jnp.zeros_like(zero_ref)

        def fill(start):
            start = pl.multiple_of(start * ROW_TILE, blk)
            return pltpu.make_async_copy(zero_ref, xs_ref.at[pl.ds(start, blk)], zsem)

        for e in range(N_EXPERTS):
            @pl.when(has_ref[e] > 0)
            def _():
                fill(last_ref[e]).start()
        for e in range(N_EXPERTS):
            @pl.when(has_ref[e] > 0)
            def _():
                fill(last_ref[e]).wait()

        def start_tail(b, carry):
            fill(b * MOE_M).start()
            return carry

        def wait_tail(b, carry):
            fill(b * MOE_M).wait()
            return carry

        lax.fori_loop(nu_ref[0], n_blocks, start_tail, 0)
        lax.fori_loop(nu_ref[0], n_blocks, wait_tail, 0)

    for r in range(td):
        for k in range(TOP_K):
            _row_copy(xn_ref, r, xs_ref, pos_ref[0, k, r], sem).start(priority=k)
    for k in range(TOP_K):
        pltpu.make_async_copy(xn_ref, xs_ref.at[pl.ds(0, td * ROW_TILE)], sem).wait()


def _dispatch(last_blk, has, n_used, pos3, xn2, n_rows):
    t = xn2.shape[0] // ROW_TILE
    td = DISP_T
    return pl.pallas_call(
        _dispatch_kernel,
        out_shape=jax.ShapeDtypeStruct((n_rows * ROW_TILE, LANES), F32),
        grid_spec=pltpu.PrefetchScalarGridSpec(
            num_scalar_prefetch=3,
            grid=(t // td,),
            in_specs=[
                pl.BlockSpec((1, TOP_K, td), lambda i, *_: (i, 0, 0), memory_space=pltpu.SMEM),
                pl.BlockSpec((td * ROW_TILE, LANES), lambda i, *_: (i, 0)),
            ],
            out_specs=pl.BlockSpec(memory_space=pl.ANY),
            scratch_shapes=[
                pltpu.VMEM((MOE_M * ROW_TILE, LANES), F32),
                pltpu.SemaphoreType.DMA,
                pltpu.SemaphoreType.DMA,
            ],
        ),
        compiler_params=pltpu.CompilerParams(
            dimension_semantics=("arbitrary",), vmem_limit_bytes=VMEM_LIMIT),
        name="dispatch",
    )(last_blk, has, n_used, pos3, xn2)


def _expert_kernel(be_ref, nu_ref, xs_ref, wg_ref, wu_ref, wd_ref, ys_ref, wgb, wub, wdb):
    i = pl.program_id(0)

    @pl.when((i == 0) | (be_ref[i] != be_ref[jnp.maximum(i - 1, 0)]))
    def _():
        wgb[...] = wg_ref[0].astype(BF16)
        wub[...] = wu_ref[0].astype(BF16)
        wdb[...] = wd_ref[0].astype(BF16)

    @pl.when(i < nu_ref[0])
    def _():
        x = _tiles_to_rows(xs_ref, MOE_M).astype(BF16)
        g = _dot(x, wgb[...])
        u = _dot(x, wub[...])
        _rows_to_tiles(ys_ref, _dot((g * _sigmoid(g) * u).astype(BF16), wdb[...]))

    @pl.when(i >= nu_ref[0])
    def _():
        ys_ref[...] = jnp.zeros_like(ys_ref)


def _experts(block_e, n_used, xs, w_gate, w_up, w_down):
    n_rows = xs.shape[0] // ROW_TILE
    row_map = lambda i, be, nu: (jnp.minimum(i, nu[0] - 1), 0)
    out_map = lambda i, be, nu: (i, 0)
    w_map = lambda i, be, nu: (be[i], 0, 0)
    return pl.pallas_call(
        _expert_kernel,
        out_shape=jax.ShapeDtypeStruct((n_rows * ROW_TILE, LANES), F32),
        grid_spec=pltpu.PrefetchScalarGridSpec(
            num_scalar_prefetch=2,
            grid=(n_rows // MOE_M,),
            in_specs=[
                pl.BlockSpec((MOE_M * ROW_TILE, LANES), row_map),
                pl.BlockSpec((1, D_MODEL, EXPERT_HIDDEN), w_map),
                pl.BlockSpec((1, D_MODEL, EXPERT_HIDDEN), w_map),
                pl.BlockSpec((1, EXPERT_HIDDEN, D_MODEL), w_map),
            ],
            out_specs=pl.BlockSpec((MOE_M * ROW_TILE, LANES), out_map),
            scratch_shapes=[
                pltpu.VMEM((D_MODEL, EXPERT_HIDDEN), BF16),
                pltpu.VMEM((D_MODEL, EXPERT_HIDDEN), BF16),
                pltpu.VMEM((EXPERT_HIDDEN, D_MODEL), BF16),
            ],
        ),
        compiler_params=pltpu.CompilerParams(
            dimension_semantics=("arbitrary",), vmem_limit_bytes=VMEM_LIMIT),
        name="experts",
    )(block_e, n_used, xs, w_gate, w_up, w_down)


def _combine_kernel(final_norm, pos_ref, nxt_ref, h2_ref, wt_ref, g_ref, ys_ref, o_ref,
                    b00_ref, b01_ref, b10_ref, b11_ref, sem):
    i = pl.program_id(0)
    n_steps = pl.num_programs(0)
    tc = h2_ref.shape[0]
    bufs = ((b00_ref, b01_ref), (b10_ref, b11_ref))

    def issue(rows_ref, s):
        for r in range(tc):
            for k in range(TOP_K):
                _row_copy(ys_ref, rows_ref[0, k, r], bufs[s][k], r, sem.at[s]).start(priority=k)

    @pl.when(i == 0)
    def _():
        issue(pos_ref, 0)

    for s in range(2):
        @pl.when(i % 2 == s)
        def _():
            @pl.when(i + 1 < n_steps)
            def _():
                issue(nxt_ref, 1 - s)

            for k in range(TOP_K):
                pltpu.make_async_copy(ys_ref.at[pl.ds(0, tc * ROW_TILE)], bufs[s][k], sem.at[s]).wait()
            y = (h2_ref[...] + wt_ref[:, 0:1] * _tiles_to_rows(bufs[s][0], tc)
                 + wt_ref[:, 1:2] * _tiles_to_rows(bufs[s][1], tc))
            o_ref[...] = _rms(y, g_ref[...]) if final_norm else y


def _combine(pos3, h2, wt, g, ys, final_norm):
    t = h2.shape[0]
    tc = COMB_T
    nt = t // tc
    return pl.pallas_call(
        functools.partial(_combine_kernel, final_norm),
        out_shape=jax.ShapeDtypeStruct((t, D_MODEL), F32),
        grid=(nt,),
        in_specs=[
            pl.BlockSpec((1, TOP_K, tc), lambda i: (i, 0, 0), memory_space=pltpu.SMEM),
            pl.BlockSpec((1, TOP_K, tc), lambda i: (jnp.minimum(i + 1, nt - 1), 0, 0),
                         memory_space=pltpu.SMEM),
            pl.BlockSpec((tc, D_MODEL), lambda i: (i, 0)),
            pl.BlockSpec((tc, TOP_K), lambda i: (i, 0)),
            pl.BlockSpec((1, D_MODEL), lambda i: (0, 0)),
            pl.BlockSpec(memory_space=pl.ANY),
        ],
        out_specs=pl.BlockSpec((tc, D_MODEL), lambda i: (i, 0)),
        scratch_shapes=[
            pltpu.VMEM((tc * ROW_TILE, LANES), F32),
            pltpu.VMEM((tc * ROW_TILE, LANES), F32),
            pltpu.VMEM((tc * ROW_TILE, LANES), F32),
            pltpu.VMEM((tc * ROW_TILE, LANES), F32),
            pltpu.SemaphoreType.DMA((2,)),
        ],
        compiler_params=pltpu.CompilerParams(
            dimension_semantics=("arbitrary",), vmem_limit_bytes=VMEM_LIMIT),
        name="combine",
    )(pos3, pos3, h2, wt, g, ys)


def _routing_tables(rid, n_tok):
    e_flat = rid[:TOP_K].reshape(-1)
    onehot = (e_flat[None, :] == jnp.arange(N_EXPERTS, dtype=jnp.int32)[:, None]).astype(jnp.int32)
    csum = jnp.cumsum(onehot, axis=1)
    counts = csum[:, -1]
    rank = jnp.sum(onehot * csum, axis=0) - 1
    padded = (counts + MOE_M - 1) // MOE_M * MOE_M
    pend = jnp.cumsum(padded)
    pstart = pend - padded
    pos = jnp.sum(onehot * pstart[:, None], axis=0) + rank
    n_blocks = (TOP_K * n_tok + N_EXPERTS * (MOE_M - 1) + MOE_M - 1) // MOE_M
    n_used = pend[-1] // MOE_M
    blk = jnp.minimum(jnp.arange(n_blocks, dtype=jnp.int32), n_used - 1)
    block_e = jnp.sum((pend[None, :] <= (blk * MOE_M)[:, None]).astype(jnp.int32), axis=1)
    block_e = jnp.minimum(block_e, N_EXPERTS - 1)
    last_blk = jnp.maximum(pend - MOE_M, 0).astype(jnp.int32)
    has = (counts > 0).astype(jnp.int32)
    return (pos.reshape(TOP_K, n_tok).astype(jnp.int32), block_e.astype(jnp.int32),
            n_used.reshape(1).astype(jnp.int32), last_blk, has, n_blocks * MOE_M)


def _tile_pos(pos, tile):
    k, t = pos.shape
    return pos.reshape(k, t // tile, tile).transpose(1, 0, 2)


def kernel(x, mem, mix_norm_g, w_in, conv_w, w_conv_out, w_ret_out, w_mix_out, xa_norm_g, mem_norm_g,
           w_xa_q, w_xa_kv, w_xa_o, moe_norm_g, w_group, b_group, w_router, b_router, w_gate, w_up,
           w_down, final_norm_g):
    batch, seq, d = x.shape
    depth = w_in.shape[0]
    t = batch * seq
    h = x.reshape(t, d)
    for l in range(depth):
        proj = _in_proj(h, mix_norm_g[l][None], w_in[l])
        h1 = _mixer(proj, h, conv_w[l], w_conv_out[l].astype(BF16), w_ret_out[l].astype(BF16),
                    w_mix_out[l].astype(BF16), batch, seq)
        kv = _mem_kv(mem, mem_norm_g[l][None], w_xa_kv[l].astype(BF16))

        rw = jnp.zeros((ROUTE_ROWS, d), F32)
        rw = rw.at[0:N_GROUPS].set(w_group[l].T).at[ROUTE_E0:ROUTE_E0 + N_EXPERTS].set(w_router[l].T)
        rwh = rw.astype(BF16)
        rwl = (rw - rwh.astype(F32)).astype(BF16)
        rb = jnp.zeros((ROUTE_ROWS,), F32)
        rb = rb.at[0:N_GROUPS].set(b_group[l]).at[ROUTE_E0:ROUTE_E0 + N_EXPERTS].set(b_router[l])
        rb = jnp.broadcast_to(rb[:, None], (ROUTE_ROWS, LANES))

        h2, xn2, rid, rwt = _xattn(h1, kv, xa_norm_g[l][None], w_xa_q[l].astype(BF16),
                                   w_xa_o[l].astype(BF16), moe_norm_g[l][None], rwh, rwl, rb, batch, seq)

        pos, block_e, n_used, last_blk, has, n_rows = _routing_tables(rid, t)
        xs = _dispatch(last_blk, has, n_used, _tile_pos(pos, DISP_T), xn2, n_rows)
        ys = _experts(block_e, n_used, xs, w_gate[l], w_up[l], w_down[l])
        h = _combine(_tile_pos(pos, COMB_T), h2, rwt[:TOP_K].T, final_norm_g[None], ys,
                     final_norm=(l == depth - 1))
    return h.reshape(batch, seq, d)
```

```python
import functools

import numpy as np
import jax
import jax.numpy as jnp
from jax import lax
from jax.experimental import pallas as pl
from jax.experimental.pallas import tpu as pltpu

F32 = jnp.float32
BF16 = jnp.bfloat16

D_MODEL = 1024
CONV_WIDTH = 512
CONV_K = 3
RET_HEADS = 8
RET_DK = 64
RET_DV = 128
RET_CHUNK = 128
QK_WIDTH = RET_HEADS * RET_DK
V_WIDTH = RET_HEADS * RET_DV
ROPE_BASE = 10000.0
IN_WIDTH = 3 * CONV_WIDTH + 2 * QK_WIDTH + 2 * V_WIDTH + 2 * D_MODEL
OFF_XIN = 0
OFF_BG = OFF_XIN + CONV_WIDTH
OFF_CG = OFF_BG + CONV_WIDTH
OFF_Q = OFF_CG + CONV_WIDTH
OFF_K = OFF_Q + QK_WIDTH
OFF_V = OFF_K + QK_WIDTH
OFF_G = OFF_V + V_WIDTH
OFF_GATE_C = OFF_G + V_WIDTH
OFF_GATE_R = OFF_GATE_C + D_MODEL
MEM_LEN = 256
XA_HEADS = 4
XA_HEAD_DIM = D_MODEL // XA_HEADS
N_GROUPS = 4
EXPERTS_PER_GROUP = 8
N_EXPERTS = N_GROUPS * EXPERTS_PER_GROUP
TOP_K = 2
EXPERT_HIDDEN = D_MODEL // 2
EPS = 1e-6

LANES = 128
SUBLANES = 8
VMEM_LIMIT = 56 * 1024 * 1024

IN_TM = 2048
IN_TN = 1664
MIX_TS = 512
MIX_OUT_ROWS = 256
XA_TS = 512
ROUTE_ROWS = 40
ROUTE_E0 = 8
MOE_M = 512
DISP_T = 1024
COMB_T = 256


def _rms(x, g):
    ms = jnp.mean(x * x, axis=-1, keepdims=True)
    return x * lax.rsqrt(ms + EPS) * g


def _sigmoid(x):
    return 1.0 / (1.0 + jnp.exp(-x))


def _dot(a, b):
    return jnp.dot(a, b, preferred_element_type=F32)


def _dot_nt(a, b):
    return lax.dot_general(a, b, (((1,), (1,)), ((), ())), preferred_element_type=F32)


def _dot_tn(a, b):
    return lax.dot_general(a, b, (((0,), (0,)), ((), ())), preferred_element_type=F32)


ROW_TILE = D_MODEL // LANES


def _rows_to_tiles(ref, x):
    n = x.shape[0]
    for c in range(ROW_TILE):
        ref[pl.ds(c, n, stride=ROW_TILE), :] = x[:, c * LANES:(c + 1) * LANES]


def _tiles_to_rows(ref, n):
    return jnp.concatenate([ref[pl.ds(c, n, stride=ROW_TILE), :] for c in range(ROW_TILE)], axis=1)


def _inproj_kernel(x_ref, g_ref, w_ref, o_ref, xn_ref):
    @pl.when(pl.program_id(1) == 0)
    def _():
        xn_ref[...] = _rms(x_ref[...], g_ref[...]).astype(BF16)

    o_ref[...] = _dot(xn_ref[...], w_ref[...].astype(BF16)).astype(BF16)


def _in_proj(x2, g, w_in):
    t = x2.shape[0]
    return pl.pallas_call(
        _inproj_kernel,
        out_shape=jax.ShapeDtypeStruct((t, IN_WIDTH), BF16),
        grid=(t // IN_TM, IN_WIDTH // IN_TN),
        in_specs=[
            pl.BlockSpec((IN_TM, D_MODEL), lambda i, j: (i, 0)),
            pl.BlockSpec((1, D_MODEL), lambda i, j: (0, 0)),
            pl.BlockSpec((D_MODEL, IN_TN), lambda i, j: (0, j)),
        ],
        out_specs=pl.BlockSpec((IN_TM, IN_TN), lambda i, j: (i, j)),
        scratch_shapes=[pltpu.VMEM((IN_TM, D_MODEL), BF16)],
        compiler_params=pltpu.CompilerParams(
            dimension_semantics=("arbitrary", "arbitrary"), vmem_limit_bytes=VMEM_LIMIT),
        name="in_proj",
    )(x2, g, w_in)


def _retention_constants(seq):
    pos = np.arange(seq, dtype=np.float64)
    inv_freq = ROPE_BASE ** (-np.arange(0, RET_DK, 2, dtype=np.float64) / RET_DK)
    ang = pos[:, None] * inv_freq[None, :]
    cos, sin = np.cos(ang), np.sin(ang)
    cos_t = np.concatenate([cos, cos, cos, cos], axis=1)
    sin_t = np.concatenate([-sin, sin, -sin, sin], axis=1)
    log_g = np.log(1.0 - 2.0 ** (-5.0 - np.arange(RET_HEADS, dtype=np.float64)))
    idx = np.arange(RET_CHUNK, dtype=np.float64)
    diff = idx[:, None] - idx[None, :]
    decay = np.where(diff >= 0, np.exp(np.maximum(diff, 0.0)[None] * log_g[:, None, None]), 0.0)
    zeta = np.exp((RET_CHUNK - 1 - idx)[None, :] * log_g[:, None])
    xi = np.exp((idx + 1)[None, :] * log_g[:, None])
    zeta_t = np.repeat(zeta.T, RET_DK, axis=1)
    xi_t = np.repeat(xi.T, RET_DK, axis=1)
    chunk_decay = np.exp(RET_CHUNK * log_g)
    f = lambda a: jnp.asarray(a, dtype=F32)
    return f(cos_t), f(sin_t), f(xi_t), f(zeta_t), f(decay), [float(c) for c in chunk_decay]


def _mixer_kernel(chunk_decay, proj_ref, x_ref, cos_ref, sin_ref, xi_ref, zeta_ref, decay_ref,
                  convw_ref, wc_ref, wr_ref, wm_ref, o_ref, state_ref, tail_ref, yin_ref):
    ts = x_ref.shape[0]

    @pl.when(pl.program_id(1) == 0)
    def _():
        state_ref[...] = jnp.zeros_like(state_ref)
        tail_ref[...] = jnp.zeros_like(tail_ref)

    xin = proj_ref[:, OFF_XIN:OFF_XIN + CONV_WIDTH].astype(F32)
    bg = proj_ref[:, OFF_BG:OFF_BG + CONV_WIDTH].astype(F32)
    cg = proj_ref[:, OFF_CG:OFF_CG + CONV_WIDTH].astype(F32)
    u = cg * xin
    ue = jnp.concatenate([tail_ref[...], u], axis=0)
    u1 = pltpu.roll(ue, 1, 0)[SUBLANES:]
    u2 = pltpu.roll(ue, 2, 0)[SUBLANES:]
    tail_ref[...] = u[ts - SUBLANES:]
    c = convw_ref[2:3, :] * u + convw_ref[1:2, :] * u1 + convw_ref[0:1, :] * u2
    y_conv = _dot((bg * c).astype(BF16), wc_ref[...])

    lane = lax.broadcasted_iota(jnp.int32, (1, LANES), 1)
    low_half = (lane % RET_DK) < (RET_DK // 2)
    head_masks = [(lane // RET_DK) == j for j in range(LANES // RET_DK)]

    def rotary(t, cosv, sinv):
        outs = []
        for p in range(QK_WIDTH // LANES):
            tp = t[:, p * LANES:(p + 1) * LANES]
            fwd = pltpu.roll(tp, LANES - RET_DK // 2, 1)
            bwd = pltpu.roll(tp, RET_DK // 2, 1)
            outs.append(tp * cosv + jnp.where(low_half, fwd, bwd) * sinv)
        return jnp.concatenate(outs, axis=1)

    n_chunks = ts // RET_CHUNK
    chunk_rows = [slice(ci * RET_CHUNK, (ci + 1) * RET_CHUNK) for ci in range(n_chunks)]
    pairs = [(ci, h) for ci in range(n_chunks) for h in range(RET_HEADS)]
    zero = jnp.zeros((), BF16)
    qm, qxm, kb, kz = {}, {}, {}, {}
    for ci, rows in enumerate(chunk_rows):
        cosv = cos_ref[rows, :]
        sinv = sin_ref[rows, :]
        qr = rotary(proj_ref[rows, OFF_Q:OFF_Q + QK_WIDTH].astype(F32), cosv, sinv)
        kr = rotary(proj_ref[rows, OFF_K:OFF_K + QK_WIDTH].astype(F32), cosv, sinv) * (RET_DK ** -0.5)
        qb = qr.astype(BF16)
        qx = (qr * xi_ref[...]).astype(BF16)
        kb[ci] = kr.astype(BF16)
        kz[ci] = (kr * zeta_ref[...]).astype(BF16)
        for h in range(RET_HEADS):
            lanes = slice((h // 2) * LANES, (h // 2 + 1) * LANES)
            qm[ci, h] = jnp.where(head_masks[h % 2], qb[:, lanes], zero)
            qxm[ci, h] = jnp.where(head_masks[h % 2], qx[:, lanes], zero)

    def pair_lanes(h):
        return slice((h // 2) * LANES, (h // 2 + 1) * LANES)

    def v_of(ci, h):
        return proj_ref[chunk_rows[ci], OFF_V + h * RET_DV:OFF_V + (h + 1) * RET_DV]

    scores = {(ci, h): _dot_nt(qm[ci, h], kb[ci][:, pair_lanes(h)]) for ci, h in pairs}
    kv = {(ci, h): _dot_tn(kz[ci][:, pair_lanes(h)], v_of(ci, h)) for ci, h in pairs}
    probs = {(ci, h): (scores[ci, h] * decay_ref[h]).astype(BF16) for ci, h in pairs}
    inner = {(ci, h): _dot(probs[ci, h], v_of(ci, h)) for ci, h in pairs}
    st_before = {}
    for h in range(RET_HEADS):
        st = state_ref[h]
        for ci in range(n_chunks):
            st_before[ci, h] = st.astype(BF16)
            st = chunk_decay[h] * st + kv[ci, h]
        state_ref[h] = st
    o = {(ci, h): inner[ci, h] + _dot(qxm[ci, h], st_before[ci, h]) for ci, h in pairs}
    mu = {p: jnp.mean(o[p], axis=-1, keepdims=True) for p in pairs}
    dlt = {p: o[p] - mu[p] for p in pairs}
    var = {p: jnp.mean(dlt[p] * dlt[p], axis=-1, keepdims=True) for p in pairs}
    for ci, h in pairs:
        g = proj_ref[chunk_rows[ci], OFF_G + h * RET_DV:OFF_G + (h + 1) * RET_DV].astype(F32)
        yin_ref[chunk_rows[ci], h * RET_DV:(h + 1) * RET_DV] = (
            g * _sigmoid(g) * (dlt[ci, h] * lax.rsqrt(var[ci, h] + EPS))).astype(BF16)

    for r0 in range(0, ts, MIX_OUT_ROWS):
        rows = slice(r0, r0 + MIX_OUT_ROWS)
        y_ret = _dot(yin_ref[rows, :], wr_ref[...])
        gate_c = proj_ref[rows, OFF_GATE_C:OFF_GATE_C + D_MODEL].astype(F32)
        gate_r = proj_ref[rows, OFF_GATE_R:OFF_GATE_R + D_MODEL].astype(F32)
        merged = _sigmoid(gate_c) * y_conv[rows] + _sigmoid(gate_r) * y_ret
        o_ref[rows, :] = x_ref[rows, :] + _dot(merged.astype(BF16), wm_ref[...])


def _mixer(proj, x2, conv_w, wc_bf, wr_bf, wm_bf, batch, seq):
    ts = MIX_TS
    ns = seq // ts
    cos_t, sin_t, xi_t, zeta_t, decay, chunk_decay = _retention_constants(seq)
    full = lambda shape: pl.BlockSpec(shape, lambda b, s: (0,) * len(shape))
    return pl.pallas_call(
        functools.partial(_mixer_kernel, chunk_decay),
        out_shape=jax.ShapeDtypeStruct((batch * seq, D_MODEL), F32),
        grid=(batch, ns),
        in_specs=[
            pl.BlockSpec((ts, IN_WIDTH), lambda b, s: (b * ns + s, 0)),
            pl.BlockSpec((ts, D_MODEL), lambda b, s: (b * ns + s, 0)),
            pl.BlockSpec((ts, LANES), lambda b, s: (s, 0)),
            pl.BlockSpec((ts, LANES), lambda b, s: (s, 0)),
            full((RET_CHUNK, QK_WIDTH)),
            full((RET_CHUNK, QK_WIDTH)),
            full((RET_HEADS, RET_CHUNK, RET_CHUNK)),
            full((CONV_K, CONV_WIDTH)),
            full((CONV_WIDTH, D_MODEL)),
            full((V_WIDTH, D_MODEL)),
            full((D_MODEL, D_MODEL)),
        ],
        out_specs=pl.BlockSpec((ts, D_MODEL), lambda b, s: (b * ns + s, 0)),
        scratch_shapes=[
            pltpu.VMEM((RET_HEADS, LANES, RET_DV), F32),
            pltpu.VMEM((SUBLANES, CONV_WIDTH), F32),
            pltpu.VMEM((ts, V_WIDTH), BF16),
        ],
        compiler_params=pltpu.CompilerParams(
            dimension_semantics=("arbitrary", "arbitrary"), vmem_limit_bytes=VMEM_LIMIT),
        name="mixer",
    )(proj, x2, cos_t, sin_t, xi_t, zeta_t, decay, conv_w, wc_bf, wr_bf, wm_bf)


def _memkv_kernel(m_ref, g_ref, w_ref, o_ref):
    o_ref[0] = _dot(_rms(m_ref[0], g_ref[...]).astype(BF16), w_ref[...]).astype(BF16)


def _mem_kv(mem, g, wkv_bf):
    b = mem.shape[0]
    return pl.pallas_call(
        _memkv_kernel,
        out_shape=jax.ShapeDtypeStruct((b, MEM_LEN, 2 * D_MODEL), BF16),
        grid=(b,),
        in_specs=[
            pl.BlockSpec((1, MEM_LEN, D_MODEL), lambda i: (i, 0, 0)),
            pl.BlockSpec((1, D_MODEL), lambda i: (0, 0)),
            pl.BlockSpec((D_MODEL, 2 * D_MODEL), lambda i: (0, 0)),
        ],
        out_specs=pl.BlockSpec((1, MEM_LEN, 2 * D_MODEL), lambda i: (i, 0, 0)),
        compiler_params=pltpu.CompilerParams(
            dimension_semantics=("arbitrary",), vmem_limit_bytes=VMEM_LIMIT),
        name="mem_kv",
    )(mem, g, wkv_bf)


def _xattn_kernel(h_ref, kv_ref, gx_ref, wq_ref, wo_ref, gm_ref, rwh_ref, rwl_ref, rb_ref,
                  h2_ref, xn2_ref, rid_ref, rwt_ref):
    ts = h_ref.shape[0]
    h = h_ref[...]
    q = _dot(_rms(h, gx_ref[...]).astype(BF16), wq_ref[...])
    qb = q.astype(BF16)
    head_cols = [slice(hd * XA_HEAD_DIM, (hd + 1) * XA_HEAD_DIM) for hd in range(XA_HEADS)]
    scores = [_dot_nt(qb[:, cols], kv_ref[0, :, cols]) * (XA_HEAD_DIM ** -0.5) for cols in head_cols]
    probs = []
    for s in scores:
        e = jnp.exp(s - jnp.max(s, axis=-1, keepdims=True))
        probs.append((e / jnp.sum(e, axis=-1, keepdims=True)).astype(BF16))
    outs = [_dot(p, kv_ref[0, :, D_MODEL + hd * XA_HEAD_DIM:D_MODEL + (hd + 1) * XA_HEAD_DIM])
            for hd, p in enumerate(probs)]
    h2 = h + _dot(jnp.concatenate(outs, axis=1).astype(BF16), wo_ref[...])
    h2_ref[...] = h2
    xn = _rms(h2, gm_ref[...])
    _rows_to_tiles(xn2_ref, xn)

    hi = xn.astype(BF16)
    lo = (xn - hi.astype(F32)).astype(BF16)
    lg = (_dot_nt(rwh_ref[...], hi) + _dot_nt(rwh_ref[...], lo) + _dot_nt(rwl_ref[...], hi)
          + rb_ref[:, 0:1])

    gl = lg[0:N_GROUPS]
    ge = jnp.exp(gl - jnp.max(gl, axis=0, keepdims=True))
    gp = ge / jnp.sum(ge, axis=0, keepdims=True)
    p_g = gp[0:1]
    g_idx = jnp.zeros((1, ts), jnp.int32)
    for i in range(1, N_GROUPS):
        better = gp[i:i + 1] > p_g
        g_idx = jnp.where(better, i, g_idx)
        p_g = jnp.where(better, gp[i:i + 1], p_g)
    sel = jnp.zeros((EXPERTS_PER_GROUP, ts), F32)
    for i in range(N_GROUPS):
        r0 = ROUTE_E0 + i * EXPERTS_PER_GROUP
        sel = jnp.where(g_idx == i, lg[r0:r0 + EXPERTS_PER_GROUP], sel)
    se = jnp.exp(sel - jnp.max(sel, axis=0, keepdims=True))
    sp = se / jnp.sum(se, axis=0, keepdims=True)
    ridx = lax.broadcasted_iota(jnp.int32, (EXPERTS_PER_GROUP, ts), 0)
    m1 = jnp.max(sp, axis=0, keepdims=True)
    i1 = jnp.min(jnp.where(sp == m1, ridx, EXPERTS_PER_GROUP), axis=0, keepdims=True)
    sp2 = jnp.where(ridx == i1, -1.0, sp)
    m2 = jnp.max(sp2, axis=0, keepdims=True)
    i2 = jnp.min(jnp.where(sp2 == m2, ridx, EXPERTS_PER_GROUP), axis=0, keepdims=True)
    den = m1 + m2
    rid_ref[...] = jnp.zeros_like(rid_ref)
    rwt_ref[...] = jnp.zeros_like(rwt_ref)
    rid_ref[0:1, :] = g_idx * EXPERTS_PER_GROUP + i1
    rid_ref[1:2, :] = g_idx * EXPERTS_PER_GROUP + i2
    rwt_ref[0:1, :] = p_g * m1 / den
    rwt_ref[1:2, :] = p_g * m2 / den


def _xattn(h1, kv, gx, wq_bf, wo_bf, gm, rwh, rwl, rb, batch, seq):
    ts = XA_TS
    ns = seq // ts
    t = batch * seq
    full = lambda shape: pl.BlockSpec(shape, lambda b, s: (0,) * len(shape))
    return pl.pallas_call(
        _xattn_kernel,
        out_shape=(
            jax.ShapeDtypeStruct((t, D_MODEL), F32),
            jax.ShapeDtypeStruct((t * ROW_TILE, LANES), F32),
            jax.ShapeDtypeStruct((SUBLANES, t), jnp.int32),
            jax.ShapeDtypeStruct((SUBLANES, t), F32),
        ),
        grid=(batch, ns),
        in_specs=[
            pl.BlockSpec((ts, D_MODEL), lambda b, s: (b * ns + s, 0)),
            pl.BlockSpec((1, MEM_LEN, 2 * D_MODEL), lambda b, s: (b, 0, 0)),
            full((1, D_MODEL)),
            full((D_MODEL, D_MODEL)),
            full((D_MODEL, D_MODEL)),
            full((1, D_MODEL)),
            full((ROUTE_ROWS, D_MODEL)),
            full((ROUTE_ROWS, D_MODEL)),
            full((ROUTE_ROWS, LANES)),
        ],
        out_specs=(
            pl.BlockSpec((ts, D_MODEL), lambda b, s: (b * ns + s, 0)),
            pl.BlockSpec((ts * ROW_TILE, LANES), lambda b, s: (b * ns + s, 0)),
            pl.BlockSpec((SUBLANES, ts), lambda b, s: (0, b * ns + s)),
            pl.BlockSpec((SUBLANES, ts), lambda b, s: (0, b * ns + s)),
        ),
        compiler_params=pltpu.CompilerParams(
            dimension_semantics=("arbitrary", "arbitrary"), vmem_limit_bytes=VMEM_LIMIT),
        name="xattn",
    )(h1, kv, gx, wq_bf, wo_bf, gm, rwh, rwl, rb)


def _tile_of(ref, row):
    start = row * ROW_TILE
    if not isinstance(start, int):
        start = pl.multiple_of(start, ROW_TILE)
    return ref.at[pl.ds(start, ROW_TILE)]


def _row_copy(src_ref, src_row, dst_ref, dst_row, sem):
    return pltpu.make_async_copy(_tile_of(src_ref, src_row), _tile_of(dst_ref, dst_row), sem)


def _dispatch_kernel(last_ref, has_ref, nu_ref, pos_ref, xn_ref, xs_ref, zero_ref, sem, zsem):
    td = xn_ref.shape[0] // ROW_TILE
    blk = MOE_M * ROW_TILE
    n_blocks = xs_ref.shape[0] // blk

    def fill(start):
        start = pl.multiple_of(start * ROW_TILE, blk)
        return pltpu.make_async_copy(zero_ref, xs_ref.at[pl.ds(start, blk)], zsem)

    @pl.when(pl.program_id(0) == 0)
    def _():
        zero_ref[...] = jnp.zeros_like(zero_ref)

        for e in range(N_EXPERTS):
            @pl.when(has_ref[e] > 0)
            def _():
                fill(last_ref[e]).start()
        for e in range(N_EXPERTS):
            @pl.when(has_ref[e] > 0)
            def _():
                fill(last_ref[e]).wait()

        def start_tail(b, carry):
            fill(b * MOE_M).start()
            return carry

        lax.fori_loop(nu_ref[0], n_blocks, start_tail, 0)

    for r in range(td):
        for k in range(TOP_K):
            _row_copy(xn_ref, r, xs_ref, pos_ref[0, k, r], sem).start(priority=k)
    for k in range(TOP_K):
        pltpu.make_async_copy(xn_ref, xs_ref.at[pl.ds(0, td * ROW_TILE)], sem).wait()

    @pl.when(pl.program_id(0) == pl.num_programs(0) - 1)
    def _():
        def wait_tail(b, carry):
            fill(b * MOE_M).wait()
            return carry

        lax.fori_loop(nu_ref[0], n_blocks, wait_tail, 0)


def _dispatch(last_blk, has, n_used, pos3, xn2, n_rows):
    t = xn2.shape[0] // ROW_TILE
    td = DISP_T
    return pl.pallas_call(
        _dispatch_kernel,
        out_shape=jax.ShapeDtypeStruct((n_rows * ROW_TILE, LANES), F32),
        grid_spec=pltpu.PrefetchScalarGridSpec(
            num_scalar_prefetch=3,
            grid=(t // td,),
            in_specs=[
                pl.BlockSpec((1, TOP_K, td), lambda i, *_: (i, 0, 0), memory_space=pltpu.SMEM),
                pl.BlockSpec((td * ROW_TILE, LANES), lambda i, *_: (i, 0)),
            ],
            out_specs=pl.BlockSpec(memory_space=pl.ANY),
            scratch_shapes=[
                pltpu.VMEM((MOE_M * ROW_TILE, LANES), F32),
                pltpu.SemaphoreType.DMA,
                pltpu.SemaphoreType.DMA,
            ],
        ),
        compiler_params=pltpu.CompilerParams(
            dimension_semantics=("arbitrary",), vmem_limit_bytes=VMEM_LIMIT),
        name="dispatch",
    )(last_blk, has, n_used, pos3, xn2)


def _expert_kernel(be_ref, ne_ref, nu_ref, xs_ref, wg_ref, wu_ref, wd_ref, ys_ref,
                   wgf, wuf, wdf, wgb, wub, wdb, wsem):
    i = pl.program_id(0)
    e = be_ref[i]

    def fetch(expert):
        return [pltpu.make_async_copy(src.at[expert], dst, wsem)
                for src, dst in ((wg_ref, wgf), (wu_ref, wuf), (wd_ref, wdf))]

    @pl.when(i == 0)
    def _():
        for c in fetch(e):
            c.start()

    @pl.when((i == 0) | (e != be_ref[jnp.maximum(i - 1, 0)]))
    def _():
        for c in fetch(e):
            c.wait()
        wgb[...] = wgf[...].astype(BF16)
        wub[...] = wuf[...].astype(BF16)
        wdb[...] = wdf[...].astype(BF16)

        @pl.when(ne_ref[i] != e)
        def _():
            for c in fetch(ne_ref[i]):
                c.start()

    @pl.when(i < nu_ref[0])
    def _():
        x = _tiles_to_rows(xs_ref, MOE_M).astype(BF16)
        g = _dot(x, wgb[...])
        u = _dot(x, wub[...])
        _rows_to_tiles(ys_ref, _dot((g * _sigmoid(g) * u).astype(BF16), wdb[...]))

    @pl.when(i >= nu_ref[0])
    def _():
        ys_ref[...] = jnp.zeros_like(ys_ref)


def _experts(block_e, next_e, n_used, xs, w_gate, w_up, w_down):
    n_rows = xs.shape[0] // ROW_TILE
    row_map = lambda i, be, ne, nu: (jnp.minimum(i, nu[0] - 1), 0)
    out_map = lambda i, be, ne, nu: (i, 0)
    return pl.pallas_call(
        _expert_kernel,
        out_shape=jax.ShapeDtypeStruct((n_rows * ROW_TILE, LANES), F32),
        grid_spec=pltpu.PrefetchScalarGridSpec(
            num_scalar_prefetch=3,
            grid=(n_rows // MOE_M,),
            in_specs=[
                pl.BlockSpec((MOE_M * ROW_TILE, LANES), row_map),
                pl.BlockSpec(memory_space=pl.ANY),
                pl.BlockSpec(memory_space=pl.ANY),
                pl.BlockSpec(memory_space=pl.ANY),
            ],
            out_specs=pl.BlockSpec((MOE_M * ROW_TILE, LANES), out_map),
            scratch_shapes=[
                pltpu.VMEM((D_MODEL, EXPERT_HIDDEN), F32),
                pltpu.VMEM((D_MODEL, EXPERT_HIDDEN), F32),
                pltpu.VMEM((EXPERT_HIDDEN, D_MODEL), F32),
                pltpu.VMEM((D_MODEL, EXPERT_HIDDEN), BF16),
                pltpu.VMEM((D_MODEL, EXPERT_HIDDEN), BF16),
                pltpu.VMEM((EXPERT_HIDDEN, D_MODEL), BF16),
                pltpu.SemaphoreType.DMA,
            ],
        ),
        compiler_params=pltpu.CompilerParams(
            dimension_semantics=("arbitrary",), vmem_limit_bytes=VMEM_LIMIT),
        name="experts",
    )(block_e, next_e, n_used, xs, w_gate, w_up, w_down)


def _combine_kernel(final_norm, pos_ref, nxt_ref, h2_ref, wt_ref, g_ref, ys_ref, o_ref,
                    b00_ref, b01_ref, b10_ref, b11_ref, sem):
    i = pl.program_id(0)
    n_steps = pl.num_programs(0)
    tc = h2_ref.shape[0]
    bufs = ((b00_ref, b01_ref), (b10_ref, b11_ref))

    def issue(rows_ref, s):
        for r in range(tc):
            for k in range(TOP_K):
                _row_copy(ys_ref, rows_ref[0, k, r], bufs[s][k], r, sem.at[s]).start(priority=k)

    @pl.when(i == 0)
    def _():
        issue(pos_ref, 0)

    for s in range(2):
        @pl.when(i % 2 == s)
        def _():
            @pl.when(i + 1 < n_steps)
            def _():
                issue(nxt_ref, 1 - s)

            for k in range(TOP_K):
                pltpu.make_async_copy(ys_ref.at[pl.ds(0, tc * ROW_TILE)], bufs[s][k], sem.at[s]).wait()
            y = (h2_ref[...] + wt_ref[:, 0:1] * _tiles_to_rows(bufs[s][0], tc)
                 + wt_ref[:, 1:2] * _tiles_to_rows(bufs[s][1], tc))
            o_ref[...] = _rms(y, g_ref[...]) if final_norm else y


def _combine(pos3, h2, wt, g, ys, final_norm):
    t = h2.shape[0]
    tc = COMB_T
    nt = t // tc
    return pl.pallas_call(
        functools.partial(_combine_kernel, final_norm),
        out_shape=jax.ShapeDtypeStruct((t, D_MODEL), F32),
        grid=(nt,),
        in_specs=[
            pl.BlockSpec((1, TOP_K, tc), lambda i: (i, 0, 0), memory_space=pltpu.SMEM),
            pl.BlockSpec((1, TOP_K, tc), lambda i: (jnp.minimum(i + 1, nt - 1), 0, 0),
                         memory_space=pltpu.SMEM),
            pl.BlockSpec((tc, D_MODEL), lambda i: (i, 0)),
            pl.BlockSpec((tc, TOP_K), lambda i: (i, 0)),
            pl.BlockSpec((1, D_MODEL), lambda i: (0, 0)),
            pl.BlockSpec(memory_space=pl.ANY),
        ],
        out_specs=pl.BlockSpec((tc, D_MODEL), lambda i: (i, 0)),
        scratch_shapes=[
            pltpu.VMEM((tc * ROW_TILE, LANES), F32),
            pltpu.VMEM((tc * ROW_TILE, LANES), F32),
            pltpu.VMEM((tc * ROW_TILE, LANES), F32),
            pltpu.VMEM((tc * ROW_TILE, LANES), F32),
            pltpu.SemaphoreType.DMA((2,)),
        ],
        compiler_params=pltpu.CompilerParams(
            dimension_semantics=("arbitrary",), vmem_limit_bytes=VMEM_LIMIT),
        name="combine",
    )(pos3, pos3, h2, wt, g, ys)


def _routing_tables(rid, n_tok):
    e_flat = rid[:TOP_K].reshape(-1)
    onehot = (e_flat[None, :] == jnp.arange(N_EXPERTS, dtype=jnp.int32)[:, None]).astype(jnp.int32)
    csum = jnp.cumsum(onehot, axis=1)
    counts = csum[:, -1]
    rank = jnp.sum(onehot * csum, axis=0) - 1
    padded = (counts + MOE_M - 1) // MOE_M * MOE_M
    pend = jnp.cumsum(padded)
    pstart = pend - padded
    pos = jnp.sum(onehot * pstart[:, None], axis=0) + rank
    n_blocks = (TOP_K * n_tok + N_EXPERTS * (MOE_M - 1) + MOE_M - 1) // MOE_M
    n_used = pend[-1] // MOE_M
    blk = jnp.minimum(jnp.arange(n_blocks, dtype=jnp.int32), n_used - 1)
    block_e = jnp.sum((pend[None, :] <= (blk * MOE_M)[:, None]).astype(jnp.int32), axis=1)
    block_e = jnp.minimum(block_e, N_EXPERTS - 1)
    last_blk = jnp.maximum(pend - MOE_M, 0).astype(jnp.int32)
    has = (counts > 0).astype(jnp.int32)
    ids = jnp.arange(N_EXPERTS, dtype=jnp.int32)
    later = (ids[None, :] > ids[:, None]) & (counts[None, :] > 0)
    nxt = jnp.min(jnp.where(later, ids[None, :], N_EXPERTS), axis=1)
    nxt = jnp.where(nxt == N_EXPERTS, ids, nxt)
    next_e = jnp.sum(jnp.where(block_e[:, None] == ids[None, :], nxt[None, :], 0), axis=1)
    return (pos.reshape(TOP_K, n_tok).astype(jnp.int32), block_e.astype(jnp.int32),
            next_e.astype(jnp.int32), n_used.reshape(1).astype(jnp.int32), last_blk, has,
            n_blocks * MOE_M)


def _tile_pos(pos, tile):
    k, t = pos.shape
    return pos.reshape(k, t // tile, tile).transpose(1, 0, 2)


def kernel(x, mem, mix_norm_g, w_in, conv_w, w_conv_out, w_ret_out, w_mix_out, xa_norm_g, mem_norm_g,
           w_xa_q, w_xa_kv, w_xa_o, moe_norm_g, w_group, b_group, w_router, b_router, w_gate, w_up,
           w_down, final_norm_g):
    batch, seq, d = x.shape
    depth = w_in.shape[0]
    t = batch * seq
    h = x.reshape(t, d)
    for l in range(depth):
        proj = _in_proj(h, mix_norm_g[l][None], w_in[l])
        h1 = _mixer(proj, h, conv_w[l], w_conv_out[l].astype(BF16), w_ret_out[l].astype(BF16),
                    w_mix_out[l].astype(BF16), batch, seq)
        kv = _mem_kv(mem, mem_norm_g[l][None], w_xa_kv[l].astype(BF16))

        rw = jnp.zeros((ROUTE_ROWS, d), F32)
        rw = rw.at[0:N_GROUPS].set(w_group[l].T).at[ROUTE_E0:ROUTE_E0 + N_EXPERTS].set(w_router[l].T)
        rwh = rw.astype(BF16)
        rwl = (rw - rwh.astype(F32)).astype(BF16)
        rb = jnp.zeros((ROUTE_ROWS,), F32)
        rb = rb.at[0:N_GROUPS].set(b_group[l]).at[ROUTE_E0:ROUTE_E0 + N_EXPERTS].set(b_router[l])
        rb = jnp.broadcast_to(rb[:, None], (ROUTE_ROWS, LANES))

        h2, xn2, rid, rwt = _xattn(h1, kv, xa_norm_g[l][None], w_xa_q[l].astype(BF16),
                                   w_xa_o[l].astype(BF16), moe_norm_g[l][None], rwh, rwl, rb, batch, seq)

        pos, block_e, next_e, n_used, last_blk, has, n_rows = _routing_tables(rid, t)
        xs = _dispatch(last_blk, has, n_used, _tile_pos(pos, DISP_T), xn2, n_rows)
        ys = _experts(block_e, next_e, n_used, xs, w_gate[l], w_up[l], w_down[l])
        h = _combine(_tile_pos(pos, COMB_T), h2, rwt[:TOP_K].T, final_norm_g[None], ys,
                     final_norm=(l == depth - 1))
    return h.reshape(batch, seq, d)
```

```python
import functools

import numpy as np
import jax
import jax.numpy as jnp
from jax import lax
from jax.experimental import pallas as pl
from jax.experimental.pallas import tpu as pltpu

F32 = jnp.float32
BF16 = jnp.bfloat16

D_MODEL = 1024
CONV_WIDTH = 512
CONV_K = 3
RET_HEADS = 8
RET_DK = 64
RET_DV = 128
RET_CHUNK = 128
QK_WIDTH = RET_HEADS * RET_DK
V_WIDTH = RET_HEADS * RET_DV
ROPE_BASE = 10000.0
IN_WIDTH = 3 * CONV_WIDTH + 2 * QK_WIDTH + 2 * V_WIDTH + 2 * D_MODEL
OFF_XIN = 0
OFF_BG = OFF_XIN + CONV_WIDTH
OFF_CG = OFF_BG + CONV_WIDTH
OFF_Q = OFF_CG + CONV_WIDTH
OFF_K = OFF_Q + QK_WIDTH
OFF_V = OFF_K + QK_WIDTH
OFF_G = OFF_V + V_WIDTH
OFF_GATE_C = OFF_G + V_WIDTH
OFF_GATE_R = OFF_GATE_C + D_MODEL
MEM_LEN = 256
XA_HEADS = 4
XA_HEAD_DIM = D_MODEL // XA_HEADS
N_GROUPS = 4
EXPERTS_PER_GROUP = 8
N_EXPERTS = N_GROUPS * EXPERTS_PER_GROUP
TOP_K = 2
EXPERT_HIDDEN = D_MODEL // 2
EPS = 1e-6

LANES = 128
SUBLANES = 8
VMEM_LIMIT = 56 * 1024 * 1024

IN_TM = 2048
IN_TN = 1664
MIX_TS = 512
MIX_OUT_ROWS = 256
MIX_STAGE_CHUNKS = 4
XA_TS = 512
ROUTE_ROWS = 40
ROUTE_E0 = 8
MOE_M = 512
DISP_T = 1024
COMB_T = 256


def _rms(x, g):
    ms = jnp.mean(x * x, axis=-1, keepdims=True)
    return x * lax.rsqrt(ms + EPS) * g


def _sigmoid(x):
    return 1.0 / (1.0 + jnp.exp(-x))


def _dot(a, b):
    return jnp.dot(a, b, preferred_element_type=F32)


def _dot_nt(a, b):
    return lax.dot_general(a, b, (((1,), (1,)), ((), ())), preferred_element_type=F32)


def _dot_tn(a, b):
    return lax.dot_general(a, b, (((0,), (0,)), ((), ())), preferred_element_type=F32)


ROW_TILE = D_MODEL // LANES


def _rows_to_tiles(ref, x):
    n = x.shape[0]
    for c in range(ROW_TILE):
        ref[pl.ds(c, n, stride=ROW_TILE), :] = x[:, c * LANES:(c + 1) * LANES]


def _tiles_to_rows(ref, n):
    return jnp.concatenate([ref[pl.ds(c, n, stride=ROW_TILE), :] for c in range(ROW_TILE)], axis=1)


def _inproj_kernel(x_ref, g_ref, w_ref, o_ref, xn_ref):
    @pl.when(pl.program_id(1) == 0)
    def _():
        xn_ref[...] = _rms(x_ref[...], g_ref[...]).astype(BF16)

    o_ref[...] = _dot(xn_ref[...], w_ref[...].astype(BF16)).astype(BF16)


def _in_proj(x2, g, w_in):
    t = x2.shape[0]
    return pl.pallas_call(
        _inproj_kernel,
        out_shape=jax.ShapeDtypeStruct((t, IN_WIDTH), BF16),
        grid=(t // IN_TM, IN_WIDTH // IN_TN),
        in_specs=[
            pl.BlockSpec((IN_TM, D_MODEL), lambda i, j: (i, 0)),
            pl.BlockSpec((1, D_MODEL), lambda i, j: (0, 0)),
            pl.BlockSpec((D_MODEL, IN_TN), lambda i, j: (0, j)),
        ],
        out_specs=pl.BlockSpec((IN_TM, IN_TN), lambda i, j: (i, j)),
        scratch_shapes=[pltpu.VMEM((IN_TM, D_MODEL), BF16)],
        compiler_params=pltpu.CompilerParams(
            dimension_semantics=("arbitrary", "arbitrary"), vmem_limit_bytes=VMEM_LIMIT),
        name="in_proj",
    )(x2, g, w_in)


def _retention_constants(seq):
    pos = np.arange(seq, dtype=np.float64)
    inv_freq = ROPE_BASE ** (-np.arange(0, RET_DK, 2, dtype=np.float64) / RET_DK)
    ang = pos[:, None] * inv_freq[None, :]
    cos, sin = np.cos(ang), np.sin(ang)
    cos_t = np.concatenate([cos, cos, cos, cos], axis=1)
    sin_t = np.concatenate([-sin, sin, -sin, sin], axis=1)
    log_g = np.log(1.0 - 2.0 ** (-5.0 - np.arange(RET_HEADS, dtype=np.float64)))
    idx = np.arange(RET_CHUNK, dtype=np.float64)
    diff = idx[:, None] - idx[None, :]
    decay = np.where(diff >= 0, np.exp(np.maximum(diff, 0.0)[None] * log_g[:, None, None]), 0.0)
    zeta = np.exp((RET_CHUNK - 1 - idx)[None, :] * log_g[:, None])
    xi = np.exp((idx + 1)[None, :] * log_g[:, None])
    zeta_t = np.repeat(zeta.T, RET_DK, axis=1)
    xi_t = np.repeat(xi.T, RET_DK, axis=1)
    chunk_decay = np.exp(RET_CHUNK * log_g)
    f = lambda a: jnp.asarray(a, dtype=F32)
    return f(cos_t), f(sin_t), f(xi_t), f(zeta_t), f(decay), [float(c) for c in chunk_decay]


def _mixer_kernel(chunk_decay, proj_ref, x_ref, cos_ref, sin_ref, xi_ref, zeta_ref, decay_ref,
                  convw_ref, wc_ref, wr_ref, wm_ref, o_ref, state_ref, tail_ref, yin_ref):
    ts = x_ref.shape[0]

    @pl.when(pl.program_id(1) == 0)
    def _():
        state_ref[...] = jnp.zeros_like(state_ref)
        tail_ref[...] = jnp.zeros_like(tail_ref)

    xin = proj_ref[:, OFF_XIN:OFF_XIN + CONV_WIDTH].astype(F32)
    bg = proj_ref[:, OFF_BG:OFF_BG + CONV_WIDTH].astype(F32)
    cg = proj_ref[:, OFF_CG:OFF_CG + CONV_WIDTH].astype(F32)
    u = cg * xin
    ue = jnp.concatenate([tail_ref[...], u], axis=0)
    u1 = pltpu.roll(ue, 1, 0)[SUBLANES:]
    u2 = pltpu.roll(ue, 2, 0)[SUBLANES:]
    tail_ref[...] = u[ts - SUBLANES:]
    c = convw_ref[2:3, :] * u + convw_ref[1:2, :] * u1 + convw_ref[0:1, :] * u2
    y_conv = _dot((bg * c).astype(BF16), wc_ref[...])

    lane = lax.broadcasted_iota(jnp.int32, (1, LANES), 1)
    low_half = (lane % RET_DK) < (RET_DK // 2)
    head_masks = [(lane // RET_DK) == j for j in range(LANES // RET_DK)]

    def rotary(t, cosv, sinv):
        outs = []
        for p in range(QK_WIDTH // LANES):
            tp = t[:, p * LANES:(p + 1) * LANES]
            fwd = pltpu.roll(tp, LANES - RET_DK // 2, 1)
            bwd = pltpu.roll(tp, RET_DK // 2, 1)
            outs.append(tp * cosv + jnp.where(low_half, fwd, bwd) * sinv)
        return jnp.concatenate(outs, axis=1)

    n_chunks = ts // RET_CHUNK
    chunk_rows = [slice(ci * RET_CHUNK, (ci + 1) * RET_CHUNK) for ci in range(n_chunks)]
    zero = jnp.zeros((), BF16)
    qm, qxm, kb, kz = {}, {}, {}, {}
    for ci, rows in enumerate(chunk_rows):
        cosv = cos_ref[rows, :]
        sinv = sin_ref[rows, :]
        qr = rotary(proj_ref[rows, OFF_Q:OFF_Q + QK_WIDTH].astype(F32), cosv, sinv)
        kr = rotary(proj_ref[rows, OFF_K:OFF_K + QK_WIDTH].astype(F32), cosv, sinv) * (RET_DK ** -0.5)
        qb = qr.astype(BF16)
        qx = (qr * xi_ref[...]).astype(BF16)
        kb[ci] = kr.astype(BF16)
        kz[ci] = (kr * zeta_ref[...]).astype(BF16)
        for h in range(RET_HEADS):
            lanes = slice((h // 2) * LANES, (h // 2 + 1) * LANES)
            qm[ci, h] = jnp.where(head_masks[h % 2], qb[:, lanes], zero)
            qxm[ci, h] = jnp.where(head_masks[h % 2], qx[:, lanes], zero)

    def pair_lanes(h):
        return slice((h // 2) * LANES, (h // 2 + 1) * LANES)

    def v_of(ci, h):
        return proj_ref[chunk_rows[ci], OFF_V + h * RET_DV:OFF_V + (h + 1) * RET_DV]

    state = [state_ref[h] for h in range(RET_HEADS)]
    for c0 in range(0, n_chunks, MIX_STAGE_CHUNKS):
        pairs = [(ci, h) for ci in range(c0, c0 + MIX_STAGE_CHUNKS) for h in range(RET_HEADS)]
        scores = {(ci, h): _dot_nt(qm[ci, h], kb[ci][:, pair_lanes(h)]) for ci, h in pairs}
        kv = {(ci, h): _dot_tn(kz[ci][:, pair_lanes(h)], v_of(ci, h)) for ci, h in pairs}
        probs = {(ci, h): (scores[ci, h] * decay_ref[h]).astype(BF16) for ci, h in pairs}
        inner = {(ci, h): _dot(probs[ci, h], v_of(ci, h)) for ci, h in pairs}
        st_before = {}
        for ci, h in pairs:
            st_before[ci, h] = state[h].astype(BF16)
            state[h] = chunk_decay[h] * state[h] + kv[ci, h]
        o = {(ci, h): inner[ci, h] + _dot(qxm[ci, h], st_before[ci, h]) for ci, h in pairs}
        mu = {p: jnp.mean(o[p], axis=-1, keepdims=True) for p in pairs}
        dlt = {p: o[p] - mu[p] for p in pairs}
        var = {p: jnp.mean(dlt[p] * dlt[p], axis=-1, keepdims=True) for p in pairs}
        for ci, h in pairs:
            g = proj_ref[chunk_rows[ci], OFF_G + h * RET_DV:OFF_G + (h + 1) * RET_DV].astype(F32)
            yin_ref[chunk_rows[ci], h * RET_DV:(h + 1) * RET_DV] = (
                g * _sigmoid(g) * (dlt[ci, h] * lax.rsqrt(var[ci, h] + EPS))).astype(BF16)
    for h in range(RET_HEADS):
        state_ref[h] = state[h]

    for r0 in range(0, ts, MIX_OUT_ROWS):
        rows = slice(r0, r0 + MIX_OUT_ROWS)
        y_ret = _dot(yin_ref[rows, :], wr_ref[...])
        gate_c = proj_ref[rows, OFF_GATE_C:OFF_GATE_C + D_MODEL].astype(F32)
        gate_r = proj_ref[rows, OFF_GATE_R:OFF_GATE_R + D_MODEL].astype(F32)
        merged = _sigmoid(gate_c) * y_conv[rows] + _sigmoid(gate_r) * y_ret
        o_ref[rows, :] = x_ref[rows, :] + _dot(merged.astype(BF16), wm_ref[...])


def _mixer(proj, x2, conv_w, wc_bf, wr_bf, wm_bf, batch, seq):
    ts = MIX_TS
    ns = seq // ts
    cos_t, sin_t, xi_t, zeta_t, decay, chunk_decay = _retention_constants(seq)
    full = lambda shape: pl.BlockSpec(shape, lambda b, s: (0,) * len(shape))
    return pl.pallas_call(
        functools.partial(_mixer_kernel, chunk_decay),
        out_shape=jax.ShapeDtypeStruct((batch * seq, D_MODEL), F32),
        grid=(batch, ns),
        in_specs=[
            pl.BlockSpec((ts, IN_WIDTH), lambda b, s: (b * ns + s, 0)),
            pl.BlockSpec((ts, D_MODEL), lambda b, s: (b * ns + s, 0)),
            pl.BlockSpec((ts, LANES), lambda b, s: (s, 0)),
            pl.BlockSpec((ts, LANES), lambda b, s: (s, 0)),
            full((RET_CHUNK, QK_WIDTH)),
            full((RET_CHUNK, QK_WIDTH)),
            full((RET_HEADS, RET_CHUNK, RET_CHUNK)),
            full((CONV_K, CONV_WIDTH)),
            full((CONV_WIDTH, D_MODEL)),
            full((V_WIDTH, D_MODEL)),
            full((D_MODEL, D_MODEL)),
        ],
        out_specs=pl.BlockSpec((ts, D_MODEL), lambda b, s: (b * ns + s, 0)),
        scratch_shapes=[
            pltpu.VMEM((RET_HEADS, LANES, RET_DV), F32),
            pltpu.VMEM((SUBLANES, CONV_WIDTH), F32),
            pltpu.VMEM((ts, V_WIDTH), BF16),
        ],
        compiler_params=pltpu.CompilerParams(
            dimension_semantics=("arbitrary", "arbitrary"), vmem_limit_bytes=VMEM_LIMIT),
        name="mixer",
    )(proj, x2, cos_t, sin_t, xi_t, zeta_t, decay, conv_w, wc_bf, wr_bf, wm_bf)


def _memkv_kernel(m_ref, g_ref, w_ref, o_ref):
    o_ref[0] = _dot(_rms(m_ref[0], g_ref[...]).astype(BF16), w_ref[...]).astype(BF16)


def _mem_kv(mem, g, wkv_bf):
    b = mem.shape[0]
    return pl.pallas_call(
        _memkv_kernel,
        out_shape=jax.ShapeDtypeStruct((b, MEM_LEN, 2 * D_MODEL), BF16),
        grid=(b,),
        in_specs=[
            pl.BlockSpec((1, MEM_LEN, D_MODEL), lambda i: (i, 0, 0)),
            pl.BlockSpec((1, D_MODEL), lambda i: (0, 0)),
            pl.BlockSpec((D_MODEL, 2 * D_MODEL), lambda i: (0, 0)),
        ],
        out_specs=pl.BlockSpec((1, MEM_LEN, 2 * D_MODEL), lambda i: (i, 0, 0)),
        compiler_params=pltpu.CompilerParams(
            dimension_semantics=("arbitrary",), vmem_limit_bytes=VMEM_LIMIT),
        name="mem_kv",
    )(mem, g, wkv_bf)


def _xattn_kernel(h_ref, kv_ref, gx_ref, wq_ref, wo_ref, gm_ref, rw_ref, rb_ref,
                  h2_ref, xn2_ref, rid_ref, rwt_ref):
    ts = h_ref.shape[0]
    h = h_ref[...]
    q = _dot(_rms(h, gx_ref[...]).astype(BF16), wq_ref[...])
    qb = q.astype(BF16)
    head_cols = [slice(hd * XA_HEAD_DIM, (hd + 1) * XA_HEAD_DIM) for hd in range(XA_HEADS)]
    scores = [_dot_nt(qb[:, cols], kv_ref[0, :, cols]) * (XA_HEAD_DIM ** -0.5) for cols in head_cols]
    probs = []
    for s in scores:
        e = jnp.exp(s - jnp.max(s, axis=-1, keepdims=True))
        probs.append((e / jnp.sum(e, axis=-1, keepdims=True)).astype(BF16))
    outs = [_dot(p, kv_ref[0, :, D_MODEL + hd * XA_HEAD_DIM:D_MODEL + (hd + 1) * XA_HEAD_DIM])
            for hd, p in enumerate(probs)]
    h2 = h + _dot(jnp.concatenate(outs, axis=1).astype(BF16), wo_ref[...])
    h2_ref[...] = h2
    xn = _rms(h2, gm_ref[...])
    _rows_to_tiles(xn2_ref, xn)

    hi = xn.astype(BF16)
    lo = (xn - hi.astype(F32)).astype(BF16)
    hi_prod = _dot(hi, rw_ref[...])
    lo_prod = _dot(lo, rw_ref[:, 0:LANES])
    lg_t = (hi_prod[:, 0:LANES] + lo_prod) + hi_prod[:, LANES:2 * LANES]
    lg = lg_t.T[0:ROUTE_ROWS] + rb_ref[:, 0:1]

    gl = lg[0:N_GROUPS]
    ge = jnp.exp(gl - jnp.max(gl, axis=0, keepdims=True))
    gp = ge / jnp.sum(ge, axis=0, keepdims=True)
    p_g = gp[0:1]
    g_idx = jnp.zeros((1, ts), jnp.int32)
    for i in range(1, N_GROUPS):
        better = gp[i:i + 1] > p_g
        g_idx = jnp.where(better, i, g_idx)
        p_g = jnp.where(better, gp[i:i + 1], p_g)
    sel = jnp.zeros((EXPERTS_PER_GROUP, ts), F32)
    for i in range(N_GROUPS):
        r0 = ROUTE_E0 + i * EXPERTS_PER_GROUP
        sel = jnp.where(g_idx == i, lg[r0:r0 + EXPERTS_PER_GROUP], sel)
    se = jnp.exp(sel - jnp.max(sel, axis=0, keepdims=True))
    sp = se / jnp.sum(se, axis=0, keepdims=True)
    ridx = lax.broadcasted_iota(jnp.int32, (EXPERTS_PER_GROUP, ts), 0)
    m1 = jnp.max(sp, axis=0, keepdims=True)
    i1 = jnp.min(jnp.where(sp == m1, ridx, EXPERTS_PER_GROUP), axis=0, keepdims=True)
    sp2 = jnp.where(ridx == i1, -1.0, sp)
    m2 = jnp.max(sp2, axis=0, keepdims=True)
    i2 = jnp.min(jnp.where(sp2 == m2, ridx, EXPERTS_PER_GROUP), axis=0, keepdims=True)
    den = m1 + m2
    rid_ref[...] = jnp.zeros_like(rid_ref)
    rwt_ref[...] = jnp.zeros_like(rwt_ref)
    rid_ref[0:1, :] = g_idx * EXPERTS_PER_GROUP + i1
    rid_ref[1:2, :] = g_idx * EXPERTS_PER_GROUP + i2
    rwt_ref[0:1, :] = p_g * m1 / den
    rwt_ref[1:2, :] = p_g * m2 / den


def _xattn(h1, kv, gx, wq_bf, wo_bf, gm, rw_cat, rb, batch, seq):
    ts = XA_TS
    ns = seq // ts
    t = batch * seq
    full = lambda shape: pl.BlockSpec(shape, lambda b, s: (0,) * len(shape))
    return pl.pallas_call(
        _xattn_kernel,
        out_shape=(
            jax.ShapeDtypeStruct((t, D_MODEL), F32),
            jax.ShapeDtypeStruct((t * ROW_TILE, LANES), F32),
            jax.ShapeDtypeStruct((SUBLANES, t), jnp.int32),
            jax.ShapeDtypeStruct((SUBLANES, t), F32),
        ),
        grid=(batch, ns),
        in_specs=[
            pl.BlockSpec((ts, D_MODEL), lambda b, s: (b * ns + s, 0)),
            pl.BlockSpec((1, MEM_LEN, 2 * D_MODEL), lambda b, s: (b, 0, 0)),
            full((1, D_MODEL)),
            full((D_MODEL, D_MODEL)),
            full((D_MODEL, D_MODEL)),
            full((1, D_MODEL)),
            full((D_MODEL, 2 * LANES)),
            full((ROUTE_ROWS, LANES)),
        ],
        out_specs=(
            pl.BlockSpec((ts, D_MODEL), lambda b, s: (b * ns + s, 0)),
            pl.BlockSpec((ts * ROW_TILE, LANES), lambda b, s: (b * ns + s, 0)),
            pl.BlockSpec((SUBLANES, ts), lambda b, s: (0, b * ns + s)),
            pl.BlockSpec((SUBLANES, ts), lambda b, s: (0, b * ns + s)),
        ),
        compiler_params=pltpu.CompilerParams(
            dimension_semantics=("arbitrary", "arbitrary"), vmem_limit_bytes=VMEM_LIMIT),
        name="xattn",
    )(h1, kv, gx, wq_bf, wo_bf, gm, rw_cat, rb)


def _tile_of(ref, row):
    start = row * ROW_TILE
    if not isinstance(start, int):
        start = pl.multiple_of(start, ROW_TILE)
    return ref.at[pl.ds(start, ROW_TILE)]


def _row_copy(src_ref, src_row, dst_ref, dst_row, sem):
    return pltpu.make_async_copy(_tile_of(src_ref, src_row), _tile_of(dst_ref, dst_row), sem)


def _dispatch_kernel(last_ref, has_ref, nu_ref, pos_ref, xn_ref, xs_ref, zero_ref, sem, zsem):
    td = xn_ref.shape[0] // ROW_TILE
    blk = MOE_M * ROW_TILE
    n_blocks = xs_ref.shape[0] // blk

    def fill(start):
        start = pl.multiple_of(start * ROW_TILE, blk)
        return pltpu.make_async_copy(zero_ref, xs_ref.at[pl.ds(start, blk)], zsem)

    @pl.when(pl.program_id(0) == 0)
    def _():
        zero_ref[...] = jnp.zeros_like(zero_ref)

        for e in range(N_EXPERTS):
            @pl.when(has_ref[e] > 0)
            def _():
                fill(last_ref[e]).start()
        for e in range(N_EXPERTS):
            @pl.when(has_ref[e] > 0)
            def _():
                fill(last_ref[e]).wait()

        def start_tail(b, carry):
            fill(b * MOE_M).start()
            return carry

        lax.fori_loop(nu_ref[0], n_blocks, start_tail, 0)

    for r in range(td):
        for k in range(TOP_K):
            _row_copy(xn_ref, r, xs_ref, pos_ref[0, k, r], sem).start(priority=k)
    for k in range(TOP_K):
        pltpu.make_async_copy(xn_ref, xs_ref.at[pl.ds(0, td * ROW_TILE)], sem).wait()

    @pl.when(pl.program_id(0) == pl.num_programs(0) - 1)
    def _():
        def wait_tail(b, carry):
            fill(b * MOE_M).wait()
            return carry

        lax.fori_loop(nu_ref[0], n_blocks, wait_tail, 0)


def _dispatch(last_blk, has, n_used, pos3, xn2, n_rows):
    t = xn2.shape[0] // ROW_TILE
    td = DISP_T
    return pl.pallas_call(
        _dispatch_kernel,
        out_shape=jax.ShapeDtypeStruct((n_rows * ROW_TILE, LANES), F32),
        grid_spec=pltpu.PrefetchScalarGridSpec(
            num_scalar_prefetch=3,
            grid=(t // td,),
            in_specs=[
                pl.BlockSpec((1, TOP_K, td), lambda i, *_: (i, 0, 0), memory_space=pltpu.SMEM),
                pl.BlockSpec((td * ROW_TILE, LANES), lambda i, *_: (i, 0)),
            ],
            out_specs=pl.BlockSpec(memory_space=pl.ANY),
            scratch_shapes=[
                pltpu.VMEM((MOE_M * ROW_TILE, LANES), F32),
                pltpu.SemaphoreType.DMA,
                pltpu.SemaphoreType.DMA,
            ],
        ),
        compiler_params=pltpu.CompilerParams(
            dimension_semantics=("arbitrary",), vmem_limit_bytes=VMEM_LIMIT),
        name="dispatch",
    )(last_blk, has, n_used, pos3, xn2)


def _expert_kernel(be_ref, ne_ref, nu_ref, xs_ref, wg_ref, wu_ref, wd_ref, ys_ref,
                   wgf, wuf, wdf, wgb, wub, wdb, wsem):
    i = pl.program_id(0)
    e = be_ref[i]

    def fetch(expert):
        return [pltpu.make_async_copy(src.at[expert], dst, wsem)
                for src, dst in ((wg_ref, wgf), (wu_ref, wuf), (wd_ref, wdf))]

    @pl.when(i == 0)
    def _():
        for c in fetch(e):
            c.start()

    @pl.when((i == 0) | (e != be_ref[jnp.maximum(i - 1, 0)]))
    def _():
        for c in fetch(e):
            c.wait()
        wgb[...] = wgf[...].astype(BF16)
        wub[...] = wuf[...].astype(BF16)
        wdb[...] = wdf[...].astype(BF16)

        @pl.when(ne_ref[i] != e)
        def _():
            for c in fetch(ne_ref[i]):
                c.start()

    @pl.when(i < nu_ref[0])
    def _():
        x = _tiles_to_rows(xs_ref, MOE_M).astype(BF16)
        g = _dot(x, wgb[...])
        u = _dot(x, wub[...])
        _rows_to_tiles(ys_ref, _dot((g * _sigmoid(g) * u).astype(BF16), wdb[...]))

    @pl.when(i >= nu_ref[0])
    def _():
        ys_ref[...] = jnp.zeros_like(ys_ref)


def _experts(block_e, next_e, n_used, xs, w_gate, w_up, w_down):
    n_rows = xs.shape[0] // ROW_TILE
    row_map = lambda i, be, ne, nu: (jnp.minimum(i, nu[0] - 1), 0)
    out_map = lambda i, be, ne, nu: (i, 0)
    return pl.pallas_call(
        _expert_kernel,
        out_shape=jax.ShapeDtypeStruct((n_rows * ROW_TILE, LANES), F32),
        grid_spec=pltpu.PrefetchScalarGridSpec(
            num_scalar_prefetch=3,
            grid=(n_rows // MOE_M,),
            in_specs=[
                pl.BlockSpec((MOE_M * ROW_TILE, LANES), row_map),
                pl.BlockSpec(memory_space=pl.ANY),
                pl.BlockSpec(memory_space=pl.ANY),
                pl.BlockSpec(memory_space=pl.ANY),
            ],
            out_specs=pl.BlockSpec((MOE_M * ROW_TILE, LANES), out_map),
            scratch_shapes=[
                pltpu.VMEM((D_MODEL, EXPERT_HIDDEN), F32),
                pltpu.VMEM((D_MODEL, EXPERT_HIDDEN), F32),
                pltpu.VMEM((EXPERT_HIDDEN, D_MODEL), F32),
                pltpu.VMEM((D_MODEL, EXPERT_HIDDEN), BF16),
                pltpu.VMEM((D_MODEL, EXPERT_HIDDEN), BF16),
                pltpu.VMEM((EXPERT_HIDDEN, D_MODEL), BF16),
                pltpu.SemaphoreType.DMA,
            ],
        ),
        compiler_params=pltpu.CompilerParams(
            dimension_semantics=("arbitrary",), vmem_limit_bytes=VMEM_LIMIT),
        name="experts",
    )(block_e, next_e, n_used, xs, w_gate, w_up, w_down)


def _combine_kernel(final_norm, pos_ref, nxt_ref, h2_ref, wt_ref, g_ref, ys_ref, o_ref,
                    b00_ref, b01_ref, b10_ref, b11_ref, sem):
    i = pl.program_id(0)
    n_steps = pl.num_programs(0)
    tc = h2_ref.shape[0]
    bufs = ((b00_ref, b01_ref), (b10_ref, b11_ref))

    def issue(rows_ref, s):
        for r in range(tc):
            for k in range(TOP_K):
                _row_copy(ys_ref, rows_ref[0, k, r], bufs[s][k], r, sem.at[s]).start(priority=k)

    @pl.when(i == 0)
    def _():
        issue(pos_ref, 0)

    for s in range(2):
        @pl.when(i % 2 == s)
        def _():
            @pl.when(i + 1 < n_steps)
            def _():
                issue(nxt_ref, 1 - s)

            for k in range(TOP_K):
                pltpu.make_async_copy(ys_ref.at[pl.ds(0, tc * ROW_TILE)], bufs[s][k], sem.at[s]).wait()
            y = (h2_ref[...] + wt_ref[:, 0:1] * _tiles_to_rows(bufs[s][0], tc)
                 + wt_ref[:, 1:2] * _tiles_to_rows(bufs[s][1], tc))
            o_ref[...] = _rms(y, g_ref[...]) if final_norm else y


def _combine(pos3, h2, wt, g, ys, final_norm):
    t = h2.shape[0]
    tc = COMB_T
    nt = t // tc
    return pl.pallas_call(
        functools.partial(_combine_kernel, final_norm),
        out_shape=jax.ShapeDtypeStruct((t, D_MODEL), F32),
        grid=(nt,),
        in_specs=[
            pl.BlockSpec((1, TOP_K, tc), lambda i: (i, 0, 0), memory_space=pltpu.SMEM),
            pl.BlockSpec((1, TOP_K, tc), lambda i: (jnp.minimum(i + 1, nt - 1), 0, 0),
                         memory_space=pltpu.SMEM),
            pl.BlockSpec((tc, D_MODEL), lambda i: (i, 0)),
            pl.BlockSpec((tc, TOP_K), lambda i: (i, 0)),
            pl.BlockSpec((1, D_MODEL), lambda i: (0, 0)),
            pl.BlockSpec(memory_space=pl.ANY),
        ],
        out_specs=pl.BlockSpec((tc, D_MODEL), lambda i: (i, 0)),
        scratch_shapes=[
            pltpu.VMEM((tc * ROW_TILE, LANES), F32),
            pltpu.VMEM((tc * ROW_TILE, LANES), F32),
            pltpu.VMEM((tc * ROW_TILE, LANES), F32),
            pltpu.VMEM((tc * ROW_TILE, LANES), F32),
            pltpu.SemaphoreType.DMA((2,)),
        ],
        compiler_params=pltpu.CompilerParams(
            dimension_semantics=("arbitrary",), vmem_limit_bytes=VMEM_LIMIT),
        name="combine",
    )(pos3, pos3, h2, wt, g, ys)


def _routing_tables(rid, n_tok):
    e_flat = rid[:TOP_K].reshape(-1)
    onehot = (e_flat[None, :] == jnp.arange(N_EXPERTS, dtype=jnp.int32)[:, None]).astype(jnp.int32)
    csum = jnp.cumsum(onehot, axis=1)
    counts = csum[:, -1]
    rank = jnp.sum(onehot * csum, axis=0) - 1
    padded = (counts + MOE_M - 1) // MOE_M * MOE_M
    pend = jnp.cumsum(padded)
    pstart = pend - padded
    pos = jnp.sum(onehot * pstart[:, None], axis=0) + rank
    n_blocks = (TOP_K * n_tok + N_EXPERTS * (MOE_M - 1) + MOE_M - 1) // MOE_M
    n_used = pend[-1] // MOE_M
    blk = jnp.minimum(jnp.arange(n_blocks, dtype=jnp.int32), n_used - 1)
    block_e = jnp.sum((pend[None, :] <= (blk * MOE_M)[:, None]).astype(jnp.int32), axis=1)
    block_e = jnp.minimum(block_e, N_EXPERTS - 1)
    last_blk = jnp.maximum(pend - MOE_M, 0).astype(jnp.int32)
    has = (counts > 0).astype(jnp.int32)
    ids = jnp.arange(N_EXPERTS, dtype=jnp.int32)
    later = (ids[None, :] > ids[:, None]) & (counts[None, :] > 0)
    nxt = jnp.min(jnp.where(later, ids[None, :], N_EXPERTS), axis=1)
    nxt = jnp.where(nxt == N_EXPERTS, ids, nxt)
    next_e = jnp.sum(jnp.where(block_e[:, None] == ids[None, :], nxt[None, :], 0), axis=1)
    return (pos.reshape(TOP_K, n_tok).astype(jnp.int32), block_e.astype(jnp.int32),
            next_e.astype(jnp.int32), n_used.reshape(1).astype(jnp.int32), last_blk, has,
            n_blocks * MOE_M)


def _tile_pos(pos, tile):
    k, t = pos.shape
    return pos.reshape(k, t // tile, tile).transpose(1, 0, 2)


def kernel(x, mem, mix_norm_g, w_in, conv_w, w_conv_out, w_ret_out, w_mix_out, xa_norm_g, mem_norm_g,
           w_xa_q, w_xa_kv, w_xa_o, moe_norm_g, w_group, b_group, w_router, b_router, w_gate, w_up,
           w_down, final_norm_g):
    batch, seq, d = x.shape
    depth = w_in.shape[0]
    t = batch * seq
    h = x.reshape(t, d)
    for l in range(depth):
        proj = _in_proj(h, mix_norm_g[l][None], w_in[l])
        h1 = _mixer(proj, h, conv_w[l], w_conv_out[l].astype(BF16), w_ret_out[l].astype(BF16),
                    w_mix_out[l].astype(BF16), batch, seq)
        kv = _mem_kv(mem, mem_norm_g[l][None], w_xa_kv[l].astype(BF16))

        rw = jnp.zeros((LANES, d), F32)
        rw = rw.at[0:N_GROUPS].set(w_group[l].T).at[ROUTE_E0:ROUTE_E0 + N_EXPERTS].set(w_router[l].T)
        rwh = rw.astype(BF16)
        rwl = (rw - rwh.astype(F32)).astype(BF16)
        rw_cat = jnp.concatenate([rwh.T, rwl.T], axis=1)
        rb = jnp.zeros((ROUTE_ROWS,), F32)
        rb = rb.at[0:N_GROUPS].set(b_group[l]).at[ROUTE_E0:ROUTE_E0 + N_EXPERTS].set(b_router[l])
        rb = jnp.broadcast_to(rb[:, None], (ROUTE_ROWS, LANES))

        h2, xn2, rid, rwt = _xattn(h1, kv, xa_norm_g[l][None], w_xa_q[l].astype(BF16),
                                   w_xa_o[l].astype(BF16), moe_norm_g[l][None], rw_cat, rb, batch, seq)

        pos, block_e, next_e, n_used, last_blk, has, n_rows = _routing_tables(rid, t)
        xs = _dispatch(last_blk, has, n_used, _tile_pos(pos, DISP_T), xn2, n_rows)
        ys = _experts(block_e, next_e, n_used, xs, w_gate[l], w_up[l], w_down[l])
        h = _combine(_tile_pos(pos, COMB_T), h2, rwt[:TOP_K].T, final_norm_g[None], ys,
                     final_norm=(l == depth - 1))
    return h.reshape(batch, seq, d)
```

```python
import functools

import numpy as np
import jax
import jax.numpy as jnp
from jax import lax
from jax.experimental import pallas as pl
from jax.experimental.pallas import tpu as pltpu

F32 = jnp.float32
BF16 = jnp.bfloat16

D_MODEL = 1024
CONV_WIDTH = 512
CONV_K = 3
RET_HEADS = 8
RET_DK = 64
RET_DV = 128
RET_CHUNK = 128
QK_WIDTH = RET_HEADS * RET_DK
V_WIDTH = RET_HEADS * RET_DV
ROPE_BASE = 10000.0
IN_WIDTH = 3 * CONV_WIDTH + 2 * QK_WIDTH + 2 * V_WIDTH + 2 * D_MODEL
OFF_XIN = 0
OFF_BG = OFF_XIN + CONV_WIDTH
OFF_CG = OFF_BG + CONV_WIDTH
OFF_Q = OFF_CG + CONV_WIDTH
OFF_K = OFF_Q + QK_WIDTH
OFF_V = OFF_K + QK_WIDTH
OFF_G = OFF_V + V_WIDTH
OFF_GATE_C = OFF_G + V_WIDTH
OFF_GATE_R = OFF_GATE_C + D_MODEL
MEM_LEN = 256
XA_HEADS = 4
XA_HEAD_DIM = D_MODEL // XA_HEADS
N_GROUPS = 4
EXPERTS_PER_GROUP = 8
N_EXPERTS = N_GROUPS * EXPERTS_PER_GROUP
TOP_K = 2
EXPERT_HIDDEN = D_MODEL // 2
EPS = 1e-6

LANES = 128
SUBLANES = 8
VMEM_LIMIT = 56 * 1024 * 1024

IN_TM = 2048
IN_TN = 1664
MIX_TS = 512
MIX_OUT_ROWS = 256
MIX_STAGE_CHUNKS = 4
XA_TS = 1024
ROUTE_ROWS = 40
ROUTE_E0 = 8
MOE_M = 512
DISP_T = 1024
COMB_T = 512


def _rms(x, g):
    ms = jnp.mean(x * x, axis=-1, keepdims=True)
    return x * lax.rsqrt(ms + EPS) * g


def _sigmoid(x):
    return 1.0 / (1.0 + jnp.exp(-x))


def _dot(a, b):
    return jnp.dot(a, b, preferred_element_type=F32)


def _dot_nt(a, b):
    return lax.dot_general(a, b, (((1,), (1,)), ((), ())), preferred_element_type=F32)


def _dot_tn(a, b):
    return lax.dot_general(a, b, (((0,), (0,)), ((), ())), preferred_element_type=F32)


ROW_TILE = D_MODEL // LANES


def _rows_to_tiles(ref, x):
    n = x.shape[0]
    for c in range(ROW_TILE):
        ref[pl.ds(c, n, stride=ROW_TILE), :] = x[:, c * LANES:(c + 1) * LANES]


def _tiles_to_rows(ref, n):
    return jnp.concatenate([ref[pl.ds(c, n, stride=ROW_TILE), :] for c in range(ROW_TILE)], axis=1)


def _inproj_kernel(x_ref, g_ref, w_ref, o_ref, xn_ref):
    @pl.when(pl.program_id(1) == 0)
    def _():
        xn_ref[...] = _rms(x_ref[...], g_ref[...]).astype(BF16)

    o_ref[...] = _dot(xn_ref[...], w_ref[...].astype(BF16)).astype(BF16)


def _in_proj(x2, g, w_in):
    t = x2.shape[0]
    return pl.pallas_call(
        _inproj_kernel,
        out_shape=jax.ShapeDtypeStruct((t, IN_WIDTH), BF16),
        grid=(t // IN_TM, IN_WIDTH // IN_TN),
        in_specs=[
            pl.BlockSpec((IN_TM, D_MODEL), lambda i, j: (i, 0)),
            pl.BlockSpec((1, D_MODEL), lambda i, j: (0, 0)),
            pl.BlockSpec((D_MODEL, IN_TN), lambda i, j: (0, j)),
        ],
        out_specs=pl.BlockSpec((IN_TM, IN_TN), lambda i, j: (i, j)),
        scratch_shapes=[pltpu.VMEM((IN_TM, D_MODEL), BF16)],
        compiler_params=pltpu.CompilerParams(
            dimension_semantics=("arbitrary", "arbitrary"), vmem_limit_bytes=VMEM_LIMIT),
        name="in_proj",
    )(x2, g, w_in)


def _retention_constants(seq):
    pos = np.arange(seq, dtype=np.float64)
    inv_freq = ROPE_BASE ** (-np.arange(0, RET_DK, 2, dtype=np.float64) / RET_DK)
    ang = pos[:, None] * inv_freq[None, :]
    cos, sin = np.cos(ang), np.sin(ang)
    cos_t = np.concatenate([cos, cos, cos, cos], axis=1)
    sin_t = np.concatenate([-sin, sin, -sin, sin], axis=1)
    log_g = np.log(1.0 - 2.0 ** (-5.0 - np.arange(RET_HEADS, dtype=np.float64)))
    idx = np.arange(RET_CHUNK, dtype=np.float64)
    diff = idx[:, None] - idx[None, :]
    decay = np.where(diff >= 0, np.exp(np.maximum(diff, 0.0)[None] * log_g[:, None, None]), 0.0)
    zeta = np.exp((RET_CHUNK - 1 - idx)[None, :] * log_g[:, None])
    xi = np.exp((idx + 1)[None, :] * log_g[:, None])
    zeta_t = np.repeat(zeta.T, RET_DK, axis=1)
    xi_t = np.repeat(xi.T, RET_DK, axis=1)
    chunk_decay = np.exp(RET_CHUNK * log_g)
    f = lambda a: jnp.asarray(a, dtype=F32)
    return f(cos_t), f(sin_t), f(xi_t), f(zeta_t), f(decay), [float(c) for c in chunk_decay]


def _mixer_kernel(chunk_decay, proj_ref, x_ref, cos_ref, sin_ref, xi_ref, zeta_ref, decay_ref,
                  convw_ref, wc_ref, wr_ref, wm_ref, o_ref, state_ref, tail_ref, yin_ref):
    ts = x_ref.shape[0]

    @pl.when(pl.program_id(1) == 0)
    def _():
        state_ref[...] = jnp.zeros_like(state_ref)
        tail_ref[...] = jnp.zeros_like(tail_ref)

    xin = proj_ref[:, OFF_XIN:OFF_XIN + CONV_WIDTH].astype(F32)
    bg = proj_ref[:, OFF_BG:OFF_BG + CONV_WIDTH].astype(F32)
    cg = proj_ref[:, OFF_CG:OFF_CG + CONV_WIDTH].astype(F32)
    u = cg * xin
    ue = jnp.concatenate([tail_ref[...], u], axis=0)
    u1 = pltpu.roll(ue, 1, 0)[SUBLANES:]
    u2 = pltpu.roll(ue, 2, 0)[SUBLANES:]
    tail_ref[...] = u[ts - SUBLANES:]
    c = convw_ref[2:3, :] * u + convw_ref[1:2, :] * u1 + convw_ref[0:1, :] * u2
    y_conv = _dot((bg * c).astype(BF16), wc_ref[...])

    lane = lax.broadcasted_iota(jnp.int32, (1, LANES), 1)
    low_half = (lane % RET_DK) < (RET_DK // 2)
    head_masks = [(lane // RET_DK) == j for j in range(LANES // RET_DK)]

    def rotary(t, cosv, sinv):
        outs = []
        for p in range(QK_WIDTH // LANES):
            tp = t[:, p * LANES:(p + 1) * LANES]
            fwd = pltpu.roll(tp, LANES - RET_DK // 2, 1)
            bwd = pltpu.roll(tp, RET_DK // 2, 1)
            outs.append(tp * cosv + jnp.where(low_half, fwd, bwd) * sinv)
        return jnp.concatenate(outs, axis=1)

    n_chunks = ts // RET_CHUNK
    chunk_rows = [slice(ci * RET_CHUNK, (ci + 1) * RET_CHUNK) for ci in range(n_chunks)]
    zero = jnp.zeros((), BF16)
    qm, qxm, kb, kz = {}, {}, {}, {}
    for ci, rows in enumerate(chunk_rows):
        cosv = cos_ref[rows, :]
        sinv = sin_ref[rows, :]
        qr = rotary(proj_ref[rows, OFF_Q:OFF_Q + QK_WIDTH].astype(F32), cosv, sinv)
        kr = rotary(proj_ref[rows, OFF_K:OFF_K + QK_WIDTH].astype(F32), cosv, sinv) * (RET_DK ** -0.5)
        qb = qr.astype(BF16)
        qx = (qr * xi_ref[...]).astype(BF16)
        kb[ci] = kr.astype(BF16)
        kz[ci] = (kr * zeta_ref[...]).astype(BF16)
        for h in range(RET_HEADS):
            lanes = slice((h // 2) * LANES, (h // 2 + 1) * LANES)
            qm[ci, h] = jnp.where(head_masks[h % 2], qb[:, lanes], zero)
            qxm[ci, h] = jnp.where(head_masks[h % 2], qx[:, lanes], zero)

    def pair_lanes(h):
        return slice((h // 2) * LANES, (h // 2 + 1) * LANES)

    def v_of(ci, h):
        return proj_ref[chunk_rows[ci], OFF_V + h * RET_DV:OFF_V + (h + 1) * RET_DV]

    state = [state_ref[h] for h in range(RET_HEADS)]
    for c0 in range(0, n_chunks, MIX_STAGE_CHUNKS):
        pairs = [(ci, h) for ci in range(c0, c0 + MIX_STAGE_CHUNKS) for h in range(RET_HEADS)]
        scores = {(ci, h): _dot_nt(qm[ci, h], kb[ci][:, pair_lanes(h)]) for ci, h in pairs}
        kv = {(ci, h): _dot_tn(kz[ci][:, pair_lanes(h)], v_of(ci, h)) for ci, h in pairs}
        probs = {(ci, h): (scores[ci, h] * decay_ref[h]).astype(BF16) for ci, h in pairs}
        inner = {(ci, h): _dot(probs[ci, h], v_of(ci, h)) for ci, h in pairs}
        st_before = {}
        for ci, h in pairs:
            st_before[ci, h] = state[h].astype(BF16)
            state[h] = chunk_decay[h] * state[h] + kv[ci, h]
        o = {(ci, h): inner[ci, h] + _dot(qxm[ci, h], st_before[ci, h]) for ci, h in pairs}
        mu = {p: jnp.mean(o[p], axis=-1, keepdims=True) for p in pairs}
        dlt = {p: o[p] - mu[p] for p in pairs}
        var = {p: jnp.mean(dlt[p] * dlt[p], axis=-1, keepdims=True) for p in pairs}
        for ci, h in pairs:
            g = proj_ref[chunk_rows[ci], OFF_G + h * RET_DV:OFF_G + (h + 1) * RET_DV].astype(F32)
            yin_ref[chunk_rows[ci], h * RET_DV:(h + 1) * RET_DV] = (
                g * _sigmoid(g) * (dlt[ci, h] * lax.rsqrt(var[ci, h] + EPS))).astype(BF16)
    for h in range(RET_HEADS):
        state_ref[h] = state[h]

    for r0 in range(0, ts, MIX_OUT_ROWS):
        rows = slice(r0, r0 + MIX_OUT_ROWS)
        y_ret = _dot(yin_ref[rows, :], wr_ref[...])
        gate_c = proj_ref[rows, OFF_GATE_C:OFF_GATE_C + D_MODEL].astype(F32)
        gate_r = proj_ref[rows, OFF_GATE_R:OFF_GATE_R + D_MODEL].astype(F32)
        merged = _sigmoid(gate_c) * y_conv[rows] + _sigmoid(gate_r) * y_ret
        o_ref[rows, :] = x_ref[rows, :] + _dot(merged.astype(BF16), wm_ref[...])


def _mixer(proj, x2, conv_w, wc_bf, wr_bf, wm_bf, batch, seq):
    ts = MIX_TS
    ns = seq // ts
    cos_t, sin_t, xi_t, zeta_t, decay, chunk_decay = _retention_constants(seq)
    full = lambda shape: pl.BlockSpec(shape, lambda b, s: (0,) * len(shape))
    return pl.pallas_call(
        functools.partial(_mixer_kernel, chunk_decay),
        out_shape=jax.ShapeDtypeStruct((batch * seq, D_MODEL), F32),
        grid=(batch, ns),
        in_specs=[
            pl.BlockSpec((ts, IN_WIDTH), lambda b, s: (b * ns + s, 0)),
            pl.BlockSpec((ts, D_MODEL), lambda b, s: (b * ns + s, 0)),
            pl.BlockSpec((ts, LANES), lambda b, s: (s, 0)),
            pl.BlockSpec((ts, LANES), lambda b, s: (s, 0)),
            full((RET_CHUNK, QK_WIDTH)),
            full((RET_CHUNK, QK_WIDTH)),
            full((RET_HEADS, RET_CHUNK, RET_CHUNK)),
            full((CONV_K, CONV_WIDTH)),
            full((CONV_WIDTH, D_MODEL)),
            full((V_WIDTH, D_MODEL)),
            full((D_MODEL, D_MODEL)),
        ],
        out_specs=pl.BlockSpec((ts, D_MODEL), lambda b, s: (b * ns + s, 0)),
        scratch_shapes=[
            pltpu.VMEM((RET_HEADS, LANES, RET_DV), F32),
            pltpu.VMEM((SUBLANES, CONV_WIDTH), F32),
            pltpu.VMEM((ts, V_WIDTH), BF16),
        ],
        compiler_params=pltpu.CompilerParams(
            dimension_semantics=("arbitrary", "arbitrary"), vmem_limit_bytes=VMEM_LIMIT),
        name="mixer",
    )(proj, x2, cos_t, sin_t, xi_t, zeta_t, decay, conv_w, wc_bf, wr_bf, wm_bf)


def _memkv_kernel(m_ref, g_ref, w_ref, o_ref):
    o_ref[0] = _dot(_rms(m_ref[0], g_ref[...]).astype(BF16), w_ref[...]).astype(BF16)


def _mem_kv(mem, g, wkv_bf):
    b = mem.shape[0]
    return pl.pallas_call(
        _memkv_kernel,
        out_shape=jax.ShapeDtypeStruct((b, MEM_LEN, 2 * D_MODEL), BF16),
        grid=(b,),
        in_specs=[
            pl.BlockSpec((1, MEM_LEN, D_MODEL), lambda i: (i, 0, 0)),
            pl.BlockSpec((1, D_MODEL), lambda i: (0, 0)),
            pl.BlockSpec((D_MODEL, 2 * D_MODEL), lambda i: (0, 0)),
        ],
        out_specs=pl.BlockSpec((1, MEM_LEN, 2 * D_MODEL), lambda i: (i, 0, 0)),
        compiler_params=pltpu.CompilerParams(
            dimension_semantics=("arbitrary",), vmem_limit_bytes=VMEM_LIMIT),
        name="mem_kv",
    )(mem, g, wkv_bf)


def _xattn_kernel(h_ref, kv_ref, gx_ref, wq_ref, wo_ref, gm_ref, rw_ref, rb_ref,
                  h2_ref, xn2_ref, rid_ref, rwt_ref):
    ts = h_ref.shape[0]
    h = h_ref[...]
    q = _dot(_rms(h, gx_ref[...]).astype(BF16), wq_ref[...])
    qb = q.astype(BF16)
    head_cols = [slice(hd * XA_HEAD_DIM, (hd + 1) * XA_HEAD_DIM) for hd in range(XA_HEADS)]
    scores = [_dot_nt(qb[:, cols], kv_ref[0, :, cols]) * (XA_HEAD_DIM ** -0.5) for cols in head_cols]
    probs = []
    for s in scores:
        e = jnp.exp(s - jnp.max(s, axis=-1, keepdims=True))
        probs.append((e / jnp.sum(e, axis=-1, keepdims=True)).astype(BF16))
    outs = [_dot(p, kv_ref[0, :, D_MODEL + hd * XA_HEAD_DIM:D_MODEL + (hd + 1) * XA_HEAD_DIM])
            for hd, p in enumerate(probs)]
    h2 = h + _dot(jnp.concatenate(outs, axis=1).astype(BF16), wo_ref[...])
    h2_ref[...] = h2
    xn = _rms(h2, gm_ref[...])
    _rows_to_tiles(xn2_ref, xn)

    hi = xn.astype(BF16)
    lo = (xn - hi.astype(F32)).astype(BF16)
    hi_prod = _dot(hi, rw_ref[...])
    lo_prod = _dot(lo, rw_ref[:, 0:LANES])
    lg_t = (hi_prod[:, 0:LANES] + lo_prod) + hi_prod[:, LANES:2 * LANES]
    lg = lg_t.T[0:ROUTE_ROWS] + rb_ref[:, 0:1]

    gl = lg[0:N_GROUPS]
    ge = jnp.exp(gl - jnp.max(gl, axis=0, keepdims=True))
    gp = ge / jnp.sum(ge, axis=0, keepdims=True)
    p_g = gp[0:1]
    g_idx = jnp.zeros((1, ts), jnp.int32)
    for i in range(1, N_GROUPS):
        better = gp[i:i + 1] > p_g
        g_idx = jnp.where(better, i, g_idx)
        p_g = jnp.where(better, gp[i:i + 1], p_g)
    sel = jnp.zeros((EXPERTS_PER_GROUP, ts), F32)
    for i in range(N_GROUPS):
        r0 = ROUTE_E0 + i * EXPERTS_PER_GROUP
        sel = jnp.where(g_idx == i, lg[r0:r0 + EXPERTS_PER_GROUP], sel)
    se = jnp.exp(sel - jnp.max(sel, axis=0, keepdims=True))
    sp = se / jnp.sum(se, axis=0, keepdims=True)
    ridx = lax.broadcasted_iota(jnp.int32, (EXPERTS_PER_GROUP, ts), 0)
    m1 = jnp.max(sp, axis=0, keepdims=True)
    i1 = jnp.min(jnp.where(sp == m1, ridx, EXPERTS_PER_GROUP), axis=0, keepdims=True)
    sp2 = jnp.where(ridx == i1, -1.0, sp)
    m2 = jnp.max(sp2, axis=0, keepdims=True)
    i2 = jnp.min(jnp.where(sp2 == m2, ridx, EXPERTS_PER_GROUP), axis=0, keepdims=True)
    den = m1 + m2
    rid_ref[...] = jnp.zeros_like(rid_ref)
    rwt_ref[...] = jnp.zeros_like(rwt_ref)
    rid_ref[0:1, :] = g_idx * EXPERTS_PER_GROUP + i1
    rid_ref[1:2, :] = g_idx * EXPERTS_PER_GROUP + i2
    rwt_ref[0:1, :] = p_g * m1 / den
    rwt_ref[1:2, :] = p_g * m2 / den


def _xattn(h1, kv, gx, wq_bf, wo_bf, gm, rw_cat, rb, batch, seq):
    ts = XA_TS
    ns = seq // ts
    t = batch * seq
    full = lambda shape: pl.BlockSpec(shape, lambda b, s: (0,) * len(shape))
    return pl.pallas_call(
        _xattn_kernel,
        out_shape=(
            jax.ShapeDtypeStruct((t, D_MODEL), F32),
            jax.ShapeDtypeStruct((t * ROW_TILE, LANES), F32),
            jax.ShapeDtypeStruct((SUBLANES, t), jnp.int32),
            jax.ShapeDtypeStruct((SUBLANES, t), F32),
        ),
        grid=(batch, ns),
        in_specs=[
            pl.BlockSpec((ts, D_MODEL), lambda b, s: (b * ns + s, 0)),
            pl.BlockSpec((1, MEM_LEN, 2 * D_MODEL), lambda b, s: (b, 0, 0)),
            full((1, D_MODEL)),
            full((D_MODEL, D_MODEL)),
            full((D_MODEL, D_MODEL)),
            full((1, D_MODEL)),
            full((D_MODEL, 2 * LANES)),
            full((ROUTE_ROWS, LANES)),
        ],
        out_specs=(
            pl.BlockSpec((ts, D_MODEL), lambda b, s: (b * ns + s, 0)),
            pl.BlockSpec((ts * ROW_TILE, LANES), lambda b, s: (b * ns + s, 0)),
            pl.BlockSpec((SUBLANES, ts), lambda b, s: (0, b * ns + s)),
            pl.BlockSpec((SUBLANES, ts), lambda b, s: (0, b * ns + s)),
        ),
        compiler_params=pltpu.CompilerParams(
            dimension_semantics=("arbitrary", "arbitrary"), vmem_limit_bytes=VMEM_LIMIT),
        name="xattn",
    )(h1, kv, gx, wq_bf, wo_bf, gm, rw_cat, rb)


def _tile_of(ref, row):
    start = row * ROW_TILE
    if not isinstance(start, int):
        start = pl.multiple_of(start, ROW_TILE)
    return ref.at[pl.ds(start, ROW_TILE)]


def _row_copy(src_ref, src_row, dst_ref, dst_row, sem):
    return pltpu.make_async_copy(_tile_of(src_ref, src_row), _tile_of(dst_ref, dst_row), sem)


def _dispatch_kernel(last_ref, has_ref, nu_ref, pos_ref, xn_ref, xs_ref, zero_ref, sem, zsem):
    td = xn_ref.shape[0] // ROW_TILE
    blk = MOE_M * ROW_TILE
    n_blocks = xs_ref.shape[0] // blk

    def fill(start):
        start = pl.multiple_of(start * ROW_TILE, blk)
        return pltpu.make_async_copy(zero_ref, xs_ref.at[pl.ds(start, blk)], zsem)

    @pl.when(pl.program_id(0) == 0)
    def _():
        zero_ref[...] = jnp.zeros_like(zero_ref)

        for e in range(N_EXPERTS):
            @pl.when(has_ref[e] > 0)
            def _():
                fill(last_ref[e]).start()
        for e in range(N_EXPERTS):
            @pl.when(has_ref[e] > 0)
            def _():
                fill(last_ref[e]).wait()

        def start_tail(b, carry):
            fill(b * MOE_M).start()
            return carry

        lax.fori_loop(nu_ref[0], n_blocks, start_tail, 0)

    for r in range(td):
        for k in range(TOP_K):
            _row_copy(xn_ref, r, xs_ref, pos_ref[0, k, r], sem).start(priority=k)
    for k in range(TOP_K):
        pltpu.make_async_copy(xn_ref, xs_ref.at[pl.ds(0, td * ROW_TILE)], sem).wait()

    @pl.when(pl.program_id(0) == pl.num_programs(0) - 1)
    def _():
        def wait_tail(b, carry):
            fill(b * MOE_M).wait()
            return carry

        lax.fori_loop(nu_ref[0], n_blocks, wait_tail, 0)


def _dispatch(last_blk, has, n_used, pos3, xn2, n_rows):
    t = xn2.shape[0] // ROW_TILE
    td = DISP_T
    return pl.pallas_call(
        _dispatch_kernel,
        out_shape=jax.ShapeDtypeStruct((n_rows * ROW_TILE, LANES), F32),
        grid_spec=pltpu.PrefetchScalarGridSpec(
            num_scalar_prefetch=3,
            grid=(t // td,),
            in_specs=[
                pl.BlockSpec((1, TOP_K, td), lambda i, *_: (i, 0, 0), memory_space=pltpu.SMEM),
                pl.BlockSpec((td * ROW_TILE, LANES), lambda i, *_: (i, 0)),
            ],
            out_specs=pl.BlockSpec(memory_space=pl.ANY),
            scratch_shapes=[
                pltpu.VMEM((MOE_M * ROW_TILE, LANES), F32),
                pltpu.SemaphoreType.DMA,
                pltpu.SemaphoreType.DMA,
            ],
        ),
        compiler_params=pltpu.CompilerParams(
            dimension_semantics=("arbitrary",), vmem_limit_bytes=VMEM_LIMIT),
        name="dispatch",
    )(last_blk, has, n_used, pos3, xn2)


def _expert_kernel(be_ref, ne_ref, nu_ref, xs_ref, wg_ref, wu_ref, wd_ref, ys_ref,
                   wgf, wuf, wdf, wgb, wub, wdb, wsem):
    i = pl.program_id(0)
    e = be_ref[i]

    def fetch(expert):
        return [pltpu.make_async_copy(src.at[expert], dst, wsem)
                for src, dst in ((wg_ref, wgf), (wu_ref, wuf), (wd_ref, wdf))]

    @pl.when(i == 0)
    def _():
        for c in fetch(e):
            c.start()

    @pl.when((i == 0) | (e != be_ref[jnp.maximum(i - 1, 0)]))
    def _():
        for c in fetch(e):
            c.wait()
        wgb[...] = wgf[...].astype(BF16)
        wub[...] = wuf[...].astype(BF16)
        wdb[...] = wdf[...].astype(BF16)

        @pl.when(ne_ref[i] != e)
        def _():
            for c in fetch(ne_ref[i]):
                c.start()

    @pl.when(i < nu_ref[0])
    def _():
        x = _tiles_to_rows(xs_ref, MOE_M).astype(BF16)
        g = _dot(x, wgb[...])
        u = _dot(x, wub[...])
        _rows_to_tiles(ys_ref, _dot((g * _sigmoid(g) * u).astype(BF16), wdb[...]))

    @pl.when(i >= nu_ref[0])
    def _():
        ys_ref[...] = jnp.zeros_like(ys_ref)


def _experts(block_e, next_e, n_used, xs, w_gate, w_up, w_down):
    n_rows = xs.shape[0] // ROW_TILE
    row_map = lambda i, be, ne, nu: (jnp.minimum(i, nu[0] - 1), 0)
    out_map = lambda i, be, ne, nu: (i, 0)
    return pl.pallas_call(
        _expert_kernel,
        out_shape=jax.ShapeDtypeStruct((n_rows * ROW_TILE, LANES), F32),
        grid_spec=pltpu.PrefetchScalarGridSpec(
            num_scalar_prefetch=3,
            grid=(n_rows // MOE_M,),
            in_specs=[
                pl.BlockSpec((MOE_M * ROW_TILE, LANES), row_map),
                pl.BlockSpec(memory_space=pl.ANY),
                pl.BlockSpec(memory_space=pl.ANY),
                pl.BlockSpec(memory_space=pl.ANY),
            ],
            out_specs=pl.BlockSpec((MOE_M * ROW_TILE, LANES), out_map),
            scratch_shapes=[
                pltpu.VMEM((D_MODEL, EXPERT_HIDDEN), F32),
                pltpu.VMEM((D_MODEL, EXPERT_HIDDEN), F32),
                pltpu.VMEM((EXPERT_HIDDEN, D_MODEL), F32),
                pltpu.VMEM((D_MODEL, EXPERT_HIDDEN), BF16),
                pltpu.VMEM((D_MODEL, EXPERT_HIDDEN), BF16),
                pltpu.VMEM((EXPERT_HIDDEN, D_MODEL), BF16),
                pltpu.SemaphoreType.DMA,
            ],
        ),
        compiler_params=pltpu.CompilerParams(
            dimension_semantics=("arbitrary",), vmem_limit_bytes=VMEM_LIMIT),
        name="experts",
    )(block_e, next_e, n_used, xs, w_gate, w_up, w_down)


def _combine_kernel(final_norm, pos_ref, nxt_ref, h2_ref, wt_ref, g_ref, ys_ref, o_ref,
                    b00_ref, b01_ref, b10_ref, b11_ref, sem):
    i = pl.program_id(0)
    n_steps = pl.num_programs(0)
    tc = h2_ref.shape[0]
    bufs = ((b00_ref, b01_ref), (b10_ref, b11_ref))

    def issue(rows_ref, s):
        for r in range(tc):
            for k in range(TOP_K):
                _row_copy(ys_ref, rows_ref[0, k, r], bufs[s][k], r, sem.at[s]).start(priority=k)

    @pl.when(i == 0)
    def _():
        issue(pos_ref, 0)

    for s in range(2):
        @pl.when(i % 2 == s)
        def _():
            @pl.when(i + 1 < n_steps)
            def _():
                issue(nxt_ref, 1 - s)

            for k in range(TOP_K):
                pltpu.make_async_copy(ys_ref.at[pl.ds(0, tc * ROW_TILE)], bufs[s][k], sem.at[s]).wait()
            y = (h2_ref[...] + wt_ref[:, 0:1] * _tiles_to_rows(bufs[s][0], tc)
                 + wt_ref[:, 1:2] * _tiles_to_rows(bufs[s][1], tc))
            o_ref[...] = _rms(y, g_ref[...]) if final_norm else y


def _combine(pos3, h2, wt, g, ys, final_norm):
    t = h2.shape[0]
    tc = COMB_T
    nt = t // tc
    return pl.pallas_call(
        functools.partial(_combine_kernel, final_norm),
        out_shape=jax.ShapeDtypeStruct((t, D_MODEL), F32),
        grid=(nt,),
        in_specs=[
            pl.BlockSpec((1, TOP_K, tc), lambda i: (i, 0, 0), memory_space=pltpu.SMEM),
            pl.BlockSpec((1, TOP_K, tc), lambda i: (jnp.minimum(i + 1, nt - 1), 0, 0),
                         memory_space=pltpu.SMEM),
            pl.BlockSpec((tc, D_MODEL), lambda i: (i, 0)),
            pl.BlockSpec((tc, TOP_K), lambda i: (i, 0)),
            pl.BlockSpec((1, D_MODEL), lambda i: (0, 0)),
            pl.BlockSpec(memory_space=pl.ANY),
        ],
        out_specs=pl.BlockSpec((tc, D_MODEL), lambda i: (i, 0)),
        scratch_shapes=[
            pltpu.VMEM((tc * ROW_TILE, LANES), F32),
            pltpu.VMEM((tc * ROW_TILE, LANES), F32),
            pltpu.VMEM((tc * ROW_TILE, LANES), F32),
            pltpu.VMEM((tc * ROW_TILE, LANES), F32),
            pltpu.SemaphoreType.DMA((2,)),
        ],
        compiler_params=pltpu.CompilerParams(
            dimension_semantics=("arbitrary",), vmem_limit_bytes=VMEM_LIMIT),
        name="combine",
    )(pos3, pos3, h2, wt, g, ys)


def _routing_tables(rid, n_tok):
    e_flat = rid[:TOP_K].reshape(-1)
    onehot = (e_flat[None, :] == jnp.arange(N_EXPERTS, dtype=jnp.int32)[:, None]).astype(jnp.int32)
    csum = jnp.cumsum(onehot, axis=1)
    counts = csum[:, -1]
    rank = jnp.sum(onehot * csum, axis=0) - 1
    padded = (counts + MOE_M - 1) // MOE_M * MOE_M
    pend = jnp.cumsum(padded)
    pstart = pend - padded
    pos = jnp.sum(onehot * pstart[:, None], axis=0) + rank
    n_blocks = (TOP_K * n_tok + N_EXPERTS * (MOE_M - 1) + MOE_M - 1) // MOE_M
    n_used = pend[-1] // MOE_M
    blk = jnp.minimum(jnp.arange(n_blocks, dtype=jnp.int32), n_used - 1)
    block_e = jnp.sum((pend[None, :] <= (blk * MOE_M)[:, None]).astype(jnp.int32), axis=1)
    block_e = jnp.minimum(block_e, N_EXPERTS - 1)
    last_blk = jnp.maximum(pend - MOE_M, 0).astype(jnp.int32)
    has = (counts > 0).astype(jnp.int32)
    ids = jnp.arange(N_EXPERTS, dtype=jnp.int32)
    later = (ids[None, :] > ids[:, None]) & (counts[None, :] > 0)
    nxt = jnp.min(jnp.where(later, ids[None, :], N_EXPERTS), axis=1)
    nxt = jnp.where(nxt == N_EXPERTS, ids, nxt)
    next_e = jnp.sum(jnp.where(block_e[:, None] == ids[None, :], nxt[None, :], 0), axis=1)
    return (pos.reshape(TOP_K, n_tok).astype(jnp.int32), block_e.astype(jnp.int32),
            next_e.astype(jnp.int32), n_used.reshape(1).astype(jnp.int32), last_blk, has,
            n_blocks * MOE_M)


def _tile_pos(pos, tile):
    k, t = pos.shape
    return pos.reshape(k, t // tile, tile).transpose(1, 0, 2)


def kernel(x, mem, mix_norm_g, w_in, conv_w, w_conv_out, w_ret_out, w_mix_out, xa_norm_g, mem_norm_g,
           w_xa_q, w_xa_kv, w_xa_o, moe_norm_g, w_group, b_group, w_router, b_router, w_gate, w_up,
           w_down, final_norm_g):
    batch, seq, d = x.shape
    depth = w_in.shape[0]
    t = batch * seq
    h = x.reshape(t, d)
    for l in range(depth):
        proj = _in_proj(h, mix_norm_g[l][None], w_in[l])
        h1 = _mixer(proj, h, conv_w[l], w_conv_out[l].astype(BF16), w_ret_out[l].astype(BF16),
                    w_mix_out[l].astype(BF16), batch, seq)
        kv = _mem_kv(mem, mem_norm_g[l][None], w_xa_kv[l].astype(BF16))

        rw = jnp.zeros((LANES, d), F32)
        rw = rw.at[0:N_GROUPS].set(w_group[l].T).at[ROUTE_E0:ROUTE_E0 + N_EXPERTS].set(w_router[l].T)
        rwh = rw.astype(BF16)
        rwl = (rw - rwh.astype(F32)).astype(BF16)
        rw_cat = jnp.concatenate([rwh.T, rwl.T], axis=1)
        rb = jnp.zeros((ROUTE_ROWS,), F32)
        rb = rb.at[0:N_GROUPS].set(b_group[l]).at[ROUTE_E0:ROUTE_E0 + N_EXPERTS].set(b_router[l])
        rb = jnp.broadcast_to(rb[:, None], (ROUTE_ROWS, LANES))

        h2, xn2, rid, rwt = _xattn(h1, kv, xa_norm_g[l][None], w_xa_q[l].astype(BF16),
                                   w_xa_o[l].astype(BF16), moe_norm_g[l][None], rw_cat, rb, batch, seq)

        pos, block_e, next_e, n_used, last_blk, has, n_rows = _routing_tables(rid, t)
        xs = _dispatch(last_blk, has, n_used, _tile_pos(pos, DISP_T), xn2, n_rows)
        ys = _experts(block_e, next_e, n_used, xs, w_gate[l], w_up[l], w_down[l])
        h = _combine(_tile_pos(pos, COMB_T), h2, rwt[:TOP_K].T, final_norm_g[None], ys,
                     final_norm=(l == depth - 1))
    return h.reshape(batch, seq, d)
```

```python
import functools

import numpy as np
import jax
import jax.numpy as jnp
from jax import lax
from jax.experimental import pallas as pl
from jax.experimental.pallas import tpu as pltpu

F32 = jnp.float32
BF16 = jnp.bfloat16

D_MODEL = 1024
CONV_WIDTH = 512
CONV_K = 3
RET_HEADS = 8
RET_DK = 64
RET_DV = 128
RET_CHUNK = 128
QK_WIDTH = RET_HEADS * RET_DK
V_WIDTH = RET_HEADS * RET_DV
ROPE_BASE = 10000.0
IN_WIDTH = 3 * CONV_WIDTH + 2 * QK_WIDTH + 2 * V_WIDTH + 2 * D_MODEL
OFF_XIN = 0
OFF_BG = OFF_XIN + CONV_WIDTH
OFF_CG = OFF_BG + CONV_WIDTH
OFF_Q = OFF_CG + CONV_WIDTH
OFF_K = OFF_Q + QK_WIDTH
OFF_V = OFF_K + QK_WIDTH
OFF_G = OFF_V + V_WIDTH
OFF_GATE_C = OFF_G + V_WIDTH
OFF_GATE_R = OFF_GATE_C + D_MODEL
MEM_LEN = 256
XA_HEADS = 4
XA_HEAD_DIM = D_MODEL // XA_HEADS
N_GROUPS = 4
EXPERTS_PER_GROUP = 8
N_EXPERTS = N_GROUPS * EXPERTS_PER_GROUP
TOP_K = 2
EXPERT_HIDDEN = D_MODEL // 2
EPS = 1e-6

LANES = 128
SUBLANES = 8
VMEM_LIMIT = 56 * 1024 * 1024

IN_TM = 2048
IN_TN = 1664
MIX_TS = 512
MIX_OUT_ROWS = 256
MIX_STAGE_CHUNKS = 4
XA_TS = 1024
ROUTE_ROWS = 40
ROUTE_E0 = 8
MOE_M = 512
DISP_T = 1024
COMB_T = 256
XS_SLOTS = 3


def _rms(x, g):
    ms = jnp.mean(x * x, axis=-1, keepdims=True)
    return x * lax.rsqrt(ms + EPS) * g


def _sigmoid(x):
    return 1.0 / (1.0 + jnp.exp(-x))


def _dot(a, b):
    return jnp.dot(a, b, preferred_element_type=F32)


def _dot_nt(a, b):
    return lax.dot_general(a, b, (((1,), (1,)), ((), ())), preferred_element_type=F32)


def _dot_tn(a, b):
    return lax.dot_general(a, b, (((0,), (0,)), ((), ())), preferred_element_type=F32)


ROW_TILE = D_MODEL // LANES


def _rows_to_tiles(ref, x):
    n = x.shape[0]
    for c in range(ROW_TILE):
        ref[pl.ds(c, n, stride=ROW_TILE), :] = x[:, c * LANES:(c + 1) * LANES]


def _tiles_to_rows(ref, n):
    return jnp.concatenate([ref[pl.ds(c, n, stride=ROW_TILE), :] for c in range(ROW_TILE)], axis=1)


def _inproj_kernel(x_ref, g_ref, w_ref, o_ref, xn_ref):
    @pl.when(pl.program_id(1) == 0)
    def _():
        xn_ref[...] = _rms(x_ref[...], g_ref[...]).astype(BF16)

    o_ref[...] = _dot(xn_ref[...], w_ref[...].astype(BF16)).astype(BF16)


def _in_proj(x2, g, w_in):
    t = x2.shape[0]
    return pl.pallas_call(
        _inproj_kernel,
        out_shape=jax.ShapeDtypeStruct((t, IN_WIDTH), BF16),
        grid=(t // IN_TM, IN_WIDTH // IN_TN),
        in_specs=[
            pl.BlockSpec((IN_TM, D_MODEL), lambda i, j: (i, 0)),
            pl.BlockSpec((1, D_MODEL), lambda i, j: (0, 0)),
            pl.BlockSpec((D_MODEL, IN_TN), lambda i, j: (0, j)),
        ],
        out_specs=pl.BlockSpec((IN_TM, IN_TN), lambda i, j: (i, j)),
        scratch_shapes=[pltpu.VMEM((IN_TM, D_MODEL), BF16)],
        compiler_params=pltpu.CompilerParams(
            dimension_semantics=("arbitrary", "arbitrary"), vmem_limit_bytes=VMEM_LIMIT),
        name="in_proj",
    )(x2, g, w_in)


def _retention_constants(seq):
    pos = np.arange(seq, dtype=np.float64)
    inv_freq = ROPE_BASE ** (-np.arange(0, RET_DK, 2, dtype=np.float64) / RET_DK)
    ang = pos[:, None] * inv_freq[None, :]
    cos, sin = np.cos(ang), np.sin(ang)
    cos_t = np.concatenate([cos, cos, cos, cos], axis=1)
    sin_t = np.concatenate([-sin, sin, -sin, sin], axis=1)
    log_g = np.log(1.0 - 2.0 ** (-5.0 - np.arange(RET_HEADS, dtype=np.float64)))
    idx = np.arange(RET_CHUNK, dtype=np.float64)
    diff = idx[:, None] - idx[None, :]
    decay = np.where(diff >= 0, np.exp(np.maximum(diff, 0.0)[None] * log_g[:, None, None]), 0.0)
    zeta = np.exp((RET_CHUNK - 1 - idx)[None, :] * log_g[:, None])
    xi = np.exp((idx + 1)[None, :] * log_g[:, None])
    zeta_t = np.repeat(zeta.T, RET_DK, axis=1)
    xi_t = np.repeat(xi.T, RET_DK, axis=1)
    chunk_decay = np.exp(RET_CHUNK * log_g)
    f = lambda a: jnp.asarray(a, dtype=F32)
    return f(cos_t), f(sin_t), f(xi_t), f(zeta_t), f(decay), [float(c) for c in chunk_decay]


def _mixer_kernel(chunk_decay, proj_ref, x_ref, cos_ref, sin_ref, xi_ref, zeta_ref, decay_ref,
                  convw_ref, wc_ref, wr_ref, wm_ref, o_ref, state_ref, tail_ref, yin_ref):
    ts = x_ref.shape[0]

    @pl.when(pl.program_id(1) == 0)
    def _():
        state_ref[...] = jnp.zeros_like(state_ref)
        tail_ref[...] = jnp.zeros_like(tail_ref)

    xin = proj_ref[:, OFF_XIN:OFF_XIN + CONV_WIDTH].astype(F32)
    bg = proj_ref[:, OFF_BG:OFF_BG + CONV_WIDTH].astype(F32)
    cg = proj_ref[:, OFF_CG:OFF_CG + CONV_WIDTH].astype(F32)
    u = cg * xin
    ue = jnp.concatenate([tail_ref[...], u], axis=0)
    u1 = pltpu.roll(ue, 1, 0)[SUBLANES:]
    u2 = pltpu.roll(ue, 2, 0)[SUBLANES:]
    tail_ref[...] = u[ts - SUBLANES:]
    c = convw_ref[2:3, :] * u + convw_ref[1:2, :] * u1 + convw_ref[0:1, :] * u2
    y_conv = _dot((bg * c).astype(BF16), wc_ref[...])

    lane = lax.broadcasted_iota(jnp.int32, (1, LANES), 1)
    low_half = (lane % RET_DK) < (RET_DK // 2)
    head_masks = [(lane // RET_DK) == j for j in range(LANES // RET_DK)]

    def rotary(t, cosv, sinv):
        outs = []
        for p in range(QK_WIDTH // LANES):
            tp = t[:, p * LANES:(p + 1) * LANES]
            fwd = pltpu.roll(tp, LANES - RET_DK // 2, 1)
            bwd = pltpu.roll(tp, RET_DK // 2, 1)
            outs.append(tp * cosv + jnp.where(low_half, fwd, bwd) * sinv)
        return jnp.concatenate(outs, axis=1)

    n_chunks = ts // RET_CHUNK
    chunk_rows = [slice(ci * RET_CHUNK, (ci + 1) * RET_CHUNK) for ci in range(n_chunks)]
    zero = jnp.zeros((), BF16)
    qm, qxm, kb, kz = {}, {}, {}, {}
    for ci, rows in enumerate(chunk_rows):
        cosv = cos_ref[rows, :]
        sinv = sin_ref[rows, :]
        qr = rotary(proj_ref[rows, OFF_Q:OFF_Q + QK_WIDTH].astype(F32), cosv, sinv)
        kr = rotary(proj_ref[rows, OFF_K:OFF_K + QK_WIDTH].astype(F32), cosv, sinv) * (RET_DK ** -0.5)
        qb = qr.astype(BF16)
        qx = (qr * xi_ref[...]).astype(BF16)
        kb[ci] = kr.astype(BF16)
        kz[ci] = (kr * zeta_ref[...]).astype(BF16)
        for h in range(RET_HEADS):
            lanes = slice((h // 2) * LANES, (h // 2 + 1) * LANES)
            qm[ci, h] = jnp.where(head_masks[h % 2], qb[:, lanes], zero)
            qxm[ci, h] = jnp.where(head_masks[h % 2], qx[:, lanes], zero)

    def pair_lanes(h):
        return slice((h // 2) * LANES, (h // 2 + 1) * LANES)

    def v_of(ci, h):
        return proj_ref[chunk_rows[ci], OFF_V + h * RET_DV:OFF_V + (h + 1) * RET_DV]

    state = [state_ref[h] for h in range(RET_HEADS)]
    for c0 in range(0, n_chunks, MIX_STAGE_CHUNKS):
        pairs = [(ci, h) for ci in range(c0, c0 + MIX_STAGE_CHUNKS) for h in range(RET_HEADS)]
        scores = {(ci, h): _dot_nt(qm[ci, h], kb[ci][:, pair_lanes(h)]) for ci, h in pairs}
        kv = {(ci, h): _dot_tn(kz[ci][:, pair_lanes(h)], v_of(ci, h)) for ci, h in pairs}
        probs = {(ci, h): (scores[ci, h] * decay_ref[h]).astype(BF16) for ci, h in pairs}
        inner = {(ci, h): _dot(probs[ci, h], v_of(ci, h)) for ci, h in pairs}
        st_before = {}
        for ci, h in pairs:
            st_before[ci, h] = state[h].astype(BF16)
            state[h] = chunk_decay[h] * state[h] + kv[ci, h]
        o = {(ci, h): inner[ci, h] + _dot(qxm[ci, h], st_before[ci, h]) for ci, h in pairs}
        mu = {p: jnp.mean(o[p], axis=-1, keepdims=True) for p in pairs}
        dlt = {p: o[p] - mu[p] for p in pairs}
        var = {p: jnp.mean(dlt[p] * dlt[p], axis=-1, keepdims=True) for p in pairs}
        for ci, h in pairs:
            g = proj_ref[chunk_rows[ci], OFF_G + h * RET_DV:OFF_G + (h + 1) * RET_DV].astype(F32)
            yin_ref[chunk_rows[ci], h * RET_DV:(h + 1) * RET_DV] = (
                g * _sigmoid(g) * (dlt[ci, h] * lax.rsqrt(var[ci, h] + EPS))).astype(BF16)
    for h in range(RET_HEADS):
        state_ref[h] = state[h]

    for r0 in range(0, ts, MIX_OUT_ROWS):
        rows = slice(r0, r0 + MIX_OUT_ROWS)
        y_ret = _dot(yin_ref[rows, :], wr_ref[...])
        gate_c = proj_ref[rows, OFF_GATE_C:OFF_GATE_C + D_MODEL].astype(F32)
        gate_r = proj_ref[rows, OFF_GATE_R:OFF_GATE_R + D_MODEL].astype(F32)
        merged = _sigmoid(gate_c) * y_conv[rows] + _sigmoid(gate_r) * y_ret
        o_ref[rows, :] = x_ref[rows, :] + _dot(merged.astype(BF16), wm_ref[...])


def _mixer(proj, x2, conv_w, wc_bf, wr_bf, wm_bf, batch, seq):
    ts = MIX_TS
    ns = seq // ts
    cos_t, sin_t, xi_t, zeta_t, decay, chunk_decay = _retention_constants(seq)
    full = lambda shape: pl.BlockSpec(shape, lambda b, s: (0,) * len(shape))
    return pl.pallas_call(
        functools.partial(_mixer_kernel, chunk_decay),
        out_shape=jax.ShapeDtypeStruct((batch * seq, D_MODEL), F32),
        grid=(batch, ns),
        in_specs=[
            pl.BlockSpec((ts, IN_WIDTH), lambda b, s: (b * ns + s, 0)),
            pl.BlockSpec((ts, D_MODEL), lambda b, s: (b * ns + s, 0)),
            pl.BlockSpec((ts, LANES), lambda b, s: (s, 0)),
            pl.BlockSpec((ts, LANES), lambda b, s: (s, 0)),
            full((RET_CHUNK, QK_WIDTH)),
            full((RET_CHUNK, QK_WIDTH)),
            full((RET_HEADS, RET_CHUNK, RET_CHUNK)),
            full((CONV_K, CONV_WIDTH)),
            full((CONV_WIDTH, D_MODEL)),
            full((V_WIDTH, D_MODEL)),
            full((D_MODEL, D_MODEL)),
        ],
        out_specs=pl.BlockSpec((ts, D_MODEL), lambda b, s: (b * ns + s, 0)),
        scratch_shapes=[
            pltpu.VMEM((RET_HEADS, LANES, RET_DV), F32),
            pltpu.VMEM((SUBLANES, CONV_WIDTH), F32),
            pltpu.VMEM((ts, V_WIDTH), BF16),
        ],
        compiler_params=pltpu.CompilerParams(
            dimension_semantics=("arbitrary", "arbitrary"), vmem_limit_bytes=VMEM_LIMIT),
        name="mixer",
    )(proj, x2, cos_t, sin_t, xi_t, zeta_t, decay, conv_w, wc_bf, wr_bf, wm_bf)


def _memkv_kernel(m_ref, g_ref, w_ref, o_ref):
    o_ref[0] = _dot(_rms(m_ref[0], g_ref[...]).astype(BF16), w_ref[...]).astype(BF16)


def _mem_kv(mem, g, wkv_bf):
    b = mem.shape[0]
    return pl.pallas_call(
        _memkv_kernel,
        out_shape=jax.ShapeDtypeStruct((b, MEM_LEN, 2 * D_MODEL), BF16),
        grid=(b,),
        in_specs=[
            pl.BlockSpec((1, MEM_LEN, D_MODEL), lambda i: (i, 0, 0)),
            pl.BlockSpec((1, D_MODEL), lambda i: (0, 0)),
            pl.BlockSpec((D_MODEL, 2 * D_MODEL), lambda i: (0, 0)),
        ],
        out_specs=pl.BlockSpec((1, MEM_LEN, 2 * D_MODEL), lambda i: (i, 0, 0)),
        compiler_params=pltpu.CompilerParams(
            dimension_semantics=("arbitrary",), vmem_limit_bytes=VMEM_LIMIT),
        name="mem_kv",
    )(mem, g, wkv_bf)


def _xattn_kernel(h_ref, kv_ref, gx_ref, wq_ref, wo_ref, gm_ref, rw_ref, rb_ref,
                  h2_ref, xn2_ref, rid_ref, rwt_ref):
    ts = h_ref.shape[0]
    h = h_ref[...]
    q = _dot(_rms(h, gx_ref[...]).astype(BF16), wq_ref[...])
    qb = q.astype(BF16)
    head_cols = [slice(hd * XA_HEAD_DIM, (hd + 1) * XA_HEAD_DIM) for hd in range(XA_HEADS)]
    scores = [_dot_nt(qb[:, cols], kv_ref[0, :, cols]) * (XA_HEAD_DIM ** -0.5) for cols in head_cols]
    probs = []
    for s in scores:
        e = jnp.exp(s - jnp.max(s, axis=-1, keepdims=True))
        probs.append((e / jnp.sum(e, axis=-1, keepdims=True)).astype(BF16))
    outs = [_dot(p, kv_ref[0, :, D_MODEL + hd * XA_HEAD_DIM:D_MODEL + (hd + 1) * XA_HEAD_DIM])
            for hd, p in enumerate(probs)]
    h2 = h + _dot(jnp.concatenate(outs, axis=1).astype(BF16), wo_ref[...])
    h2_ref[...] = h2
    xn = _rms(h2, gm_ref[...])
    _rows_to_tiles(xn2_ref, xn)

    hi = xn.astype(BF16)
    lo = (xn - hi.astype(F32)).astype(BF16)
    hi_prod = _dot(hi, rw_ref[...])
    lo_prod = _dot(lo, rw_ref[:, 0:LANES])
    lg_t = (hi_prod[:, 0:LANES] + lo_prod) + hi_prod[:, LANES:2 * LANES]
    lg = lg_t.T[0:ROUTE_ROWS] + rb_ref[:, 0:1]

    gl = lg[0:N_GROUPS]
    ge = jnp.exp(gl - jnp.max(gl, axis=0, keepdims=True))
    gp = ge / jnp.sum(ge, axis=0, keepdims=True)
    p_g = gp[0:1]
    g_idx = jnp.zeros((1, ts), jnp.int32)
    for i in range(1, N_GROUPS):
        better = gp[i:i + 1] > p_g
        g_idx = jnp.where(better, i, g_idx)
        p_g = jnp.where(better, gp[i:i + 1], p_g)
    sel = jnp.zeros((EXPERTS_PER_GROUP, ts), F32)
    for i in range(N_GROUPS):
        r0 = ROUTE_E0 + i * EXPERTS_PER_GROUP
        sel = jnp.where(g_idx == i, lg[r0:r0 + EXPERTS_PER_GROUP], sel)
    se = jnp.exp(sel - jnp.max(sel, axis=0, keepdims=True))
    sp = se / jnp.sum(se, axis=0, keepdims=True)
    ridx = lax.broadcasted_iota(jnp.int32, (EXPERTS_PER_GROUP, ts), 0)
    m1 = jnp.max(sp, axis=0, keepdims=True)
    i1 = jnp.min(jnp.where(sp == m1, ridx, EXPERTS_PER_GROUP), axis=0, keepdims=True)
    sp2 = jnp.where(ridx == i1, -1.0, sp)
    m2 = jnp.max(sp2, axis=0, keepdims=True)
    i2 = jnp.min(jnp.where(sp2 == m2, ridx, EXPERTS_PER_GROUP), axis=0, keepdims=True)
    den = m1 + m2
    rid_ref[...] = jnp.zeros_like(rid_ref)
    rwt_ref[...] = jnp.zeros_like(rwt_ref)
    rid_ref[0:1, :] = g_idx * EXPERTS_PER_GROUP + i1
    rid_ref[1:2, :] = g_idx * EXPERTS_PER_GROUP + i2
    rwt_ref[0:1, :] = p_g * m1 / den
    rwt_ref[1:2, :] = p_g * m2 / den


def _xattn(h1, kv, gx, wq_bf, wo_bf, gm, rw_cat, rb, batch, seq):
    ts = XA_TS
    ns = seq // ts
    t = batch * seq
    full = lambda shape: pl.BlockSpec(shape, lambda b, s: (0,) * len(shape))
    return pl.pallas_call(
        _xattn_kernel,
        out_shape=(
            jax.ShapeDtypeStruct((t, D_MODEL), F32),
            jax.ShapeDtypeStruct((t * ROW_TILE, LANES), F32),
            jax.ShapeDtypeStruct((SUBLANES, t), jnp.int32),
            jax.ShapeDtypeStruct((SUBLANES, t), F32),
        ),
        grid=(batch, ns),
        in_specs=[
            pl.BlockSpec((ts, D_MODEL), lambda b, s: (b * ns + s, 0)),
            pl.BlockSpec((1, MEM_LEN, 2 * D_MODEL), lambda b, s: (b, 0, 0)),
            full((1, D_MODEL)),
            full((D_MODEL, D_MODEL)),
            full((D_MODEL, D_MODEL)),
            full((1, D_MODEL)),
            full((D_MODEL, 2 * LANES)),
            full((ROUTE_ROWS, LANES)),
        ],
        out_specs=(
            pl.BlockSpec((ts, D_MODEL), lambda b, s: (b * ns + s, 0)),
            pl.BlockSpec((ts * ROW_TILE, LANES), lambda b, s: (b * ns + s, 0)),
            pl.BlockSpec((SUBLANES, ts), lambda b, s: (0, b * ns + s)),
            pl.BlockSpec((SUBLANES, ts), lambda b, s: (0, b * ns + s)),
        ),
        compiler_params=pltpu.CompilerParams(
            dimension_semantics=("arbitrary", "arbitrary"), vmem_limit_bytes=VMEM_LIMIT),
        name="xattn",
    )(h1, kv, gx, wq_bf, wo_bf, gm, rw_cat, rb)


def _tile_of(ref, row):
    start = row * ROW_TILE
    if not isinstance(start, int):
        start = pl.multiple_of(start, ROW_TILE)
    return ref.at[pl.ds(start, ROW_TILE)]


def _row_copy(src_ref, src_row, dst_ref, dst_row, sem):
    return pltpu.make_async_copy(_tile_of(src_ref, src_row), _tile_of(dst_ref, dst_row), sem)


def _dispatch_kernel(last_ref, has_ref, nu_ref, pos_ref, xn_ref, xs_ref, zero_ref, sem, zsem):
    td = xn_ref.shape[0] // ROW_TILE
    blk = MOE_M * ROW_TILE
    n_blocks = xs_ref.shape[0] // blk

    def fill(start):
        start = pl.multiple_of(start * ROW_TILE, blk)
        return pltpu.make_async_copy(zero_ref, xs_ref.at[pl.ds(start, blk)], zsem)

    @pl.when(pl.program_id(0) == 0)
    def _():
        zero_ref[...] = jnp.zeros_like(zero_ref)

        for e in range(N_EXPERTS):
            @pl.when(has_ref[e] > 0)
            def _():
                fill(last_ref[e]).start()
        for e in range(N_EXPERTS):
            @pl.when(has_ref[e] > 0)
            def _():
                fill(last_ref[e]).wait()

        def start_tail(b, carry):
            fill(b * MOE_M).start()
            return carry

        lax.fori_loop(nu_ref[0], n_blocks, start_tail, 0)

    for r in range(td):
        for k in range(TOP_K):
            _row_copy(xn_ref, r, xs_ref, pos_ref[0, k, r], sem).start(priority=k)
    for k in range(TOP_K):
        pltpu.make_async_copy(xn_ref, xs_ref.at[pl.ds(0, td * ROW_TILE)], sem).wait()

    @pl.when(pl.program_id(0) == pl.num_programs(0) - 1)
    def _():
        def wait_tail(b, carry):
            fill(b * MOE_M).wait()
            return carry

        lax.fori_loop(nu_ref[0], n_blocks, wait_tail, 0)


def _dispatch(last_blk, has, n_used, pos3, xn2, n_rows):
    t = xn2.shape[0] // ROW_TILE
    td = DISP_T
    return pl.pallas_call(
        _dispatch_kernel,
        out_shape=jax.ShapeDtypeStruct((n_rows * ROW_TILE, LANES), F32),
        grid_spec=pltpu.PrefetchScalarGridSpec(
            num_scalar_prefetch=3,
            grid=(t // td,),
            in_specs=[
                pl.BlockSpec((1, TOP_K, td), lambda i, *_: (i, 0, 0), memory_space=pltpu.SMEM),
                pl.BlockSpec((td * ROW_TILE, LANES), lambda i, *_: (i, 0)),
            ],
            out_specs=pl.BlockSpec(memory_space=pl.ANY),
            scratch_shapes=[
                pltpu.VMEM((MOE_M * ROW_TILE, LANES), F32),
                pltpu.SemaphoreType.DMA,
                pltpu.SemaphoreType.DMA,
            ],
        ),
        compiler_params=pltpu.CompilerParams(
            dimension_semantics=("arbitrary",), vmem_limit_bytes=VMEM_LIMIT),
        name="dispatch",
    )(last_blk, has, n_used, pos3, xn2)


def _expert_kernel(be_ref, ne_ref, nu_ref, xs_ref, wg_ref, wu_ref, wd_ref, ys_ref,
                   wgf, wuf, wdf, wgb, wub, wdb, wsem, xb0, xb1, xb2, xsem):
    xbufs = (xb0, xb1, xb2)
    assert len(xbufs) == XS_SLOTS
    i = pl.program_id(0)
    e = be_ref[i]

    def fetch(expert):
        return [pltpu.make_async_copy(src.at[expert], dst, wsem)
                for src, dst in ((wg_ref, wgf), (wu_ref, wuf), (wd_ref, wdf))]

    @pl.when(i == 0)
    def _():
        for c in fetch(e):
            c.start()

    @pl.when((i == 0) | (e != be_ref[jnp.maximum(i - 1, 0)]))
    def _():
        for c in fetch(e):
            c.wait()
        wgb[...] = wgf[...].astype(BF16)
        wub[...] = wuf[...].astype(BF16)
        wdb[...] = wdf[...].astype(BF16)

        @pl.when(ne_ref[i] != e)
        def _():
            for c in fetch(ne_ref[i]):
                c.start()

    nu = nu_ref[0]
    blk = MOE_M * ROW_TILE

    def rows_in(b, s):
        start = b * blk if isinstance(b, int) else pl.multiple_of(b * blk, blk)
        return pltpu.make_async_copy(xs_ref.at[pl.ds(start, blk)], xbufs[s], xsem.at[s])

    @pl.when(i == 0)
    def _():
        for b in range(XS_SLOTS - 1):
            @pl.when(b < nu)
            def _():
                rows_in(b, b).start()

    for s in range(XS_SLOTS):
        @pl.when((i < nu) & (lax.rem(i, XS_SLOTS) == s))
        def _():
            @pl.when(i + (XS_SLOTS - 1) < nu)
            def _():
                rows_in(i + (XS_SLOTS - 1), (s + XS_SLOTS - 1) % XS_SLOTS).start()

            rows_in(i, s).wait()
            x = _tiles_to_rows(xbufs[s], MOE_M).astype(BF16)
            g = _dot(x, wgb[...])
            u = _dot(x, wub[...])
            _rows_to_tiles(ys_ref, _dot((g * _sigmoid(g) * u).astype(BF16), wdb[...]))

    @pl.when(i >= nu)
    def _():
        ys_ref[...] = jnp.zeros_like(ys_ref)


def _experts(block_e, next_e, n_used, xs, w_gate, w_up, w_down):
    n_rows = xs.shape[0] // ROW_TILE
    out_map = lambda i, be, ne, nu: (i, 0)
    return pl.pallas_call(
        _expert_kernel,
        out_shape=jax.ShapeDtypeStruct((n_rows * ROW_TILE, LANES), F32),
        grid_spec=pltpu.PrefetchScalarGridSpec(
            num_scalar_prefetch=3,
            grid=(n_rows // MOE_M,),
            in_specs=[
                pl.BlockSpec(memory_space=pl.ANY),
                pl.BlockSpec(memory_space=pl.ANY),
                pl.BlockSpec(memory_space=pl.ANY),
                pl.BlockSpec(memory_space=pl.ANY),
            ],
            out_specs=pl.BlockSpec((MOE_M * ROW_TILE, LANES), out_map),
            scratch_shapes=[
                pltpu.VMEM((D_MODEL, EXPERT_HIDDEN), F32),
                pltpu.VMEM((D_MODEL, EXPERT_HIDDEN), F32),
                pltpu.VMEM((EXPERT_HIDDEN, D_MODEL), F32),
                pltpu.VMEM((D_MODEL, EXPERT_HIDDEN), BF16),
                pltpu.VMEM((D_MODEL, EXPERT_HIDDEN), BF16),
                pltpu.VMEM((EXPERT_HIDDEN, D_MODEL), BF16),
                pltpu.SemaphoreType.DMA,
                pltpu.VMEM((MOE_M * ROW_TILE, LANES), F32),
                pltpu.VMEM((MOE_M * ROW_TILE, LANES), F32),
                pltpu.VMEM((MOE_M * ROW_TILE, LANES), F32),
                pltpu.SemaphoreType.DMA((XS_SLOTS,)),
            ],
        ),
        compiler_params=pltpu.CompilerParams(
            dimension_semantics=("arbitrary",), vmem_limit_bytes=VMEM_LIMIT),
        name="experts",
    )(block_e, next_e, n_used, xs, w_gate, w_up, w_down)


def _combine_kernel(final_norm, pos_ref, nxt_ref, h2_ref, wt_ref, g_ref, ys_ref, o_ref,
                    b00_ref, b01_ref, b10_ref, b11_ref, sem):
    i = pl.program_id(0)
    n_steps = pl.num_programs(0)
    tc = h2_ref.shape[0]
    bufs = ((b00_ref, b01_ref), (b10_ref, b11_ref))

    def issue(rows_ref, s):
        for r in range(tc):
            for k in range(TOP_K):
                _row_copy(ys_ref, rows_ref[0, k, r], bufs[s][k], r, sem.at[s]).start(priority=k)

    @pl.when(i == 0)
    def _():
        issue(pos_ref, 0)

    for s in range(2):
        @pl.when(i % 2 == s)
        def _():
            @pl.when(i + 1 < n_steps)
            def _():
                issue(nxt_ref, 1 - s)

            for k in range(TOP_K):
                pltpu.make_async_copy(ys_ref.at[pl.ds(0, tc * ROW_TILE)], bufs[s][k], sem.at[s]).wait()
            y = (h2_ref[...] + wt_ref[:, 0:1] * _tiles_to_rows(bufs[s][0], tc)
                 + wt_ref[:, 1:2] * _tiles_to_rows(bufs[s][1], tc))
            o_ref[...] = _rms(y, g_ref[...]) if final_norm else y


def _combine(pos3, h2, wt, g, ys, final_norm):
    t = h2.shape[0]
    tc = COMB_T
    nt = t // tc
    return pl.pallas_call(
        functools.partial(_combine_kernel, final_norm),
        out_shape=jax.ShapeDtypeStruct((t, D_MODEL), F32),
        grid=(nt,),
        in_specs=[
            pl.BlockSpec((1, TOP_K, tc), lambda i: (i, 0, 0), memory_space=pltpu.SMEM),
            pl.BlockSpec((1, TOP_K, tc), lambda i: (jnp.minimum(i + 1, nt - 1), 0, 0),
                         memory_space=pltpu.SMEM),
            pl.BlockSpec((tc, D_MODEL), lambda i: (i, 0)),
            pl.BlockSpec((tc, TOP_K), lambda i: (i, 0)),
            pl.BlockSpec((1, D_MODEL), lambda i: (0, 0)),
            pl.BlockSpec(memory_space=pl.ANY),
        ],
        out_specs=pl.BlockSpec((tc, D_MODEL), lambda i: (i, 0)),
        scratch_shapes=[
            pltpu.VMEM((tc * ROW_TILE, LANES), F32),
            pltpu.VMEM((tc * ROW_TILE, LANES), F32),
            pltpu.VMEM((tc * ROW_TILE, LANES), F32),
            pltpu.VMEM((tc * ROW_TILE, LANES), F32),
            pltpu.SemaphoreType.DMA((2,)),
        ],
        compiler_params=pltpu.CompilerParams(
            dimension_semantics=("arbitrary",), vmem_limit_bytes=VMEM_LIMIT),
        name="combine",
    )(pos3, pos3, h2, wt, g, ys)


def _routing_tables(rid, n_tok):
    e_flat = rid[:TOP_K].reshape(-1)
    onehot = (e_flat[None, :] == jnp.arange(N_EXPERTS, dtype=jnp.int32)[:, None]).astype(jnp.int32)
    csum = jnp.cumsum(onehot, axis=1)
    counts = csum[:, -1]
    rank = jnp.sum(onehot * csum, axis=0) - 1
    padded = (counts + MOE_M - 1) // MOE_M * MOE_M
    pend = jnp.cumsum(padded)
    pstart = pend - padded
    pos = jnp.sum(onehot * pstart[:, None], axis=0) + rank
    n_blocks = (TOP_K * n_tok + N_EXPERTS * (MOE_M - 1) + MOE_M - 1) // MOE_M
    n_used = pend[-1] // MOE_M
    blk = jnp.minimum(jnp.arange(n_blocks, dtype=jnp.int32), n_used - 1)
    block_e = jnp.sum((pend[None, :] <= (blk * MOE_M)[:, None]).astype(jnp.int32), axis=1)
    block_e = jnp.minimum(block_e, N_EXPERTS - 1)
    last_blk = jnp.maximum(pend - MOE_M, 0).astype(jnp.int32)
    has = (counts > 0).astype(jnp.int32)
    ids = jnp.arange(N_EXPERTS, dtype=jnp.int32)
    later = (ids[None, :] > ids[:, None]) & (counts[None, :] > 0)
    nxt = jnp.min(jnp.where(later, ids[None, :], N_EXPERTS), axis=1)
    nxt = jnp.where(nxt == N_EXPERTS, ids, nxt)
    next_e = jnp.sum(jnp.where(block_e[:, None] == ids[None, :], nxt[None, :], 0), axis=1)
    return (pos.reshape(TOP_K, n_tok).astype(jnp.int32), block_e.astype(jnp.int32),
            next_e.astype(jnp.int32), n_used.reshape(1).astype(jnp.int32), last_blk, has,
            n_blocks * MOE_M)


def _tile_pos(pos, tile):
    k, t = pos.shape
    return pos.reshape(k, t // tile, tile).transpose(1, 0, 2)


def kernel(x, mem, mix_norm_g, w_in, conv_w, w_conv_out, w_ret_out, w_mix_out, xa_norm_g, mem_norm_g,
           w_xa_q, w_xa_kv, w_xa_o, moe_norm_g, w_group, b_group, w_router, b_router, w_gate, w_up,
           w_down, final_norm_g):
    batch, seq, d = x.shape
    depth = w_in.shape[0]
    t = batch * seq
    h = x.reshape(t, d)
    for l in range(depth):
        proj = _in_proj(h, mix_norm_g[l][None], w_in[l])
        h1 = _mixer(proj, h, conv_w[l], w_conv_out[l].astype(BF16), w_ret_out[l].astype(BF16),
                    w_mix_out[l].astype(BF16), batch, seq)
        kv = _mem_kv(mem, mem_norm_g[l][None], w_xa_kv[l].astype(BF16))

        rw = jnp.zeros((LANES, d), F32)
        rw = rw.at[0:N_GROUPS].set(w_group[l].T).at[ROUTE_E0:ROUTE_E0 + N_EXPERTS].set(w_router[l].T)
        rwh = rw.astype(BF16)
        rwl = (rw - rwh.astype(F32)).astype(BF16)
        rw_cat = jnp.concatenate([rwh.T, rwl.T], axis=1)
        rb = jnp.zeros((ROUTE_ROWS,), F32)
        rb = rb.at[0:N_GROUPS].set(b_group[l]).at[ROUTE_E0:ROUTE_E0 + N_EXPERTS].set(b_router[l])
        rb = jnp.broadcast_to(rb[:, None], (ROUTE_ROWS, LANES))

        h2, xn2, rid, rwt = _xattn(h1, kv, xa_norm_g[l][None], w_xa_q[l].astype(BF16),
                                   w_xa_o[l].astype(BF16), moe_norm_g[l][None], rw_cat, rb, batch, seq)

        pos, block_e, next_e, n_used, last_blk, has, n_rows = _routing_tables(rid, t)
        xs = _dispatch(last_blk, has, n_used, _tile_pos(pos, DISP_T), xn2, n_rows)
        ys = _experts(block_e, next_e, n_used, xs, w_gate[l], w_up[l], w_down[l])
        h = _combine(_tile_pos(pos, COMB_T), h2, rwt[:TOP_K].T, final_norm_g[None], ys,
                     final_norm=(l == depth - 1))
    return h.reshape(batch, seq, d)
```

```python
import functools

import numpy as np
import jax
import jax.numpy as jnp
from jax import lax
from jax.experimental import pallas as pl
from jax.experimental.pallas import tpu as pltpu

F32 = jnp.float32
BF16 = jnp.bfloat16

D_MODEL = 1024
CONV_WIDTH = 512
CONV_K = 3
RET_HEADS = 8
RET_DK = 64
RET_DV = 128
RET_CHUNK = 128
QK_WIDTH = RET_HEADS * RET_DK
V_WIDTH = RET_HEADS * RET_DV
ROPE_BASE = 10000.0
IN_WIDTH = 3 * CONV_WIDTH + 2 * QK_WIDTH + 2 * V_WIDTH + 2 * D_MODEL
OFF_XIN = 0
OFF_BG = OFF_XIN + CONV_WIDTH
OFF_CG = OFF_BG + CONV_WIDTH
OFF_Q = OFF_CG + CONV_WIDTH
OFF_K = OFF_Q + QK_WIDTH
OFF_V = OFF_K + QK_WIDTH
OFF_G = OFF_V + V_WIDTH
OFF_GATE_C = OFF_G + V_WIDTH
OFF_GATE_R = OFF_GATE_C + D_MODEL
MEM_LEN = 256
XA_HEADS = 4
XA_HEAD_DIM = D_MODEL // XA_HEADS
N_GROUPS = 4
EXPERTS_PER_GROUP = 8
N_EXPERTS = N_GROUPS * EXPERTS_PER_GROUP
TOP_K = 2
EXPERT_HIDDEN = D_MODEL // 2
EPS = 1e-6

LANES = 128
SUBLANES = 8
VMEM_LIMIT = 56 * 1024 * 1024

IN_TM = 2048
IN_TN = 1664
MIX_TS = 512
MIX_OUT_ROWS = 256
MIX_STAGE_CHUNKS = 4
XA_TS = 1024
ROUTE_ROWS = 40
ROUTE_E0 = 8
MOE_M = 512
DISP_T = 1024
COMB_T = 256
XS_SLOTS = 3


def _rms(x, g):
    ms = jnp.mean(x * x, axis=-1, keepdims=True)
    return x * lax.rsqrt(ms + EPS) * g


def _sigmoid(x):
    return 1.0 / (1.0 + jnp.exp(-x))


def _dot(a, b):
    return jnp.dot(a, b, preferred_element_type=F32)


def _dot_nt(a, b):
    return lax.dot_general(a, b, (((1,), (1,)), ((), ())), preferred_element_type=F32)


def _dot_tn(a, b):
    return lax.dot_general(a, b, (((0,), (0,)), ((), ())), preferred_element_type=F32)


ROW_TILE = D_MODEL // LANES


def _rows_to_tiles(ref, x):
    n = x.shape[0]
    for c in range(ROW_TILE):
        ref[pl.ds(c, n, stride=ROW_TILE), :] = x[:, c * LANES:(c + 1) * LANES]


def _tiles_to_rows(ref, n):
    return jnp.concatenate([ref[pl.ds(c, n, stride=ROW_TILE), :] for c in range(ROW_TILE)], axis=1)


def _inproj_kernel(x_ref, g_ref, w_ref, o_ref, xn_ref):
    @pl.when(pl.program_id(1) == 0)
    def _():
        xn_ref[...] = _rms(x_ref[...], g_ref[...]).astype(BF16)

    o_ref[...] = _dot(xn_ref[...], w_ref[...].astype(BF16)).astype(BF16)


def _in_proj(x2, g, w_in):
    t = x2.shape[0]
    return pl.pallas_call(
        _inproj_kernel,
        out_shape=jax.ShapeDtypeStruct((t, IN_WIDTH), BF16),
        grid=(t // IN_TM, IN_WIDTH // IN_TN),
        in_specs=[
            pl.BlockSpec((IN_TM, D_MODEL), lambda i, j: (i, 0)),
            pl.BlockSpec((1, D_MODEL), lambda i, j: (0, 0)),
            pl.BlockSpec((D_MODEL, IN_TN), lambda i, j: (0, j)),
        ],
        out_specs=pl.BlockSpec((IN_TM, IN_TN), lambda i, j: (i, j)),
        scratch_shapes=[pltpu.VMEM((IN_TM, D_MODEL), BF16)],
        compiler_params=pltpu.CompilerParams(
            dimension_semantics=("arbitrary", "arbitrary"), vmem_limit_bytes=VMEM_LIMIT),
        name="in_proj",
    )(x2, g, w_in)


def _retention_constants(seq):
    pos = np.arange(seq, dtype=np.float64)
    inv_freq = ROPE_BASE ** (-np.arange(0, RET_DK, 2, dtype=np.float64) / RET_DK)
    ang = pos[:, None] * inv_freq[None, :]
    cos, sin = np.cos(ang), np.sin(ang)
    cos_t = np.concatenate([cos, cos, cos, cos], axis=1)
    sin_t = np.concatenate([-sin, sin, -sin, sin], axis=1)
    log_g = np.log(1.0 - 2.0 ** (-5.0 - np.arange(RET_HEADS, dtype=np.float64)))
    idx = np.arange(RET_CHUNK, dtype=np.float64)
    diff = idx[:, None] - idx[None, :]
    decay = np.where(diff >= 0, np.exp(np.maximum(diff, 0.0)[None] * log_g[:, None, None]), 0.0)
    zeta = np.exp((RET_CHUNK - 1 - idx)[None, :] * log_g[:, None])
    xi = np.exp((idx + 1)[None, :] * log_g[:, None])
    zeta_t = np.repeat(zeta.T, RET_DK, axis=1)
    xi_t = np.repeat(xi.T, RET_DK, axis=1)
    chunk_decay = np.exp(RET_CHUNK * log_g)
    f = lambda a: jnp.asarray(a, dtype=F32)
    return f(cos_t), f(sin_t), f(xi_t), f(zeta_t), f(decay), [float(c) for c in chunk_decay]


def _mixer_kernel(chunk_decay, proj_ref, x_ref, cos_ref, sin_ref, xi_ref, zeta_ref, decay_ref,
                  convw_ref, wc_ref, wr_ref, wm_ref, o_ref, state_ref, tail_ref, yin_ref):
    ts = x_ref.shape[0]

    @pl.when(pl.program_id(1) == 0)
    def _():
        state_ref[...] = jnp.zeros_like(state_ref)
        tail_ref[...] = jnp.zeros_like(tail_ref)

    xin = proj_ref[:, OFF_XIN:OFF_XIN + CONV_WIDTH].astype(F32)
    bg = proj_ref[:, OFF_BG:OFF_BG + CONV_WIDTH].astype(F32)
    cg = proj_ref[:, OFF_CG:OFF_CG + CONV_WIDTH].astype(F32)
    u = cg * xin
    ue = jnp.concatenate([tail_ref[...], u], axis=0)
    u1 = pltpu.roll(ue, 1, 0)[SUBLANES:]
    u2 = pltpu.roll(ue, 2, 0)[SUBLANES:]
    tail_ref[...] = u[ts - SUBLANES:]
    c = convw_ref[2:3, :] * u + convw_ref[1:2, :] * u1 + convw_ref[0:1, :] * u2
    y_conv = _dot((bg * c).astype(BF16), wc_ref[...])

    lane = lax.broadcasted_iota(jnp.int32, (1, LANES), 1)
    low_half = (lane % RET_DK) < (RET_DK // 2)
    head_masks = [(lane // RET_DK) == j for j in range(LANES // RET_DK)]

    def rotary(t, cosv, sinv):
        outs = []
        for p in range(QK_WIDTH // LANES):
            tp = t[:, p * LANES:(p + 1) * LANES]
            fwd = pltpu.roll(tp, LANES - RET_DK // 2, 1)
            bwd = pltpu.roll(tp, RET_DK // 2, 1)
            outs.append(tp * cosv + jnp.where(low_half, fwd, bwd) * sinv)
        return jnp.concatenate(outs, axis=1)

    n_chunks = ts // RET_CHUNK
    chunk_rows = [slice(ci * RET_CHUNK, (ci + 1) * RET_CHUNK) for ci in range(n_chunks)]
    zero = jnp.zeros((), BF16)
    qm, qxm, kb, kz = {}, {}, {}, {}
    for ci, rows in enumerate(chunk_rows):
        cosv = cos_ref[rows, :]
        sinv = sin_ref[rows, :]
        qr = rotary(proj_ref[rows, OFF_Q:OFF_Q + QK_WIDTH].astype(F32), cosv, sinv)
        kr = rotary(proj_ref[rows, OFF_K:OFF_K + QK_WIDTH].astype(F32), cosv, sinv) * (RET_DK ** -0.5)
        qb = qr.astype(BF16)
        qx = (qr * xi_ref[...]).astype(BF16)
        kb[ci] = kr.astype(BF16)
        kz[ci] = (kr * zeta_ref[...]).astype(BF16)
        for h in range(RET_HEADS):
            lanes = slice((h // 2) * LANES, (h // 2 + 1) * LANES)
            qm[ci, h] = jnp.where(head_masks[h % 2], qb[:, lanes], zero)
            qxm[ci, h] = jnp.where(head_masks[h % 2], qx[:, lanes], zero)

    def pair_lanes(h):
        return slice((h // 2) * LANES, (h // 2 + 1) * LANES)

    def v_of(ci, h):
        return proj_ref[chunk_rows[ci], OFF_V + h * RET_DV:OFF_V + (h + 1) * RET_DV]

    state = [state_ref[h] for h in range(RET_HEADS)]
    for c0 in range(0, n_chunks, MIX_STAGE_CHUNKS):
        pairs = [(ci, h) for ci in range(c0, c0 + MIX_STAGE_CHUNKS) for h in range(RET_HEADS)]
        scores = {(ci, h): _dot_nt(qm[ci, h], kb[ci][:, pair_lanes(h)]) for ci, h in pairs}
        kv = {(ci, h): _dot_tn(kz[ci][:, pair_lanes(h)], v_of(ci, h)) for ci, h in pairs}
        probs = {(ci, h): (scores[ci, h] * decay_ref[h]).astype(BF16) for ci, h in pairs}
        inner = {(ci, h): _dot(probs[ci, h], v_of(ci, h)) for ci, h in pairs}
        st_before = {}
        for ci, h in pairs:
            st_before[ci, h] = state[h].astype(BF16)
            state[h] = chunk_decay[h] * state[h] + kv[ci, h]
        o = {(ci, h): inner[ci, h] + _dot(qxm[ci, h], st_before[ci, h]) for ci, h in pairs}
        mu = {p: jnp.mean(o[p], axis=-1, keepdims=True) for p in pairs}
        dlt = {p: o[p] - mu[p] for p in pairs}
        var = {p: jnp.mean(dlt[p] * dlt[p], axis=-1, keepdims=True) for p in pairs}
        for ci, h in pairs:
            g = proj_ref[chunk_rows[ci], OFF_G + h * RET_DV:OFF_G + (h + 1) * RET_DV].astype(F32)
            yin_ref[chunk_rows[ci], h * RET_DV:(h + 1) * RET_DV] = (
                g * _sigmoid(g) * (dlt[ci, h] * lax.rsqrt(var[ci, h] + EPS))).astype(BF16)
    for h in range(RET_HEADS):
        state_ref[h] = state[h]

    for r0 in range(0, ts, MIX_OUT_ROWS):
        rows = slice(r0, r0 + MIX_OUT_ROWS)
        y_ret = _dot(yin_ref[rows, :], wr_ref[...])
        gate_c = proj_ref[rows, OFF_GATE_C:OFF_GATE_C + D_MODEL].astype(F32)
        gate_r = proj_ref[rows, OFF_GATE_R:OFF_GATE_R + D_MODEL].astype(F32)
        merged = _sigmoid(gate_c) * y_conv[rows] + _sigmoid(gate_r) * y_ret
        o_ref[rows, :] = x_ref[rows, :] + _dot(merged.astype(BF16), wm_ref[...])


def _mixer(proj, x2, conv_w, wc_bf, wr_bf, wm_bf, batch, seq):
    ts = MIX_TS
    ns = seq // ts
    cos_t, sin_t, xi_t, zeta_t, decay, chunk_decay = _retention_constants(seq)
    full = lambda shape: pl.BlockSpec(shape, lambda b, s: (0,) * len(shape))
    return pl.pallas_call(
        functools.partial(_mixer_kernel, chunk_decay),
        out_shape=jax.ShapeDtypeStruct((batch * seq, D_MODEL), F32),
        grid=(batch, ns),
        in_specs=[
            pl.BlockSpec((ts, IN_WIDTH), lambda b, s: (b * ns + s, 0)),
            pl.BlockSpec((ts, D_MODEL), lambda b, s: (b * ns + s, 0)),
            pl.BlockSpec((ts, LANES), lambda b, s: (s, 0)),
            pl.BlockSpec((ts, LANES), lambda b, s: (s, 0)),
            full((RET_CHUNK, QK_WIDTH)),
            full((RET_CHUNK, QK_WIDTH)),
            full((RET_HEADS, RET_CHUNK, RET_CHUNK)),
            full((CONV_K, CONV_WIDTH)),
            full((CONV_WIDTH, D_MODEL)),
            full((V_WIDTH, D_MODEL)),
            full((D_MODEL, D_MODEL)),
        ],
        out_specs=pl.BlockSpec((ts, D_MODEL), lambda b, s: (b * ns + s, 0)),
        scratch_shapes=[
            pltpu.VMEM((RET_HEADS, LANES, RET_DV), F32),
            pltpu.VMEM((SUBLANES, CONV_WIDTH), F32),
            pltpu.VMEM((ts, V_WIDTH), BF16),
        ],
        compiler_params=pltpu.CompilerParams(
            dimension_semantics=("arbitrary", "arbitrary"), vmem_limit_bytes=VMEM_LIMIT),
        name="mixer",
    )(proj, x2, cos_t, sin_t, xi_t, zeta_t, decay, conv_w, wc_bf, wr_bf, wm_bf)


def _memkv_kernel(m_ref, g_ref, w_ref, o_ref):
    o_ref[0] = _dot(_rms(m_ref[0], g_ref[...]).astype(BF16), w_ref[...].astype(BF16)).astype(BF16)


def _mem_kv(mem, g, w_kv):
    b = mem.shape[0]
    return pl.pallas_call(
        _memkv_kernel,
        out_shape=jax.ShapeDtypeStruct((b, MEM_LEN, 2 * D_MODEL), BF16),
        grid=(b,),
        in_specs=[
            pl.BlockSpec((1, MEM_LEN, D_MODEL), lambda i: (i, 0, 0)),
            pl.BlockSpec((1, D_MODEL), lambda i: (0, 0)),
            pl.BlockSpec((D_MODEL, 2 * D_MODEL), lambda i: (0, 0)),
        ],
        out_specs=pl.BlockSpec((1, MEM_LEN, 2 * D_MODEL), lambda i: (i, 0, 0)),
        compiler_params=pltpu.CompilerParams(
            dimension_semantics=("arbitrary",), vmem_limit_bytes=VMEM_LIMIT),
        name="mem_kv",
    )(mem, g, w_kv)


def _xattn_kernel(h_ref, kv_ref, gx_ref, wq_ref, wo_ref, gm_ref, rw_ref, rb_ref,
                  h2_ref, xn2_ref, rid_ref, rwt_ref):
    ts = h_ref.shape[0]
    h = h_ref[...]
    q = _dot(_rms(h, gx_ref[...]).astype(BF16), wq_ref[...])
    qb = q.astype(BF16)
    head_cols = [slice(hd * XA_HEAD_DIM, (hd + 1) * XA_HEAD_DIM) for hd in range(XA_HEADS)]
    scores = [_dot_nt(qb[:, cols], kv_ref[0, :, cols]) * (XA_HEAD_DIM ** -0.5) for cols in head_cols]
    probs = []
    for s in scores:
        e = jnp.exp(s - jnp.max(s, axis=-1, keepdims=True))
        probs.append((e / jnp.sum(e, axis=-1, keepdims=True)).astype(BF16))
    outs = [_dot(p, kv_ref[0, :, D_MODEL + hd * XA_HEAD_DIM:D_MODEL + (hd + 1) * XA_HEAD_DIM])
            for hd, p in enumerate(probs)]
    h2 = h + _dot(jnp.concatenate(outs, axis=1).astype(BF16), wo_ref[...])
    h2_ref[...] = h2
    xn = _rms(h2, gm_ref[...])
    _rows_to_tiles(xn2_ref, xn)

    hi = xn.astype(BF16)
    lo = (xn - hi.astype(F32)).astype(BF16)
    hi_prod = _dot(hi, rw_ref[...])
    lo_prod = _dot(lo, rw_ref[:, 0:LANES])
    lg_t = (hi_prod[:, 0:LANES] + lo_prod) + hi_prod[:, LANES:2 * LANES]
    lg = lg_t.T[0:ROUTE_ROWS] + rb_ref[:, 0:1]

    gl = lg[0:N_GROUPS]
    ge = jnp.exp(gl - jnp.max(gl, axis=0, keepdims=True))
    gp = ge / jnp.sum(ge, axis=0, keepdims=True)
    p_g = gp[0:1]
    g_idx = jnp.zeros((1, ts), jnp.int32)
    for i in range(1, N_GROUPS):
        better = gp[i:i + 1] > p_g
        g_idx = jnp.where(better, i, g_idx)
        p_g = jnp.where(better, gp[i:i + 1], p_g)
    sel = jnp.zeros((EXPERTS_PER_GROUP, ts), F32)
    for i in range(N_GROUPS):
        r0 = ROUTE_E0 + i * EXPERTS_PER_GROUP
        sel = jnp.where(g_idx == i, lg[r0:r0 + EXPERTS_PER_GROUP], sel)
    se = jnp.exp(sel - jnp.max(sel, axis=0, keepdims=True))
    sp = se / jnp.sum(se, axis=0, keepdims=True)
    ridx = lax.broadcasted_iota(jnp.int32, (EXPERTS_PER_GROUP, ts), 0)
    m1 = jnp.max(sp, axis=0, keepdims=True)
    i1 = jnp.min(jnp.where(sp == m1, ridx, EXPERTS_PER_GROUP), axis=0, keepdims=True)
    sp2 = jnp.where(ridx == i1, -1.0, sp)
    m2 = jnp.max(sp2, axis=0, keepdims=True)
    i2 = jnp.min(jnp.where(sp2 == m2, ridx, EXPERTS_PER_GROUP), axis=0, keepdims=True)
    den = m1 + m2
    rid_ref[...] = jnp.zeros_like(rid_ref)
    rwt_ref[...] = jnp.zeros_like(rwt_ref)
    rid_ref[0:1, :] = g_idx * EXPERTS_PER_GROUP + i1
    rid_ref[1:2, :] = g_idx * EXPERTS_PER_GROUP + i2
    rwt_ref[0:1, :] = p_g * m1 / den
    rwt_ref[1:2, :] = p_g * m2 / den


def _xattn(h1, kv, gx, wq_bf, wo_bf, gm, rw_cat, rb, batch, seq):
    ts = XA_TS
    ns = seq // ts
    t = batch * seq
    full = lambda shape: pl.BlockSpec(shape, lambda b, s: (0,) * len(shape))
    return pl.pallas_call(
        _xattn_kernel,
        out_shape=(
            jax.ShapeDtypeStruct((t, D_MODEL), F32),
            jax.ShapeDtypeStruct((t * ROW_TILE, LANES), F32),
            jax.ShapeDtypeStruct((SUBLANES, t), jnp.int32),
            jax.ShapeDtypeStruct((SUBLANES, t), F32),
        ),
        grid=(batch, ns),
        in_specs=[
            pl.BlockSpec((ts, D_MODEL), lambda b, s: (b * ns + s, 0)),
            pl.BlockSpec((1, MEM_LEN, 2 * D_MODEL), lambda b, s: (b, 0, 0)),
            full((1, D_MODEL)),
            full((D_MODEL, D_MODEL)),
            full((D_MODEL, D_MODEL)),
            full((1, D_MODEL)),
            full((D_MODEL, 2 * LANES)),
            full((ROUTE_ROWS, LANES)),
        ],
        out_specs=(
            pl.BlockSpec((ts, D_MODEL), lambda b, s: (b * ns + s, 0)),
            pl.BlockSpec((ts * ROW_TILE, LANES), lambda b, s: (b * ns + s, 0)),
            pl.BlockSpec((SUBLANES, ts), lambda b, s: (0, b * ns + s)),
            pl.BlockSpec((SUBLANES, ts), lambda b, s: (0, b * ns + s)),
        ),
        compiler_params=pltpu.CompilerParams(
            dimension_semantics=("arbitrary", "arbitrary"), vmem_limit_bytes=VMEM_LIMIT),
        name="xattn",
    )(h1, kv, gx, wq_bf, wo_bf, gm, rw_cat, rb)


def _tile_of(ref, row):
    start = row * ROW_TILE
    if not isinstance(start, int):
        start = pl.multiple_of(start, ROW_TILE)
    return ref.at[pl.ds(start, ROW_TILE)]


def _row_copy(src_ref, src_row, dst_ref, dst_row, sem):
    return pltpu.make_async_copy(_tile_of(src_ref, src_row), _tile_of(dst_ref, dst_row), sem)


def _dispatch_kernel(last_ref, has_ref, nu_ref, pos0_ref, pos1_ref, xn_ref, xs_ref, zero_ref, sem, zsem):
    pos_refs = (pos0_ref, pos1_ref)
    td = xn_ref.shape[0] // ROW_TILE
    blk = MOE_M * ROW_TILE
    n_blocks = xs_ref.shape[0] // blk

    def fill(start):
        start = pl.multiple_of(start * ROW_TILE, blk)
        return pltpu.make_async_copy(zero_ref, xs_ref.at[pl.ds(start, blk)], zsem)

    @pl.when(pl.program_id(0) == 0)
    def _():
        zero_ref[...] = jnp.zeros_like(zero_ref)

        for e in range(N_EXPERTS):
            @pl.when(has_ref[e] > 0)
            def _():
                fill(last_ref[e]).start()
        for e in range(N_EXPERTS):
            @pl.when(has_ref[e] > 0)
            def _():
                fill(last_ref[e]).wait()

        def start_tail(b, carry):
            fill(b * MOE_M).start()
            return carry

        lax.fori_loop(nu_ref[0], n_blocks, start_tail, 0)

    for r in range(td):
        for k in range(TOP_K):
            _row_copy(xn_ref, r, xs_ref, pos_refs[k][0, 0, r], sem).start(priority=k)
    for k in range(TOP_K):
        pltpu.make_async_copy(xn_ref, xs_ref.at[pl.ds(0, td * ROW_TILE)], sem).wait()

    @pl.when(pl.program_id(0) == pl.num_programs(0) - 1)
    def _():
        def wait_tail(b, carry):
            fill(b * MOE_M).wait()
            return carry

        lax.fori_loop(nu_ref[0], n_blocks, wait_tail, 0)


def _dispatch(last_blk, has, n_used, pos3, xn2, n_rows):
    t = xn2.shape[0] // ROW_TILE
    td = DISP_T
    return pl.pallas_call(
        _dispatch_kernel,
        out_shape=jax.ShapeDtypeStruct((n_rows * ROW_TILE, LANES), F32),
        grid_spec=pltpu.PrefetchScalarGridSpec(
            num_scalar_prefetch=3,
            grid=(t // td,),
            in_specs=[
                pl.BlockSpec((1, 1, td), lambda i, *_: (i, 0, 0), memory_space=pltpu.SMEM),
                pl.BlockSpec((1, 1, td), lambda i, *_: (t // td + i, 0, 0), memory_space=pltpu.SMEM),
                pl.BlockSpec((td * ROW_TILE, LANES), lambda i, *_: (i, 0)),
            ],
            out_specs=pl.BlockSpec(memory_space=pl.ANY),
            scratch_shapes=[
                pltpu.VMEM((MOE_M * ROW_TILE, LANES), F32),
                pltpu.SemaphoreType.DMA,
                pltpu.SemaphoreType.DMA,
            ],
        ),
        compiler_params=pltpu.CompilerParams(
            dimension_semantics=("arbitrary",), vmem_limit_bytes=VMEM_LIMIT),
        name="dispatch",
    )(last_blk, has, n_used, pos3, pos3, xn2)


def _expert_kernel(be_ref, ne_ref, nu_ref, xs_ref, wg_ref, wu_ref, wd_ref, ys_ref,
                   wgf, wuf, wdf, wgb, wub, wdb, wsem, xb0, xb1, xb2, xsem):
    xbufs = (xb0, xb1, xb2)
    assert len(xbufs) == XS_SLOTS
    i = pl.program_id(0)
    e = be_ref[i]

    def fetch(expert):
        return [pltpu.make_async_copy(src.at[expert], dst, wsem)
                for src, dst in ((wg_ref, wgf), (wu_ref, wuf), (wd_ref, wdf))]

    @pl.when(i == 0)
    def _():
        for c in fetch(e):
            c.start()

    @pl.when((i == 0) | (e != be_ref[jnp.maximum(i - 1, 0)]))
    def _():
        for c in fetch(e):
            c.wait()
        wgb[...] = wgf[...].astype(BF16)
        wub[...] = wuf[...].astype(BF16)
        wdb[...] = wdf[...].astype(BF16)

        @pl.when(ne_ref[i] != e)
        def _():
            for c in fetch(ne_ref[i]):
                c.start()

    nu = nu_ref[0]
    blk = MOE_M * ROW_TILE

    def rows_in(b, s):
        start = b * blk if isinstance(b, int) else pl.multiple_of(b * blk, blk)
        return pltpu.make_async_copy(xs_ref.at[pl.ds(start, blk)], xbufs[s], xsem.at[s])

    @pl.when(i == 0)
    def _():
        for b in range(XS_SLOTS - 1):
            @pl.when(b < nu)
            def _():
                rows_in(b, b).start()

    for s in range(XS_SLOTS):
        @pl.when((i < nu) & (lax.rem(i, XS_SLOTS) == s))
        def _():
            @pl.when(i + (XS_SLOTS - 1) < nu)
            def _():
                rows_in(i + (XS_SLOTS - 1), (s + XS_SLOTS - 1) % XS_SLOTS).start()

            rows_in(i, s).wait()
            x = _tiles_to_rows(xbufs[s], MOE_M).astype(BF16)
            g = _dot(x, wgb[...])
            u = _dot(x, wub[...])
            _rows_to_tiles(ys_ref, _dot((g * _sigmoid(g) * u).astype(BF16), wdb[...]))

    @pl.when(i >= nu)
    def _():
        ys_ref[...] = jnp.zeros_like(ys_ref)


def _experts(block_e, next_e, n_used, xs, w_gate, w_up, w_down):
    n_rows = xs.shape[0] // ROW_TILE
    out_map = lambda i, be, ne, nu: (i, 0)
    return pl.pallas_call(
        _expert_kernel,
        out_shape=jax.ShapeDtypeStruct((n_rows * ROW_TILE, LANES), F32),
        grid_spec=pltpu.PrefetchScalarGridSpec(
            num_scalar_prefetch=3,
            grid=(n_rows // MOE_M,),
            in_specs=[
                pl.BlockSpec(memory_space=pl.ANY),
                pl.BlockSpec(memory_space=pl.ANY),
                pl.BlockSpec(memory_space=pl.ANY),
                pl.BlockSpec(memory_space=pl.ANY),
            ],
            out_specs=pl.BlockSpec((MOE_M * ROW_TILE, LANES), out_map),
            scratch_shapes=[
                pltpu.VMEM((D_MODEL, EXPERT_HIDDEN), F32),
                pltpu.VMEM((D_MODEL, EXPERT_HIDDEN), F32),
                pltpu.VMEM((EXPERT_HIDDEN, D_MODEL), F32),
                pltpu.VMEM((D_MODEL, EXPERT_HIDDEN), BF16),
                pltpu.VMEM((D_MODEL, EXPERT_HIDDEN), BF16),
                pltpu.VMEM((EXPERT_HIDDEN, D_MODEL), BF16),
                pltpu.SemaphoreType.DMA,
                pltpu.VMEM((MOE_M * ROW_TILE, LANES), F32),
                pltpu.VMEM((MOE_M * ROW_TILE, LANES), F32),
                pltpu.VMEM((MOE_M * ROW_TILE, LANES), F32),
                pltpu.SemaphoreType.DMA((XS_SLOTS,)),
            ],
        ),
        compiler_params=pltpu.CompilerParams(
            dimension_semantics=("arbitrary",), vmem_limit_bytes=VMEM_LIMIT),
        name="experts",
    )(block_e, next_e, n_used, xs, w_gate, w_up, w_down)


def _combine_kernel(final_norm, pos0_ref, pos1_ref, nxt0_ref, nxt1_ref, h2_ref, wt_ref, g_ref, ys_ref,
                    o_ref, b00_ref, b01_ref, b10_ref, b11_ref, sem):
    i = pl.program_id(0)
    n_steps = pl.num_programs(0)
    tc = h2_ref.shape[0]
    bufs = ((b00_ref, b01_ref), (b10_ref, b11_ref))

    def issue(rows_refs, s):
        for r in range(tc):
            for k in range(TOP_K):
                _row_copy(ys_ref, rows_refs[k][0, 0, r], bufs[s][k], r, sem.at[s]).start(priority=k)

    @pl.when(i == 0)
    def _():
        issue((pos0_ref, pos1_ref), 0)

    for s in range(2):
        @pl.when(i % 2 == s)
        def _():
            @pl.when(i + 1 < n_steps)
            def _():
                issue((nxt0_ref, nxt1_ref), 1 - s)

            for k in range(TOP_K):
                pltpu.make_async_copy(ys_ref.at[pl.ds(0, tc * ROW_TILE)], bufs[s][k], sem.at[s]).wait()
            y = (h2_ref[...] + wt_ref[:, 0:1] * _tiles_to_rows(bufs[s][0], tc)
                 + wt_ref[:, 1:2] * _tiles_to_rows(bufs[s][1], tc))
            o_ref[...] = _rms(y, g_ref[...]) if final_norm else y


def _combine(pos3, h2, wt, g, ys, final_norm):
    t = h2.shape[0]
    tc = COMB_T
    nt = t // tc
    return pl.pallas_call(
        functools.partial(_combine_kernel, final_norm),
        out_shape=jax.ShapeDtypeStruct((t, D_MODEL), F32),
        grid=(nt,),
        in_specs=[
            pl.BlockSpec((1, 1, tc), lambda i: (i, 0, 0), memory_space=pltpu.SMEM),
            pl.BlockSpec((1, 1, tc), lambda i: (nt + i, 0, 0), memory_space=pltpu.SMEM),
            pl.BlockSpec((1, 1, tc), lambda i: (jnp.minimum(i + 1, nt - 1), 0, 0), memory_space=pltpu.SMEM),
            pl.BlockSpec((1, 1, tc), lambda i: (nt + jnp.minimum(i + 1, nt - 1), 0, 0),
                         memory_space=pltpu.SMEM),
            pl.BlockSpec((tc, D_MODEL), lambda i: (i, 0)),
            pl.BlockSpec((tc, TOP_K), lambda i: (i, 0)),
            pl.BlockSpec((1, D_MODEL), lambda i: (0, 0)),
            pl.BlockSpec(memory_space=pl.ANY),
        ],
        out_specs=pl.BlockSpec((tc, D_MODEL), lambda i: (i, 0)),
        scratch_shapes=[
            pltpu.VMEM((tc * ROW_TILE, LANES), F32),
            pltpu.VMEM((tc * ROW_TILE, LANES), F32),
            pltpu.VMEM((tc * ROW_TILE, LANES), F32),
            pltpu.VMEM((tc * ROW_TILE, LANES), F32),
            pltpu.SemaphoreType.DMA((2,)),
        ],
        compiler_params=pltpu.CompilerParams(
            dimension_semantics=("arbitrary",), vmem_limit_bytes=VMEM_LIMIT),
        name="combine",
    )(pos3, pos3, pos3, pos3, h2, wt, g, ys)


def _routing_tables(rid, n_tok):
    e_flat = rid[:TOP_K].reshape(-1)
    onehot = (e_flat[None, :] == jnp.arange(N_EXPERTS, dtype=jnp.int32)[:, None]).astype(jnp.int32)
    csum = jnp.cumsum(onehot, axis=1)
    counts = csum[:, -1]
    rank = jnp.sum(onehot * csum, axis=0) - 1
    padded = (counts + MOE_M - 1) // MOE_M * MOE_M
    pend = jnp.cumsum(padded)
    pstart = pend - padded
    pos = jnp.sum(onehot * pstart[:, None], axis=0) + rank
    n_blocks = (TOP_K * n_tok + N_EXPERTS * (MOE_M - 1) + MOE_M - 1) // MOE_M
    n_used = pend[-1] // MOE_M
    blk = jnp.minimum(jnp.arange(n_blocks, dtype=jnp.int32), n_used - 1)
    block_e = jnp.sum((pend[None, :] <= (blk * MOE_M)[:, None]).astype(jnp.int32), axis=1)
    block_e = jnp.minimum(block_e, N_EXPERTS - 1)
    last_blk = jnp.maximum(pend - MOE_M, 0).astype(jnp.int32)
    has = (counts > 0).astype(jnp.int32)
    ids = jnp.arange(N_EXPERTS, dtype=jnp.int32)
    later = (ids[None, :] > ids[:, None]) & (counts[None, :] > 0)
    nxt = jnp.min(jnp.where(later, ids[None, :], N_EXPERTS), axis=1)
    nxt = jnp.where(nxt == N_EXPERTS, ids, nxt)
    next_e = jnp.sum(jnp.where(block_e[:, None] == ids[None, :], nxt[None, :], 0), axis=1)
    return (pos.reshape(TOP_K, n_tok).astype(jnp.int32), block_e.astype(jnp.int32),
            next_e.astype(jnp.int32), n_used.reshape(1).astype(jnp.int32), last_blk, has,
            n_blocks * MOE_M)


def _tile_pos(pos, tile):
    k, t = pos.shape
    return pos.reshape(k * (t // tile), 1, tile)


def kernel(x, mem, mix_norm_g, w_in, conv_w, w_conv_out, w_ret_out, w_mix_out, xa_norm_g, mem_norm_g,
           w_xa_q, w_xa_kv, w_xa_o, moe_norm_g, w_group, b_group, w_router, b_router, w_gate, w_up,
           w_down, final_norm_g):
    batch, seq, d = x.shape
    depth = w_in.shape[0]
    t = batch * seq
    h = x.reshape(t, d)
    for l in range(depth):
        proj = _in_proj(h, mix_norm_g[l][None], w_in[l])
        h1 = _mixer(proj, h, conv_w[l], w_conv_out[l].astype(BF16), w_ret_out[l].astype(BF16),
                    w_mix_out[l].astype(BF16), batch, seq)
        kv = _mem_kv(mem, mem_norm_g[l][None], w_xa_kv[l])

        rw = jnp.zeros((LANES, d), F32)
        rw = rw.at[0:N_GROUPS].set(w_group[l].T).at[ROUTE_E0:ROUTE_E0 + N_EXPERTS].set(w_router[l].T)
        rwh = rw.astype(BF16)
        rwl = (rw - rwh.astype(F32)).astype(BF16)
        rw_cat = jnp.concatenate([rwh.T, rwl.T], axis=1)
        rb = jnp.zeros((ROUTE_ROWS,), F32)
        rb = rb.at[0:N_GROUPS].set(b_group[l]).at[ROUTE_E0:ROUTE_E0 + N_EXPERTS].set(b_router[l])
        rb = jnp.broadcast_to(rb[:, None], (ROUTE_ROWS, LANES))

        h2, xn2, rid, rwt = _xattn(h1, kv, xa_norm_g[l][None], w_xa_q[l].astype(BF16),
                                   w_xa_o[l].astype(BF16), moe_norm_g[l][None], rw_cat, rb, batch, seq)

        pos, block_e, next_e, n_used, last_blk, has, n_rows = _routing_tables(rid, t)
        xs = _dispatch(last_blk, has, n_used, _tile_pos(pos, DISP_T), xn2, n_rows)
        ys = _experts(block_e, next_e, n_used, xs, w_gate[l], w_up[l], w_down[l])
        h = _combine(_tile_pos(pos, COMB_T), h2, rwt[:TOP_K].T, final_norm_g[None], ys,
                     final_norm=(l == depth - 1))
    return h.reshape(batch, seq, d)
```

```python
import functools

import numpy as np
import jax
import jax.numpy as jnp
from jax import lax
from jax.experimental import pallas as pl
from jax.experimental.pallas import tpu as pltpu

F32 = jnp.float32
BF16 = jnp.bfloat16

D_MODEL = 1024
CONV_WIDTH = 512
CONV_K = 3
RET_HEADS = 8
RET_DK = 64
RET_DV = 128
RET_CHUNK = 128
QK_WIDTH = RET_HEADS * RET_DK
V_WIDTH = RET_HEADS * RET_DV
ROPE_BASE = 10000.0
IN_WIDTH = 3 * CONV_WIDTH + 2 * QK_WIDTH + 2 * V_WIDTH + 2 * D_MODEL
OFF_XIN = 0
OFF_BG = OFF_XIN + CONV_WIDTH
OFF_CG = OFF_BG + CONV_WIDTH
OFF_Q = OFF_CG + CONV_WIDTH
OFF_K = OFF_Q + QK_WIDTH
OFF_V = OFF_K + QK_WIDTH
OFF_G = OFF_V + V_WIDTH
OFF_GATE_C = OFF_G + V_WIDTH
OFF_GATE_R = OFF_GATE_C + D_MODEL
MEM_LEN = 256
XA_HEADS = 4
XA_HEAD_DIM = D_MODEL // XA_HEADS
N_GROUPS = 4
EXPERTS_PER_GROUP = 8
N_EXPERTS = N_GROUPS * EXPERTS_PER_GROUP
TOP_K = 2
EXPERT_HIDDEN = D_MODEL // 2
EPS = 1e-6

LANES = 128
SUBLANES = 8
VMEM_LIMIT = 56 * 1024 * 1024

IN_TM = 2048
IN_TN = 1664
MIX_TS = 512
MIX_OUT_ROWS = 256
MIX_STAGE_CHUNKS = 4
XA_TS = 1024
ROUTE_ROWS = 40
ROUTE_E0 = 8
MOE_M = 512
DISP_T = 2048
COMB_T = 256
XS_SLOTS = 3


def _rms(x, g):
    ms = jnp.mean(x * x, axis=-1, keepdims=True)
    return x * lax.rsqrt(ms + EPS) * g


def _sigmoid(x):
    return 1.0 / (1.0 + jnp.exp(-x))


def _dot(a, b):
    return jnp.dot(a, b, preferred_element_type=F32)


def _dot_nt(a, b):
    return lax.dot_general(a, b, (((1,), (1,)), ((), ())), preferred_element_type=F32)


def _dot_tn(a, b):
    return lax.dot_general(a, b, (((0,), (0,)), ((), ())), preferred_element_type=F32)


ROW_TILE = D_MODEL // LANES


def _rows_to_tiles(ref, x):
    n = x.shape[0]
    for c in range(ROW_TILE):
        ref[pl.ds(c, n, stride=ROW_TILE), :] = x[:, c * LANES:(c + 1) * LANES]


def _tiles_to_rows(ref, n):
    return jnp.concatenate([ref[pl.ds(c, n, stride=ROW_TILE), :] for c in range(ROW_TILE)], axis=1)


def _inproj_kernel(x_ref, g_ref, w_ref, o_ref, xn_ref):
    @pl.when(pl.program_id(1) == 0)
    def _():
        xn_ref[...] = _rms(x_ref[...], g_ref[...]).astype(BF16)

    o_ref[...] = _dot(xn_ref[...], w_ref[...].astype(BF16)).astype(BF16)


def _in_proj(x2, g, w_in):
    t = x2.shape[0]
    return pl.pallas_call(
        _inproj_kernel,
        out_shape=jax.ShapeDtypeStruct((t, IN_WIDTH), BF16),
        grid=(t // IN_TM, IN_WIDTH // IN_TN),
        in_specs=[
            pl.BlockSpec((IN_TM, D_MODEL), lambda i, j: (i, 0)),
            pl.BlockSpec((1, D_MODEL), lambda i, j: (0, 0)),
            pl.BlockSpec((D_MODEL, IN_TN), lambda i, j: (0, j)),
        ],
        out_specs=pl.BlockSpec((IN_TM, IN_TN), lambda i, j: (i, j)),
        scratch_shapes=[pltpu.VMEM((IN_TM, D_MODEL), BF16)],
        compiler_params=pltpu.CompilerParams(
            dimension_semantics=("arbitrary", "arbitrary"), vmem_limit_bytes=VMEM_LIMIT),
        name="in_proj",
    )(x2, g, w_in)


def _retention_constants(seq):
    pos = np.arange(seq, dtype=np.float64)
    inv_freq = ROPE_BASE ** (-np.arange(0, RET_DK, 2, dtype=np.float64) / RET_DK)
    ang = pos[:, None] * inv_freq[None, :]
    cos, sin = np.cos(ang), np.sin(ang)
    cos_t = np.concatenate([cos, cos, cos, cos], axis=1)
    sin_t = np.concatenate([-sin, sin, -sin, sin], axis=1)
    log_g = np.log(1.0 - 2.0 ** (-5.0 - np.arange(RET_HEADS, dtype=np.float64)))
    idx = np.arange(RET_CHUNK, dtype=np.float64)
    diff = idx[:, None] - idx[None, :]
    decay = np.where(diff >= 0, np.exp(np.maximum(diff, 0.0)[None] * log_g[:, None, None]), 0.0)
    zeta = np.exp((RET_CHUNK - 1 - idx)[None, :] * log_g[:, None])
    xi = np.exp((idx + 1)[None, :] * log_g[:, None])
    zeta_t = np.repeat(zeta.T, RET_DK, axis=1)
    xi_t = np.repeat(xi.T, RET_DK, axis=1)
    chunk_decay = np.exp(RET_CHUNK * log_g)
    f = lambda a: jnp.asarray(a, dtype=F32)
    return f(cos_t), f(sin_t), f(xi_t), f(zeta_t), f(decay), [float(c) for c in chunk_decay]


def _mixer_kernel(chunk_decay, proj_ref, x_ref, cos_ref, sin_ref, xi_ref, zeta_ref, decay_ref,
                  convw_ref, wc_ref, wr_ref, wm_ref, o_ref, state_ref, tail_ref, yin_ref):
    ts = x_ref.shape[0]

    @pl.when(pl.program_id(1) == 0)
    def _():
        state_ref[...] = jnp.zeros_like(state_ref)
        tail_ref[...] = jnp.zeros_like(tail_ref)

    xin = proj_ref[:, OFF_XIN:OFF_XIN + CONV_WIDTH].astype(F32)
    bg = proj_ref[:, OFF_BG:OFF_BG + CONV_WIDTH].astype(F32)
    cg = proj_ref[:, OFF_CG:OFF_CG + CONV_WIDTH].astype(F32)
    u = cg * xin
    ue = jnp.concatenate([tail_ref[...], u], axis=0)
    u1 = pltpu.roll(ue, 1, 0)[SUBLANES:]
    u2 = pltpu.roll(ue, 2, 0)[SUBLANES:]
    tail_ref[...] = u[ts - SUBLANES:]
    c = convw_ref[2:3, :] * u + convw_ref[1:2, :] * u1 + convw_ref[0:1, :] * u2
    y_conv = _dot((bg * c).astype(BF16), wc_ref[...])

    lane = lax.broadcasted_iota(jnp.int32, (1, LANES), 1)
    low_half = (lane % RET_DK) < (RET_DK // 2)
    head_masks = [(lane // RET_DK) == j for j in range(LANES // RET_DK)]

    def rotary(t, cosv, sinv):
        outs = []
        for p in range(QK_WIDTH // LANES):
            tp = t[:, p * LANES:(p + 1) * LANES]
            fwd = pltpu.roll(tp, LANES - RET_DK // 2, 1)
            bwd = pltpu.roll(tp, RET_DK // 2, 1)
            outs.append(tp * cosv + jnp.where(low_half, fwd, bwd) * sinv)
        return jnp.concatenate(outs, axis=1)

    n_chunks = ts // RET_CHUNK
    chunk_rows = [slice(ci * RET_CHUNK, (ci + 1) * RET_CHUNK) for ci in range(n_chunks)]
    zero = jnp.zeros((), BF16)
    qm, qxm, kb, kz = {}, {}, {}, {}
    for ci, rows in enumerate(chunk_rows):
        cosv = cos_ref[rows, :]
        sinv = sin_ref[rows, :]
        qr = rotary(proj_ref[rows, OFF_Q:OFF_Q + QK_WIDTH].astype(F32), cosv, sinv)
        kr = rotary(proj_ref[rows, OFF_K:OFF_K + QK_WIDTH].astype(F32), cosv, sinv) * (RET_DK ** -0.5)
        qb = qr.astype(BF16)
        qx = (qr * xi_ref[...]).astype(BF16)
        kb[ci] = kr.astype(BF16)
        kz[ci] = (kr * zeta_ref[...]).astype(BF16)
        for h in range(RET_HEADS):
            lanes = slice((h // 2) * LANES, (h // 2 + 1) * LANES)
            qm[ci, h] = jnp.where(head_masks[h % 2], qb[:, lanes], zero)
            qxm[ci, h] = jnp.where(head_masks[h % 2], qx[:, lanes], zero)

    def pair_lanes(h):
        return slice((h // 2) * LANES, (h // 2 + 1) * LANES)

    def v_of(ci, h):
        return proj_ref[chunk_rows[ci], OFF_V + h * RET_DV:OFF_V + (h + 1) * RET_DV]

    state = [state_ref[h] for h in range(RET_HEADS)]
    for c0 in range(0, n_chunks, MIX_STAGE_CHUNKS):
        pairs = [(ci, h) for ci in range(c0, c0 + MIX_STAGE_CHUNKS) for h in range(RET_HEADS)]
        scores = {(ci, h): _dot_nt(qm[ci, h], kb[ci][:, pair_lanes(h)]) for ci, h in pairs}
        kv = {(ci, h): _dot_tn(kz[ci][:, pair_lanes(h)], v_of(ci, h)) for ci, h in pairs}
        probs = {(ci, h): (scores[ci, h] * decay_ref[h]).astype(BF16) for ci, h in pairs}
        inner = {(ci, h): _dot(probs[ci, h], v_of(ci, h)) for ci, h in pairs}
        st_before = {}
        for ci, h in pairs:
            st_before[ci, h] = state[h].astype(BF16)
            state[h] = chunk_decay[h] * state[h] + kv[ci, h]
        o = {(ci, h): inner[ci, h] + _dot(qxm[ci, h], st_before[ci, h]) for ci, h in pairs}
        mu = {p: jnp.mean(o[p], axis=-1, keepdims=True) for p in pairs}
        dlt = {p: o[p] - mu[p] for p in pairs}
        var = {p: jnp.mean(dlt[p] * dlt[p], axis=-1, keepdims=True) for p in pairs}
        for ci, h in pairs:
            g = proj_ref[chunk_rows[ci], OFF_G + h * RET_DV:OFF_G + (h + 1) * RET_DV].astype(F32)
            yin_ref[chunk_rows[ci], h * RET_DV:(h + 1) * RET_DV] = (
                g * _sigmoid(g) * (dlt[ci, h] * lax.rsqrt(var[ci, h] + EPS))).astype(BF16)
    for h in range(RET_HEADS):
        state_ref[h] = state[h]

    for r0 in range(0, ts, MIX_OUT_ROWS):
        rows = slice(r0, r0 + MIX_OUT_ROWS)
        y_ret = _dot(yin_ref[rows, :], wr_ref[...])
        gate_c = proj_ref[rows, OFF_GATE_C:OFF_GATE_C + D_MODEL].astype(F32)
        gate_r = proj_ref[rows, OFF_GATE_R:OFF_GATE_R + D_MODEL].astype(F32)
        merged = _sigmoid(gate_c) * y_conv[rows] + _sigmoid(gate_r) * y_ret
        o_ref[rows, :] = x_ref[rows, :] + _dot(merged.astype(BF16), wm_ref[...])


def _mixer(proj, x2, conv_w, wc_bf, wr_bf, wm_bf, batch, seq):
    ts = MIX_TS
    ns = seq // ts
    cos_t, sin_t, xi_t, zeta_t, decay, chunk_decay = _retention_constants(seq)
    full = lambda shape: pl.BlockSpec(shape, lambda b, s: (0,) * len(shape))
    return pl.pallas_call(
        functools.partial(_mixer_kernel, chunk_decay),
        out_shape=jax.ShapeDtypeStruct((batch * seq, D_MODEL), F32),
        grid=(batch, ns),
        in_specs=[
            pl.BlockSpec((ts, IN_WIDTH), lambda b, s: (b * ns + s, 0)),
            pl.BlockSpec((ts, D_MODEL), lambda b, s: (b * ns + s, 0)),
            pl.BlockSpec((ts, LANES), lambda b, s: (s, 0)),
            pl.BlockSpec((ts, LANES), lambda b, s: (s, 0)),
            full((RET_CHUNK, QK_WIDTH)),
            full((RET_CHUNK, QK_WIDTH)),
            full((RET_HEADS, RET_CHUNK, RET_CHUNK)),
            full((CONV_K, CONV_WIDTH)),
            full((CONV_WIDTH, D_MODEL)),
            full((V_WIDTH, D_MODEL)),
            full((D_MODEL, D_MODEL)),
        ],
        out_specs=pl.BlockSpec((ts, D_MODEL), lambda b, s: (b * ns + s, 0)),
        scratch_shapes=[
            pltpu.VMEM((RET_HEADS, LANES, RET_DV), F32),
            pltpu.VMEM((SUBLANES, CONV_WIDTH), F32),
            pltpu.VMEM((ts, V_WIDTH), BF16),
        ],
        compiler_params=pltpu.CompilerParams(
            dimension_semantics=("arbitrary", "arbitrary"), vmem_limit_bytes=VMEM_LIMIT),
        name="mixer",
    )(proj, x2, cos_t, sin_t, xi_t, zeta_t, decay, conv_w, wc_bf, wr_bf, wm_bf)


def _memkv_kernel(m_ref, g_ref, w_ref, o_ref):
    o_ref[0] = _dot(_rms(m_ref[0], g_ref[...]).astype(BF16), w_ref[...].astype(BF16)).astype(BF16)


def _mem_kv(mem, g, w_kv):
    b = mem.shape[0]
    return pl.pallas_call(
        _memkv_kernel,
        out_shape=jax.ShapeDtypeStruct((b, MEM_LEN, 2 * D_MODEL), BF16),
        grid=(b,),
        in_specs=[
            pl.BlockSpec((1, MEM_LEN, D_MODEL), lambda i: (i, 0, 0)),
            pl.BlockSpec((1, D_MODEL), lambda i: (0, 0)),
            pl.BlockSpec((D_MODEL, 2 * D_MODEL), lambda i: (0, 0)),
        ],
        out_specs=pl.BlockSpec((1, MEM_LEN, 2 * D_MODEL), lambda i: (i, 0, 0)),
        compiler_params=pltpu.CompilerParams(
            dimension_semantics=("arbitrary",), vmem_limit_bytes=VMEM_LIMIT),
        name="mem_kv",
    )(mem, g, w_kv)


def _xattn_kernel(h_ref, kv_ref, gx_ref, wq_ref, wo_ref, gm_ref, rw_ref, rb_ref,
                  h2_ref, xn2_ref, rid_ref, rwt_ref):
    ts = h_ref.shape[0]
    h = h_ref[...]
    q = _dot(_rms(h, gx_ref[...]).astype(BF16), wq_ref[...])
    qb = q.astype(BF16)
    head_cols = [slice(hd * XA_HEAD_DIM, (hd + 1) * XA_HEAD_DIM) for hd in range(XA_HEADS)]
    scores = [_dot_nt(qb[:, cols], kv_ref[0, :, cols]) * (XA_HEAD_DIM ** -0.5) for cols in head_cols]
    probs = []
    for s in scores:
        e = jnp.exp(s - jnp.max(s, axis=-1, keepdims=True))
        probs.append((e / jnp.sum(e, axis=-1, keepdims=True)).astype(BF16))
    outs = [_dot(p, kv_ref[0, :, D_MODEL + hd * XA_HEAD_DIM:D_MODEL + (hd + 1) * XA_HEAD_DIM])
            for hd, p in enumerate(probs)]
    h2 = h + _dot(jnp.concatenate(outs, axis=1).astype(BF16), wo_ref[...])
    h2_ref[...] = h2
    xn = _rms(h2, gm_ref[...])
    _rows_to_tiles(xn2_ref, xn)

    hi = xn.astype(BF16)
    lo = (xn - hi.astype(F32)).astype(BF16)
    hi_prod = _dot(hi, rw_ref[...])
    lo_prod = _dot(lo, rw_ref[:, 0:LANES])
    lg_t = (hi_prod[:, 0:LANES] + lo_prod) + hi_prod[:, LANES:2 * LANES]
    lg = lg_t.T[0:ROUTE_ROWS] + rb_ref[:, 0:1]

    gl = lg[0:N_GROUPS]
    ge = jnp.exp(gl - jnp.max(gl, axis=0, keepdims=True))
    gp = ge / jnp.sum(ge, axis=0, keepdims=True)
    p_g = gp[0:1]
    g_idx = jnp.zeros((1, ts), jnp.int32)
    for i in range(1, N_GROUPS):
        better = gp[i:i + 1] > p_g
        g_idx = jnp.where(better, i, g_idx)
        p_g = jnp.where(better, gp[i:i + 1], p_g)
    sel = jnp.zeros((EXPERTS_PER_GROUP, ts), F32)
    for i in range(N_GROUPS):
        r0 = ROUTE_E0 + i * EXPERTS_PER_GROUP
        sel = jnp.where(g_idx == i, lg[r0:r0 + EXPERTS_PER_GROUP], sel)
    se = jnp.exp(sel - jnp.max(sel, axis=0, keepdims=True))
    sp = se / jnp.sum(se, axis=0, keepdims=True)
    ridx = lax.broadcasted_iota(jnp.int32, (EXPERTS_PER_GROUP, ts), 0)
    m1 = jnp.max(sp, axis=0, keepdims=True)
    i1 = jnp.min(jnp.where(sp == m1, ridx, EXPERTS_PER_GROUP), axis=0, keepdims=True)
    sp2 = jnp.where(ridx == i1, -1.0, sp)
    m2 = jnp.max(sp2, axis=0, keepdims=True)
    i2 = jnp.min(jnp.where(sp2 == m2, ridx, EXPERTS_PER_GROUP), axis=0, keepdims=True)
    den = m1 + m2
    rid_ref[...] = jnp.zeros_like(rid_ref)
    rwt_ref[...] = jnp.zeros_like(rwt_ref)
    rid_ref[0:1, :] = g_idx * EXPERTS_PER_GROUP + i1
    rid_ref[1:2, :] = g_idx * EXPERTS_PER_GROUP + i2
    rwt_ref[0:1, :] = p_g * m1 / den
    rwt_ref[1:2, :] = p_g * m2 / den


def _xattn(h1, kv, gx, wq_bf, wo_bf, gm, rw_cat, rb, batch, seq):
    ts = XA_TS
    ns = seq // ts
    t = batch * seq
    full = lambda shape: pl.BlockSpec(shape, lambda b, s: (0,) * len(shape))
    return pl.pallas_call(
        _xattn_kernel,
        out_shape=(
            jax.ShapeDtypeStruct((t, D_MODEL), F32),
            jax.ShapeDtypeStruct((t * ROW_TILE, LANES), F32),
            jax.ShapeDtypeStruct((SUBLANES, t), jnp.int32),
            jax.ShapeDtypeStruct((SUBLANES, t), F32),
        ),
        grid=(batch, ns),
        in_specs=[
            pl.BlockSpec((ts, D_MODEL), lambda b, s: (b * ns + s, 0)),
            pl.BlockSpec((1, MEM_LEN, 2 * D_MODEL), lambda b, s: (b, 0, 0)),
            full((1, D_MODEL)),
            full((D_MODEL, D_MODEL)),
            full((D_MODEL, D_MODEL)),
            full((1, D_MODEL)),
            full((D_MODEL, 2 * LANES)),
            full((ROUTE_ROWS, LANES)),
        ],
        out_specs=(
            pl.BlockSpec((ts, D_MODEL), lambda b, s: (b * ns + s, 0)),
            pl.BlockSpec((ts * ROW_TILE, LANES), lambda b, s: (b * ns + s, 0)),
            pl.BlockSpec((SUBLANES, ts), lambda b, s: (0, b * ns + s)),
            pl.BlockSpec((SUBLANES, ts), lambda b, s: (0, b * ns + s)),
        ),
        compiler_params=pltpu.CompilerParams(
            dimension_semantics=("arbitrary", "arbitrary"), vmem_limit_bytes=VMEM_LIMIT),
        name="xattn",
    )(h1, kv, gx, wq_bf, wo_bf, gm, rw_cat, rb)


def _tile_of(ref, row):
    start = row * ROW_TILE
    if not isinstance(start, int):
        start = pl.multiple_of(start, ROW_TILE)
    return ref.at[pl.ds(start, ROW_TILE)]


def _row_copy(src_ref, src_row, dst_ref, dst_row, sem):
    return pltpu.make_async_copy(_tile_of(src_ref, src_row), _tile_of(dst_ref, dst_row), sem)


def _dispatch_kernel(last_ref, has_ref, nu_ref, pos0_ref, pos1_ref, xn_ref, xs_ref, zero_ref, sem, zsem):
    pos_refs = (pos0_ref, pos1_ref)
    td = xn_ref.shape[0] // ROW_TILE
    blk = MOE_M * ROW_TILE
    n_blocks = xs_ref.shape[0] // blk

    def fill(start):
        start = pl.multiple_of(start * ROW_TILE, blk)
        return pltpu.make_async_copy(zero_ref, xs_ref.at[pl.ds(start, blk)], zsem)

    @pl.when(pl.program_id(0) == 0)
    def _():
        zero_ref[...] = jnp.zeros_like(zero_ref)

        for e in range(N_EXPERTS):
            @pl.when(has_ref[e] > 0)
            def _():
                fill(last_ref[e]).start()
        for e in range(N_EXPERTS):
            @pl.when(has_ref[e] > 0)
            def _():
                fill(last_ref[e]).wait()

        def start_tail(b, carry):
            fill(b * MOE_M).start()
            return carry

        lax.fori_loop(nu_ref[0], n_blocks, start_tail, 0)

    for r in range(td):
        for k in range(TOP_K):
            _row_copy(xn_ref, r, xs_ref, pos_refs[k][0, 0, r], sem).start(priority=k)
    for k in range(TOP_K):
        pltpu.make_async_copy(xn_ref, xs_ref.at[pl.ds(0, td * ROW_TILE)], sem).wait()

    @pl.when(pl.program_id(0) == pl.num_programs(0) - 1)
    def _():
        def wait_tail(b, carry):
            fill(b * MOE_M).wait()
            return carry

        lax.fori_loop(nu_ref[0], n_blocks, wait_tail, 0)


def _dispatch(last_blk, has, n_used, pos3, xn2, n_rows):
    t = xn2.shape[0] // ROW_TILE
    td = DISP_T
    return pl.pallas_call(
        _dispatch_kernel,
        out_shape=jax.ShapeDtypeStruct((n_rows * ROW_TILE, LANES), F32),
        grid_spec=pltpu.PrefetchScalarGridSpec(
            num_scalar_prefetch=3,
            grid=(t // td,),
            in_specs=[
                pl.BlockSpec((1, 1, td), lambda i, *_: (i, 0, 0), memory_space=pltpu.SMEM),
                pl.BlockSpec((1, 1, td), lambda i, *_: (t // td + i, 0, 0), memory_space=pltpu.SMEM),
                pl.BlockSpec((td * ROW_TILE, LANES), lambda i, *_: (i, 0)),
            ],
            out_specs=pl.BlockSpec(memory_space=pl.ANY),
            scratch_shapes=[
                pltpu.VMEM((MOE_M * ROW_TILE, LANES), F32),
                pltpu.SemaphoreType.DMA,
                pltpu.SemaphoreType.DMA,
            ],
        ),
        compiler_params=pltpu.CompilerParams(
            dimension_semantics=("arbitrary",), vmem_limit_bytes=VMEM_LIMIT),
        name="dispatch",
    )(last_blk, has, n_used, pos3, pos3, xn2)


def _expert_kernel(be_ref, ne_ref, bv_ref, nu_ref, xs_ref, wg_ref, wu_ref, wd_ref, ys_ref,
                   wgf, wuf, wdf, wgb, wub, wdb, wsem, xb0, xb1, xb2, xsem):
    xbufs = (xb0, xb1, xb2)
    assert len(xbufs) == XS_SLOTS
    i = pl.program_id(0)
    e = be_ref[i]

    def fetch(expert):
        return [pltpu.make_async_copy(src.at[expert], dst, wsem)
                for src, dst in ((wg_ref, wgf), (wu_ref, wuf), (wd_ref, wdf))]

    @pl.when(i == 0)
    def _():
        for c in fetch(e):
            c.start()

    @pl.when((i == 0) | (e != be_ref[jnp.maximum(i - 1, 0)]))
    def _():
        for c in fetch(e):
            c.wait()
        wgb[...] = wgf[...].astype(BF16)
        wub[...] = wuf[...].astype(BF16)
        wdb[...] = wdf[...].astype(BF16)

        @pl.when(ne_ref[i] != e)
        def _():
            for c in fetch(ne_ref[i]):
                c.start()

    nu = nu_ref[0]
    blk = MOE_M * ROW_TILE

    def rows_in(b, s):
        start = b * blk if isinstance(b, int) else pl.multiple_of(b * blk, blk)
        return pltpu.make_async_copy(xs_ref.at[pl.ds(start, blk)], xbufs[s], xsem.at[s])

    @pl.when(i == 0)
    def _():
        for b in range(XS_SLOTS - 1):
            @pl.when(b < nu)
            def _():
                rows_in(b, b).start()

    for s in range(XS_SLOTS):
        @pl.when((i < nu) & (lax.rem(i, XS_SLOTS) == s))
        def _():
            @pl.when(i + (XS_SLOTS - 1) < nu)
            def _():
                rows_in(i + (XS_SLOTS - 1), (s + XS_SLOTS - 1) % XS_SLOTS).start()

            rows_in(i, s).wait()

            def mlp(n_rows):
                x = _tiles_to_rows(xbufs[s], n_rows).astype(BF16)
                g = _dot(x, wgb[...])
                u = _dot(x, wub[...])
                _rows_to_tiles(ys_ref, _dot((g * _sigmoid(g) * u).astype(BF16), wdb[...]))

            half = MOE_M // 2

            @pl.when(bv_ref[i] > half)
            def _():
                mlp(MOE_M)

            @pl.when(bv_ref[i] <= half)
            def _():
                mlp(half)
                ys_ref[pl.ds(half * ROW_TILE, half * ROW_TILE), :] = jnp.zeros(
                    (half * ROW_TILE, LANES), F32)

    @pl.when(i >= nu)
    def _():
        ys_ref[...] = jnp.zeros_like(ys_ref)


def _experts(block_e, next_e, block_valid, n_used, xs, w_gate, w_up, w_down):
    n_rows = xs.shape[0] // ROW_TILE
    out_map = lambda i, be, ne, bv, nu: (i, 0)
    return pl.pallas_call(
        _expert_kernel,
        out_shape=jax.ShapeDtypeStruct((n_rows * ROW_TILE, LANES), F32),
        grid_spec=pltpu.PrefetchScalarGridSpec(
            num_scalar_prefetch=4,
            grid=(n_rows // MOE_M,),
            in_specs=[
                pl.BlockSpec(memory_space=pl.ANY),
                pl.BlockSpec(memory_space=pl.ANY),
                pl.BlockSpec(memory_space=pl.ANY),
                pl.BlockSpec(memory_space=pl.ANY),
            ],
            out_specs=pl.BlockSpec((MOE_M * ROW_TILE, LANES), out_map),
            scratch_shapes=[
                pltpu.VMEM((D_MODEL, EXPERT_HIDDEN), F32),
                pltpu.VMEM((D_MODEL, EXPERT_HIDDEN), F32),
                pltpu.VMEM((EXPERT_HIDDEN, D_MODEL), F32),
                pltpu.VMEM((D_MODEL, EXPERT_HIDDEN), BF16),
                pltpu.VMEM((D_MODEL, EXPERT_HIDDEN), BF16),
                pltpu.VMEM((EXPERT_HIDDEN, D_MODEL), BF16),
                pltpu.SemaphoreType.DMA,
                pltpu.VMEM((MOE_M * ROW_TILE, LANES), F32),
                pltpu.VMEM((MOE_M * ROW_TILE, LANES), F32),
                pltpu.VMEM((MOE_M * ROW_TILE, LANES), F32),
                pltpu.SemaphoreType.DMA((XS_SLOTS,)),
            ],
        ),
        compiler_params=pltpu.CompilerParams(
            dimension_semantics=("arbitrary",), vmem_limit_bytes=VMEM_LIMIT),
        name="experts",
    )(block_e, next_e, block_valid, n_used, xs, w_gate, w_up, w_down)


def _combine_kernel(final_norm, pos0_ref, pos1_ref, nxt0_ref, nxt1_ref, h2_ref, wt_ref, g_ref, ys_ref,
                    o_ref, b00_ref, b01_ref, b10_ref, b11_ref, sem):
    i = pl.program_id(0)
    n_steps = pl.num_programs(0)
    tc = h2_ref.shape[0]
    bufs = ((b00_ref, b01_ref), (b10_ref, b11_ref))

    def issue(rows_refs, s):
        for r in range(tc):
            for k in range(TOP_K):
                _row_copy(ys_ref, rows_refs[k][0, 0, r], bufs[s][k], r, sem.at[s]).start(priority=k)

    @pl.when(i == 0)
    def _():
        issue((pos0_ref, pos1_ref), 0)

    for s in range(2):
        @pl.when(i % 2 == s)
        def _():
            @pl.when(i + 1 < n_steps)
            def _():
                issue((nxt0_ref, nxt1_ref), 1 - s)

            for k in range(TOP_K):
                pltpu.make_async_copy(ys_ref.at[pl.ds(0, tc * ROW_TILE)], bufs[s][k], sem.at[s]).wait()
            y = (h2_ref[...] + wt_ref[:, 0:1] * _tiles_to_rows(bufs[s][0], tc)
                 + wt_ref[:, 1:2] * _tiles_to_rows(bufs[s][1], tc))
            o_ref[...] = _rms(y, g_ref[...]) if final_norm else y


def _combine(pos3, h2, wt, g, ys, final_norm):
    t = h2.shape[0]
    tc = COMB_T
    nt = t // tc
    return pl.pallas_call(
        functools.partial(_combine_kernel, final_norm),
        out_shape=jax.ShapeDtypeStruct((t, D_MODEL), F32),
        grid=(nt,),
        in_specs=[
            pl.BlockSpec((1, 1, tc), lambda i: (i, 0, 0), memory_space=pltpu.SMEM),
            pl.BlockSpec((1, 1, tc), lambda i: (nt + i, 0, 0), memory_space=pltpu.SMEM),
            pl.BlockSpec((1, 1, tc), lambda i: (jnp.minimum(i + 1, nt - 1), 0, 0), memory_space=pltpu.SMEM),
            pl.BlockSpec((1, 1, tc), lambda i: (nt + jnp.minimum(i + 1, nt - 1), 0, 0),
                         memory_space=pltpu.SMEM),
            pl.BlockSpec((tc, D_MODEL), lambda i: (i, 0)),
            pl.BlockSpec((tc, TOP_K), lambda i: (i, 0)),
            pl.BlockSpec((1, D_MODEL), lambda i: (0, 0)),
            pl.BlockSpec(memory_space=pl.ANY),
        ],
        out_specs=pl.BlockSpec((tc, D_MODEL), lambda i: (i, 0)),
        scratch_shapes=[
            pltpu.VMEM((tc * ROW_TILE, LANES), F32),
            pltpu.VMEM((tc * ROW_TILE, LANES), F32),
            pltpu.VMEM((tc * ROW_TILE, LANES), F32),
            pltpu.VMEM((tc * ROW_TILE, LANES), F32),
            pltpu.SemaphoreType.DMA((2,)),
        ],
        compiler_params=pltpu.CompilerParams(
            dimension_semantics=("arbitrary",), vmem_limit_bytes=VMEM_LIMIT),
        name="combine",
    )(pos3, pos3, pos3, pos3, h2, wt, g, ys)


def _routing_tables(rid, n_tok):
    e_flat = rid[:TOP_K].reshape(-1)
    onehot = (e_flat[None, :] == jnp.arange(N_EXPERTS, dtype=jnp.int32)[:, None]).astype(jnp.int32)
    csum = jnp.cumsum(onehot, axis=1)
    counts = csum[:, -1]
    rank = jnp.sum(onehot * csum, axis=0) - 1
    padded = (counts + MOE_M - 1) // MOE_M * MOE_M
    pend = jnp.cumsum(padded)
    pstart = pend - padded
    pos = jnp.sum(onehot * pstart[:, None], axis=0) + rank
    n_blocks = (TOP_K * n_tok + N_EXPERTS * (MOE_M - 1) + MOE_M - 1) // MOE_M
    n_used = pend[-1] // MOE_M
    blk = jnp.minimum(jnp.arange(n_blocks, dtype=jnp.int32), n_used - 1)
    block_e = jnp.sum((pend[None, :] <= (blk * MOE_M)[:, None]).astype(jnp.int32), axis=1)
    block_e = jnp.minimum(block_e, N_EXPERTS - 1)
    last_blk = jnp.maximum(pend - MOE_M, 0).astype(jnp.int32)
    has = (counts > 0).astype(jnp.int32)
    ids = jnp.arange(N_EXPERTS, dtype=jnp.int32)
    later = (ids[None, :] > ids[:, None]) & (counts[None, :] > 0)
    nxt = jnp.min(jnp.where(later, ids[None, :], N_EXPERTS), axis=1)
    nxt = jnp.where(nxt == N_EXPERTS, ids, nxt)
    next_e = jnp.sum(jnp.where(block_e[:, None] == ids[None, :], nxt[None, :], 0), axis=1)
    seg_end = jnp.sum(jnp.where(block_e[:, None] == ids[None, :], (pstart + counts)[None, :], 0), axis=1)
    block_valid = jnp.clip(seg_end - blk * MOE_M, 0, MOE_M)
    return (pos.reshape(TOP_K, n_tok).astype(jnp.int32), block_e.astype(jnp.int32),
            next_e.astype(jnp.int32), block_valid.astype(jnp.int32), n_used.reshape(1).astype(jnp.int32),
            last_blk, has, n_blocks * MOE_M)


def _tile_pos(pos, tile):
    k, t = pos.shape
    return pos.reshape(k * (t // tile), 1, tile)


def kernel(x, mem, mix_norm_g, w_in, conv_w, w_conv_out, w_ret_out, w_mix_out, xa_norm_g, mem_norm_g,
           w_xa_q, w_xa_kv, w_xa_o, moe_norm_g, w_group, b_group, w_router, b_router, w_gate, w_up,
           w_down, final_norm_g):
    batch, seq, d = x.shape
    depth = w_in.shape[0]
    t = batch * seq
    h = x.reshape(t, d)
    for l in range(depth):
        proj = _in_proj(h, mix_norm_g[l][None], w_in[l])
        h1 = _mixer(proj, h, conv_w[l], w_conv_out[l].astype(BF16), w_ret_out[l].astype(BF16),
                    w_mix_out[l].astype(BF16), batch, seq)
        kv = _mem_kv(mem, mem_norm_g[l][None], w_xa_kv[l])

        rw = jnp.zeros((LANES, d), F32)
        rw = rw.at[0:N_GROUPS].set(w_group[l].T).at[ROUTE_E0:ROUTE_E0 + N_EXPERTS].set(w_router[l].T)
        rwh = rw.astype(BF16)
        rwl = (rw - rwh.astype(F32)).astype(BF16)
        rw_cat = jnp.concatenate([rwh.T, rwl.T], axis=1)
        rb = jnp.zeros((ROUTE_ROWS,), F32)
        rb = rb.at[0:N_GROUPS].set(b_group[l]).at[ROUTE_E0:ROUTE_E0 + N_EXPERTS].set(b_router[l])
        rb = jnp.broadcast_to(rb[:, None], (ROUTE_ROWS, LANES))

        h2, xn2, rid, rwt = _xattn(h1, kv, xa_norm_g[l][None], w_xa_q[l].astype(BF16),
                                   w_xa_o[l].astype(BF16), moe_norm_g[l][None], rw_cat, rb, batch, seq)

        pos, block_e, next_e, block_valid, n_used, last_blk, has, n_rows = _routing_tables(rid, t)
        xs = _dispatch(last_blk, has, n_used, _tile_pos(pos, DISP_T), xn2, n_rows)
        ys = _experts(block_e, next_e, block_valid, n_used, xs, w_gate[l], w_up[l], w_down[l])
        h = _combine(_tile_pos(pos, COMB_T), h2, rwt[:TOP_K].T, final_norm_g[None], ys,
                     final_norm=(l == depth - 1))
    return h.reshape(batch, seq, d)
```

```python
import functools

import numpy as np
import jax
import jax.numpy as jnp
from jax import lax
from jax.experimental import pallas as pl
from jax.experimental.pallas import tpu as pltpu

F32 = jnp.float32
BF16 = jnp.bfloat16

D_MODEL = 1024
CONV_WIDTH = 512
CONV_K = 3
RET_HEADS = 8
RET_DK = 64
RET_DV = 128
RET_CHUNK = 128
QK_WIDTH = RET_HEADS * RET_DK
V_WIDTH = RET_HEADS * RET_DV
ROPE_BASE = 10000.0
IN_WIDTH = 3 * CONV_WIDTH + 2 * QK_WIDTH + 2 * V_WIDTH + 2 * D_MODEL
OFF_XIN = 0
OFF_BG = OFF_XIN + CONV_WIDTH
OFF_CG = OFF_BG + CONV_WIDTH
OFF_Q = OFF_CG + CONV_WIDTH
OFF_K = OFF_Q + QK_WIDTH
OFF_V = OFF_K + QK_WIDTH
OFF_G = OFF_V + V_WIDTH
OFF_GATE_C = OFF_G + V_WIDTH
OFF_GATE_R = OFF_GATE_C + D_MODEL
MEM_LEN = 256
XA_HEADS = 4
XA_HEAD_DIM = D_MODEL // XA_HEADS
N_GROUPS = 4
EXPERTS_PER_GROUP = 8
N_EXPERTS = N_GROUPS * EXPERTS_PER_GROUP
TOP_K = 2
EXPERT_HIDDEN = D_MODEL // 2
EPS = 1e-6

LANES = 128
SUBLANES = 8
VMEM_LIMIT = 56 * 1024 * 1024

IN_TM = 2048
IN_TN = 1664
MIX_TS = 512
MIX_OUT_ROWS = 256
MIX_STAGE_CHUNKS = 4
XA_TS = 1024
ROUTE_ROWS = 40
ROUTE_E0 = 8
MOE_M = 512
DISP_T = 2048
COMB_T = 256
XS_SLOTS = 3


def _rms(x, g):
    ms = jnp.mean(x * x, axis=-1, keepdims=True)
    return x * lax.rsqrt(ms + EPS) * g


def _sigmoid(x):
    return 1.0 / (1.0 + jnp.exp(-x))


def _dot(a, b):
    return jnp.dot(a, b, preferred_element_type=F32)


def _dot_nt(a, b):
    return lax.dot_general(a, b, (((1,), (1,)), ((), ())), preferred_element_type=F32)


def _dot_tn(a, b):
    return lax.dot_general(a, b, (((0,), (0,)), ((), ())), preferred_element_type=F32)


ROW_TILE = D_MODEL // LANES


def _rows_to_tiles(ref, x):
    n = x.shape[0]
    for c in range(ROW_TILE):
        ref[pl.ds(c, n, stride=ROW_TILE), :] = x[:, c * LANES:(c + 1) * LANES]


def _tiles_to_rows(ref, n):
    return jnp.concatenate([ref[pl.ds(c, n, stride=ROW_TILE), :] for c in range(ROW_TILE)], axis=1)


def _inproj_kernel(x_ref, g_ref, w_ref, o_ref, xn_ref):
    @pl.when(pl.program_id(1) == 0)
    def _():
        xn_ref[...] = _rms(x_ref[...], g_ref[...]).astype(BF16)

    o_ref[...] = _dot(xn_ref[...], w_ref[...].astype(BF16)).astype(BF16)


def _in_proj(x2, g, w_in):
    t = x2.shape[0]
    return pl.pallas_call(
        _inproj_kernel,
        out_shape=jax.ShapeDtypeStruct((t, IN_WIDTH), BF16),
        grid=(t // IN_TM, IN_WIDTH // IN_TN),
        in_specs=[
            pl.BlockSpec((IN_TM, D_MODEL), lambda i, j: (i, 0)),
            pl.BlockSpec((1, D_MODEL), lambda i, j: (0, 0)),
            pl.BlockSpec((D_MODEL, IN_TN), lambda i, j: (0, j)),
        ],
        out_specs=pl.BlockSpec((IN_TM, IN_TN), lambda i, j: (i, j)),
        scratch_shapes=[pltpu.VMEM((IN_TM, D_MODEL), BF16)],
        compiler_params=pltpu.CompilerParams(
            dimension_semantics=("arbitrary", "arbitrary"), vmem_limit_bytes=VMEM_LIMIT),
        name="in_proj",
    )(x2, g, w_in)


def _retention_constants(seq):
    pos = np.arange(seq, dtype=np.float64)
    inv_freq = ROPE_BASE ** (-np.arange(0, RET_DK, 2, dtype=np.float64) / RET_DK)
    ang = pos[:, None] * inv_freq[None, :]
    cos, sin = np.cos(ang), np.sin(ang)
    cos_t = np.concatenate([cos, cos, cos, cos], axis=1)
    sin_t = np.concatenate([-sin, sin, -sin, sin], axis=1)
    log_g = np.log(1.0 - 2.0 ** (-5.0 - np.arange(RET_HEADS, dtype=np.float64)))
    idx = np.arange(RET_CHUNK, dtype=np.float64)
    diff = idx[:, None] - idx[None, :]
    decay = np.where(diff >= 0, np.exp(np.maximum(diff, 0.0)[None] * log_g[:, None, None]), 0.0)
    zeta = np.exp((RET_CHUNK - 1 - idx)[None, :] * log_g[:, None])
    xi = np.exp((idx + 1)[None, :] * log_g[:, None])
    zeta_t = np.repeat(zeta.T, RET_DK, axis=1)
    xi_t = np.repeat(xi.T, RET_DK, axis=1)
    chunk_decay = np.exp(RET_CHUNK * log_g)
    f = lambda a: jnp.asarray(a, dtype=F32)
    return f(cos_t), f(sin_t), f(xi_t), f(zeta_t), f(decay), [float(c) for c in chunk_decay]


def _mixer_kernel(chunk_decay, proj_ref, x_ref, cos_ref, sin_ref, xi_ref, zeta_ref, decay_ref,
                  convw_ref, wc_ref, wr_ref, wm_ref, o_ref, state_ref, tail_ref, yin_ref):
    ts = x_ref.shape[0]

    @pl.when(pl.program_id(1) == 0)
    def _():
        state_ref[...] = jnp.zeros_like(state_ref)
        tail_ref[...] = jnp.zeros_like(tail_ref)

    xin = proj_ref[:, OFF_XIN:OFF_XIN + CONV_WIDTH].astype(F32)
    bg = proj_ref[:, OFF_BG:OFF_BG + CONV_WIDTH].astype(F32)
    cg = proj_ref[:, OFF_CG:OFF_CG + CONV_WIDTH].astype(F32)
    u = cg * xin
    ue = jnp.concatenate([tail_ref[...], u], axis=0)
    u1 = pltpu.roll(ue, 1, 0)[SUBLANES:]
    u2 = pltpu.roll(ue, 2, 0)[SUBLANES:]
    tail_ref[...] = u[ts - SUBLANES:]
    c = convw_ref[2:3, :] * u + convw_ref[1:2, :] * u1 + convw_ref[0:1, :] * u2
    y_conv = _dot((bg * c).astype(BF16), wc_ref[...])

    lane = lax.broadcasted_iota(jnp.int32, (1, LANES), 1)
    low_half = (lane % RET_DK) < (RET_DK // 2)
    head_masks = [(lane // RET_DK) == j for j in range(LANES // RET_DK)]

    def rotary(t, cosv, sinv):
        outs = []
        for p in range(QK_WIDTH // LANES):
            tp = t[:, p * LANES:(p + 1) * LANES]
            fwd = pltpu.roll(tp, LANES - RET_DK // 2, 1)
            bwd = pltpu.roll(tp, RET_DK // 2, 1)
            outs.append(tp * cosv + jnp.where(low_half, fwd, bwd) * sinv)
        return jnp.concatenate(outs, axis=1)

    n_chunks = ts // RET_CHUNK
    chunk_rows = [slice(ci * RET_CHUNK, (ci + 1) * RET_CHUNK) for ci in range(n_chunks)]
    zero = jnp.zeros((), BF16)
    qm, qxm, kb, kz = {}, {}, {}, {}
    for ci, rows in enumerate(chunk_rows):
        cosv = cos_ref[rows, :]
        sinv = sin_ref[rows, :]
        qr = rotary(proj_ref[rows, OFF_Q:OFF_Q + QK_WIDTH].astype(F32), cosv, sinv)
        kr = rotary(proj_ref[rows, OFF_K:OFF_K + QK_WIDTH].astype(F32), cosv, sinv) * (RET_DK ** -0.5)
        qb = qr.astype(BF16)
        qx = (qr * xi_ref[...]).astype(BF16)
        kb[ci] = kr.astype(BF16)
        kz[ci] = (kr * zeta_ref[...]).astype(BF16)
        for h in range(RET_HEADS):
            lanes = slice((h // 2) * LANES, (h // 2 + 1) * LANES)
            qm[ci, h] = jnp.where(head_masks[h % 2], qb[:, lanes], zero)
            qxm[ci, h] = jnp.where(head_masks[h % 2], qx[:, lanes], zero)

    def pair_lanes(h):
        return slice((h // 2) * LANES, (h // 2 + 1) * LANES)

    def v_of(ci, h):
        return proj_ref[chunk_rows[ci], OFF_V + h * RET_DV:OFF_V + (h + 1) * RET_DV]

    state = [state_ref[h] for h in range(RET_HEADS)]
    for c0 in range(0, n_chunks, MIX_STAGE_CHUNKS):
        pairs = [(ci, h) for ci in range(c0, c0 + MIX_STAGE_CHUNKS) for h in range(RET_HEADS)]
        scores = {(ci, h): _dot_nt(qm[ci, h], kb[ci][:, pair_lanes(h)]) for ci, h in pairs}
        kv = {(ci, h): _dot_tn(kz[ci][:, pair_lanes(h)], v_of(ci, h)) for ci, h in pairs}
        probs = {(ci, h): (scores[ci, h] * decay_ref[h]).astype(BF16) for ci, h in pairs}
        inner = {(ci, h): _dot(probs[ci, h], v_of(ci, h)) for ci, h in pairs}
        st_before = {}
        for ci, h in pairs:
            st_before[ci, h] = state[h].astype(BF16)
            state[h] = chunk_decay[h] * state[h] + kv[ci, h]
        o = {(ci, h): inner[ci, h] + _dot(qxm[ci, h], st_before[ci, h]) for ci, h in pairs}
        mu = {p: jnp.mean(o[p], axis=-1, keepdims=True) for p in pairs}
        dlt = {p: o[p] - mu[p] for p in pairs}
        var = {p: jnp.mean(dlt[p] * dlt[p], axis=-1, keepdims=True) for p in pairs}
        for ci, h in pairs:
            g = proj_ref[chunk_rows[ci], OFF_G + h * RET_DV:OFF_G + (h + 1) * RET_DV].astype(F32)
            yin_ref[chunk_rows[ci], h * RET_DV:(h + 1) * RET_DV] = (
                g * _sigmoid(g) * (dlt[ci, h] * lax.rsqrt(var[ci, h] + EPS))).astype(BF16)
    for h in range(RET_HEADS):
        state_ref[h] = state[h]

    for r0 in range(0, ts, MIX_OUT_ROWS):
        rows = slice(r0, r0 + MIX_OUT_ROWS)
        y_ret = _dot(yin_ref[rows, :], wr_ref[...])
        gate_c = proj_ref[rows, OFF_GATE_C:OFF_GATE_C + D_MODEL].astype(F32)
        gate_r = proj_ref[rows, OFF_GATE_R:OFF_GATE_R + D_MODEL].astype(F32)
        merged = _sigmoid(gate_c) * y_conv[rows] + _sigmoid(gate_r) * y_ret
        o_ref[rows, :] = x_ref[rows, :] + _dot(merged.astype(BF16), wm_ref[...])


def _mixer(proj, x2, conv_w, wc_bf, wr_bf, wm_bf, batch, seq):
    ts = MIX_TS
    ns = seq // ts
    cos_t, sin_t, xi_t, zeta_t, decay, chunk_decay = _retention_constants(seq)
    full = lambda shape: pl.BlockSpec(shape, lambda b, s: (0,) * len(shape))
    return pl.pallas_call(
        functools.partial(_mixer_kernel, chunk_decay),
        out_shape=jax.ShapeDtypeStruct((batch * seq, D_MODEL), F32),
        grid=(batch, ns),
        in_specs=[
            pl.BlockSpec((ts, IN_WIDTH), lambda b, s: (b * ns + s, 0)),
            pl.BlockSpec((ts, D_MODEL), lambda b, s: (b * ns + s, 0)),
            pl.BlockSpec((ts, LANES), lambda b, s: (s, 0)),
            pl.BlockSpec((ts, LANES), lambda b, s: (s, 0)),
            full((RET_CHUNK, QK_WIDTH)),
            full((RET_CHUNK, QK_WIDTH)),
            full((RET_HEADS, RET_CHUNK, RET_CHUNK)),
            full((CONV_K, CONV_WIDTH)),
            full((CONV_WIDTH, D_MODEL)),
            full((V_WIDTH, D_MODEL)),
            full((D_MODEL, D_MODEL)),
        ],
        out_specs=pl.BlockSpec((ts, D_MODEL), lambda b, s: (b * ns + s, 0)),
        scratch_shapes=[
            pltpu.VMEM((RET_HEADS, LANES, RET_DV), F32),
            pltpu.VMEM((SUBLANES, CONV_WIDTH), F32),
            pltpu.VMEM((ts, V_WIDTH), BF16),
        ],
        compiler_params=pltpu.CompilerParams(
            dimension_semantics=("arbitrary", "arbitrary"), vmem_limit_bytes=VMEM_LIMIT),
        name="mixer",
    )(proj, x2, cos_t, sin_t, xi_t, zeta_t, decay, conv_w, wc_bf, wr_bf, wm_bf)


def _memkv_kernel(m_ref, g_ref, w_ref, o_ref):
    o_ref[0] = _dot(_rms(m_ref[0], g_ref[...]).astype(BF16), w_ref[...].astype(BF16)).astype(BF16)


def _mem_kv(mem, g, w_kv):
    b = mem.shape[0]
    return pl.pallas_call(
        _memkv_kernel,
        out_shape=jax.ShapeDtypeStruct((b, MEM_LEN, 2 * D_MODEL), BF16),
        grid=(b,),
        in_specs=[
            pl.BlockSpec((1, MEM_LEN, D_MODEL), lambda i: (i, 0, 0)),
            pl.BlockSpec((1, D_MODEL), lambda i: (0, 0)),
            pl.BlockSpec((D_MODEL, 2 * D_MODEL), lambda i: (0, 0)),
        ],
        out_specs=pl.BlockSpec((1, MEM_LEN, 2 * D_MODEL), lambda i: (i, 0, 0)),
        compiler_params=pltpu.CompilerParams(
            dimension_semantics=("arbitrary",), vmem_limit_bytes=VMEM_LIMIT),
        name="mem_kv",
    )(mem, g, w_kv)


def _xattn_kernel(h_ref, kv_ref, gx_ref, wq_ref, wo_ref, gm_ref, rw_ref, rb_ref,
                  h2_ref, xn2_ref, rid_ref, rwt_ref):
    ts = h_ref.shape[0]
    h = h_ref[...]
    q = _dot(_rms(h, gx_ref[...]).astype(BF16), wq_ref[...])
    qb = q.astype(BF16)
    head_cols = [slice(hd * XA_HEAD_DIM, (hd + 1) * XA_HEAD_DIM) for hd in range(XA_HEADS)]
    scores = [_dot_nt(qb[:, cols], kv_ref[0, :, cols]) * (XA_HEAD_DIM ** -0.5) for cols in head_cols]
    probs = []
    for s in scores:
        e = jnp.exp(s - jnp.max(s, axis=-1, keepdims=True))
        probs.append((e / jnp.sum(e, axis=-1, keepdims=True)).astype(BF16))
    outs = [_dot(p, kv_ref[0, :, D_MODEL + hd * XA_HEAD_DIM:D_MODEL + (hd + 1) * XA_HEAD_DIM])
            for hd, p in enumerate(probs)]
    h2 = h + _dot(jnp.concatenate(outs, axis=1).astype(BF16), wo_ref[...])
    h2_ref[...] = h2
    xn = _rms(h2, gm_ref[...])
    _rows_to_tiles(xn2_ref, xn)

    hi = xn.astype(BF16)
    lo = (xn - hi.astype(F32)).astype(BF16)
    hi_prod = _dot(hi, rw_ref[...])
    lo_prod = _dot(lo, rw_ref[:, 0:LANES])
    lg_t = (hi_prod[:, 0:LANES] + lo_prod) + hi_prod[:, LANES:2 * LANES]
    lg = lg_t.T[0:ROUTE_ROWS] + rb_ref[:, 0:1]

    gl = lg[0:N_GROUPS]
    ge = jnp.exp(gl - jnp.max(gl, axis=0, keepdims=True))
    gp = ge / jnp.sum(ge, axis=0, keepdims=True)
    p_g = gp[0:1]
    g_idx = jnp.zeros((1, ts), jnp.int32)
    for i in range(1, N_GROUPS):
        better = gp[i:i + 1] > p_g
        g_idx = jnp.where(better, i, g_idx)
        p_g = jnp.where(better, gp[i:i + 1], p_g)
    sel = jnp.zeros((EXPERTS_PER_GROUP, ts), F32)
    for i in range(N_GROUPS):
        r0 = ROUTE_E0 + i * EXPERTS_PER_GROUP
        sel = jnp.where(g_idx == i, lg[r0:r0 + EXPERTS_PER_GROUP], sel)
    se = jnp.exp(sel - jnp.max(sel, axis=0, keepdims=True))
    sp = se / jnp.sum(se, axis=0, keepdims=True)
    ridx = lax.broadcasted_iota(jnp.int32, (EXPERTS_PER_GROUP, ts), 0)
    m1 = jnp.max(sp, axis=0, keepdims=True)
    i1 = jnp.min(jnp.where(sp == m1, ridx, EXPERTS_PER_GROUP), axis=0, keepdims=True)
    sp2 = jnp.where(ridx == i1, -1.0, sp)
    m2 = jnp.max(sp2, axis=0, keepdims=True)
    i2 = jnp.min(jnp.where(sp2 == m2, ridx, EXPERTS_PER_GROUP), axis=0, keepdims=True)
    den = m1 + m2
    rid_ref[...] = jnp.zeros_like(rid_ref)
    rid_ref[0:1, :] = g_idx * EXPERTS_PER_GROUP + i1
    rid_ref[1:2, :] = g_idx * EXPERTS_PER_GROUP + i2
    w_rows = jnp.concatenate([p_g * m1 / den, p_g * m2 / den, jnp.zeros((LANES - TOP_K, ts), F32)], axis=0)
    rwt_ref[...] = w_rows.T


def _xattn(h1, kv, gx, wq_bf, wo_bf, gm, rw_cat, rb, batch, seq):
    ts = XA_TS
    ns = seq // ts
    t = batch * seq
    full = lambda shape: pl.BlockSpec(shape, lambda b, s: (0,) * len(shape))
    return pl.pallas_call(
        _xattn_kernel,
        out_shape=(
            jax.ShapeDtypeStruct((t, D_MODEL), F32),
            jax.ShapeDtypeStruct((t * ROW_TILE, LANES), F32),
            jax.ShapeDtypeStruct((SUBLANES, t), jnp.int32),
            jax.ShapeDtypeStruct((t, LANES), F32),
        ),
        grid=(batch, ns),
        in_specs=[
            pl.BlockSpec((ts, D_MODEL), lambda b, s: (b * ns + s, 0)),
            pl.BlockSpec((1, MEM_LEN, 2 * D_MODEL), lambda b, s: (b, 0, 0)),
            full((1, D_MODEL)),
            full((D_MODEL, D_MODEL)),
            full((D_MODEL, D_MODEL)),
            full((1, D_MODEL)),
            full((D_MODEL, 2 * LANES)),
            full((ROUTE_ROWS, LANES)),
        ],
        out_specs=(
            pl.BlockSpec((ts, D_MODEL), lambda b, s: (b * ns + s, 0)),
            pl.BlockSpec((ts * ROW_TILE, LANES), lambda b, s: (b * ns + s, 0)),
            pl.BlockSpec((SUBLANES, ts), lambda b, s: (0, b * ns + s)),
            pl.BlockSpec((ts, LANES), lambda b, s: (b * ns + s, 0)),
        ),
        compiler_params=pltpu.CompilerParams(
            dimension_semantics=("arbitrary", "arbitrary"), vmem_limit_bytes=VMEM_LIMIT),
        name="xattn",
    )(h1, kv, gx, wq_bf, wo_bf, gm, rw_cat, rb)


def _tile_of(ref, row):
    start = row * ROW_TILE
    if not isinstance(start, int):
        start = pl.multiple_of(start, ROW_TILE)
    return ref.at[pl.ds(start, ROW_TILE)]


def _row_copy(src_ref, src_row, dst_ref, dst_row, sem):
    return pltpu.make_async_copy(_tile_of(src_ref, src_row), _tile_of(dst_ref, dst_row), sem)


def _dispatch_kernel(last_ref, has_ref, nu_ref, pos0_ref, pos1_ref, xn_ref, xs_ref, zero_ref, sem, zsem):
    pos_refs = (pos0_ref, pos1_ref)
    td = xn_ref.shape[0] // ROW_TILE
    blk = MOE_M * ROW_TILE
    n_blocks = xs_ref.shape[0] // blk

    def fill(start):
        start = pl.multiple_of(start * ROW_TILE, blk)
        return pltpu.make_async_copy(zero_ref, xs_ref.at[pl.ds(start, blk)], zsem)

    @pl.when(pl.program_id(0) == 0)
    def _():
        zero_ref[...] = jnp.zeros_like(zero_ref)

        for e in range(N_EXPERTS):
            @pl.when(has_ref[e] > 0)
            def _():
                fill(last_ref[e]).start()
        for e in range(N_EXPERTS):
            @pl.when(has_ref[e] > 0)
            def _():
                fill(last_ref[e]).wait()

        def start_tail(b, carry):
            fill(b * MOE_M).start()
            return carry

        lax.fori_loop(nu_ref[0], n_blocks, start_tail, 0)

    for r in range(td):
        for k in range(TOP_K):
            _row_copy(xn_ref, r, xs_ref, pos_refs[k][0, 0, r], sem).start(priority=k)
    for k in range(TOP_K):
        pltpu.make_async_copy(xn_ref, xs_ref.at[pl.ds(0, td * ROW_TILE)], sem).wait()

    @pl.when(pl.program_id(0) == pl.num_programs(0) - 1)
    def _():
        def wait_tail(b, carry):
            fill(b * MOE_M).wait()
            return carry

        lax.fori_loop(nu_ref[0], n_blocks, wait_tail, 0)


def _dispatch(last_blk, has, n_used, pos3, xn2, n_rows):
    t = xn2.shape[0] // ROW_TILE
    td = DISP_T
    return pl.pallas_call(
        _dispatch_kernel,
        out_shape=jax.ShapeDtypeStruct((n_rows * ROW_TILE, LANES), F32),
        grid_spec=pltpu.PrefetchScalarGridSpec(
            num_scalar_prefetch=3,
            grid=(t // td,),
            in_specs=[
                pl.BlockSpec((1, 1, td), lambda i, *_: (i, 0, 0), memory_space=pltpu.SMEM),
                pl.BlockSpec((1, 1, td), lambda i, *_: (t // td + i, 0, 0), memory_space=pltpu.SMEM),
                pl.BlockSpec((td * ROW_TILE, LANES), lambda i, *_: (i, 0)),
            ],
            out_specs=pl.BlockSpec(memory_space=pl.ANY),
            scratch_shapes=[
                pltpu.VMEM((MOE_M * ROW_TILE, LANES), F32),
                pltpu.SemaphoreType.DMA,
                pltpu.SemaphoreType.DMA,
            ],
        ),
        compiler_params=pltpu.CompilerParams(
            dimension_semantics=("arbitrary",), vmem_limit_bytes=VMEM_LIMIT),
        name="dispatch",
    )(last_blk, has, n_used, pos3, pos3, xn2)


def _expert_kernel(be_ref, ne_ref, bv_ref, nu_ref, xs_ref, wg_ref, wu_ref, wd_ref, ys_ref,
                   wgf, wuf, wdf, wgb, wub, wdb, wsem, xb0, xb1, xb2, xsem):
    xbufs = (xb0, xb1, xb2)
    assert len(xbufs) == XS_SLOTS
    i = pl.program_id(0)
    e = be_ref[i]

    def fetch(expert):
        return [pltpu.make_async_copy(src.at[expert], dst, wsem)
                for src, dst in ((wg_ref, wgf), (wu_ref, wuf), (wd_ref, wdf))]

    @pl.when(i == 0)
    def _():
        for c in fetch(e):
            c.start()

    @pl.when((i == 0) | (e != be_ref[jnp.maximum(i - 1, 0)]))
    def _():
        for c in fetch(e):
            c.wait()
        wgb[...] = wgf[...].astype(BF16)
        wub[...] = wuf[...].astype(BF16)
        wdb[...] = wdf[...].astype(BF16)

        @pl.when(ne_ref[i] != e)
        def _():
            for c in fetch(ne_ref[i]):
                c.start()

    nu = nu_ref[0]
    blk = MOE_M * ROW_TILE

    def rows_in(b, s):
        start = b * blk if isinstance(b, int) else pl.multiple_of(b * blk, blk)
        return pltpu.make_async_copy(xs_ref.at[pl.ds(start, blk)], xbufs[s], xsem.at[s])

    @pl.when(i == 0)
    def _():
        for b in range(XS_SLOTS - 1):
            @pl.when(b < nu)
            def _():
                rows_in(b, b).start()

    for s in range(XS_SLOTS):
        @pl.when((i < nu) & (lax.rem(i, XS_SLOTS) == s))
        def _():
            @pl.when(i + (XS_SLOTS - 1) < nu)
            def _():
                rows_in(i + (XS_SLOTS - 1), (s + XS_SLOTS - 1) % XS_SLOTS).start()

            rows_in(i, s).wait()

            def mlp(n_rows):
                x = _tiles_to_rows(xbufs[s], n_rows).astype(BF16)
                g = _dot(x, wgb[...])
                u = _dot(x, wub[...])
                _rows_to_tiles(ys_ref, _dot((g * _sigmoid(g) * u).astype(BF16), wdb[...]))

            half = MOE_M // 2

            @pl.when(bv_ref[i] > half)
            def _():
                mlp(MOE_M)

            @pl.when(bv_ref[i] <= half)
            def _():
                mlp(half)
                ys_ref[pl.ds(half * ROW_TILE, half * ROW_TILE), :] = jnp.zeros(
                    (half * ROW_TILE, LANES), F32)

    @pl.when(i >= nu)
    def _():
        ys_ref[...] = jnp.zeros_like(ys_ref)


def _experts(block_e, next_e, block_valid, n_used, xs, w_gate, w_up, w_down):
    n_rows = xs.shape[0] // ROW_TILE
    out_map = lambda i, be, ne, bv, nu: (i, 0)
    return pl.pallas_call(
        _expert_kernel,
        out_shape=jax.ShapeDtypeStruct((n_rows * ROW_TILE, LANES), F32),
        grid_spec=pltpu.PrefetchScalarGridSpec(
            num_scalar_prefetch=4,
            grid=(n_rows // MOE_M,),
            in_specs=[
                pl.BlockSpec(memory_space=pl.ANY),
                pl.BlockSpec(memory_space=pl.ANY),
                pl.BlockSpec(memory_space=pl.ANY),
                pl.BlockSpec(memory_space=pl.ANY),
            ],
            out_specs=pl.BlockSpec((MOE_M * ROW_TILE, LANES), out_map),
            scratch_shapes=[
                pltpu.VMEM((D_MODEL, EXPERT_HIDDEN), F32),
                pltpu.VMEM((D_MODEL, EXPERT_HIDDEN), F32),
                pltpu.VMEM((EXPERT_HIDDEN, D_MODEL), F32),
                pltpu.VMEM((D_MODEL, EXPERT_HIDDEN), BF16),
                pltpu.VMEM((D_MODEL, EXPERT_HIDDEN), BF16),
                pltpu.VMEM((EXPERT_HIDDEN, D_MODEL), BF16),
                pltpu.SemaphoreType.DMA,
                pltpu.VMEM((MOE_M * ROW_TILE, LANES), F32),
                pltpu.VMEM((MOE_M * ROW_TILE, LANES), F32),
                pltpu.VMEM((MOE_M * ROW_TILE, LANES), F32),
                pltpu.SemaphoreType.DMA((XS_SLOTS,)),
            ],
        ),
        compiler_params=pltpu.CompilerParams(
            dimension_semantics=("arbitrary",), vmem_limit_bytes=VMEM_LIMIT),
        name="experts",
    )(block_e, next_e, block_valid, n_used, xs, w_gate, w_up, w_down)


def _combine_kernel(final_norm, pos0_ref, pos1_ref, nxt0_ref, nxt1_ref, h2_ref, wt_ref, g_ref, ys_ref,
                    o_ref, b00_ref, b01_ref, b10_ref, b11_ref, sem):
    i = pl.program_id(0)
    n_steps = pl.num_programs(0)
    tc = h2_ref.shape[0]
    bufs = ((b00_ref, b01_ref), (b10_ref, b11_ref))

    def issue(rows_refs, s):
        for r in range(tc):
            for k in range(TOP_K):
                _row_copy(ys_ref, rows_refs[k][0, 0, r], bufs[s][k], r, sem.at[s]).start(priority=k)

    @pl.when(i == 0)
    def _():
        issue((pos0_ref, pos1_ref), 0)

    for s in range(2):
        @pl.when(i % 2 == s)
        def _():
            @pl.when(i + 1 < n_steps)
            def _():
                issue((nxt0_ref, nxt1_ref), 1 - s)

            for k in range(TOP_K):
                pltpu.make_async_copy(ys_ref.at[pl.ds(0, tc * ROW_TILE)], bufs[s][k], sem.at[s]).wait()
            y = (h2_ref[...] + wt_ref[:, 0:1] * _tiles_to_rows(bufs[s][0], tc)
                 + wt_ref[:, 1:2] * _tiles_to_rows(bufs[s][1], tc))
            o_ref[...] = _rms(y, g_ref[...]) if final_norm else y


def _combine(pos3, h2, wt, g, ys, final_norm):
    t = h2.shape[0]
    tc = COMB_T
    nt = t // tc
    return pl.pallas_call(
        functools.partial(_combine_kernel, final_norm),
        out_shape=jax.ShapeDtypeStruct((t, D_MODEL), F32),
        grid=(nt,),
        in_specs=[
            pl.BlockSpec((1, 1, tc), lambda i: (i, 0, 0), memory_space=pltpu.SMEM),
            pl.BlockSpec((1, 1, tc), lambda i: (nt + i, 0, 0), memory_space=pltpu.SMEM),
            pl.BlockSpec((1, 1, tc), lambda i: (jnp.minimum(i + 1, nt - 1), 0, 0), memory_space=pltpu.SMEM),
            pl.BlockSpec((1, 1, tc), lambda i: (nt + jnp.minimum(i + 1, nt - 1), 0, 0),
                         memory_space=pltpu.SMEM),
            pl.BlockSpec((tc, D_MODEL), lambda i: (i, 0)),
            pl.BlockSpec((tc, LANES), lambda i: (i, 0)),
            pl.BlockSpec((1, D_MODEL), lambda i: (0, 0)),
            pl.BlockSpec(memory_space=pl.ANY),
        ],
        out_specs=pl.BlockSpec((tc, D_MODEL), lambda i: (i, 0)),
        scratch_shapes=[
            pltpu.VMEM((tc * ROW_TILE, LANES), F32),
            pltpu.VMEM((tc * ROW_TILE, LANES), F32),
            pltpu.VMEM((tc * ROW_TILE, LANES), F32),
            pltpu.VMEM((tc * ROW_TILE, LANES), F32),
            pltpu.SemaphoreType.DMA((2,)),
        ],
        compiler_params=pltpu.CompilerParams(
            dimension_semantics=("arbitrary",), vmem_limit_bytes=VMEM_LIMIT),
        name="combine",
    )(pos3, pos3, pos3, pos3, h2, wt, g, ys)


def _routing_tables(rid, n_tok):
    e_flat = rid[:TOP_K].reshape(-1)
    onehot = (e_flat[None, :] == jnp.arange(N_EXPERTS, dtype=jnp.int32)[:, None]).astype(jnp.int32)
    csum = jnp.cumsum(onehot, axis=1)
    counts = csum[:, -1]
    rank = jnp.sum(onehot * csum, axis=0) - 1
    padded = (counts + MOE_M - 1) // MOE_M * MOE_M
    pend = jnp.cumsum(padded)
    pstart = pend - padded
    pos = jnp.sum(onehot * pstart[:, None], axis=0) + rank
    n_blocks = (TOP_K * n_tok + N_EXPERTS * (MOE_M - 1) + MOE_M - 1) // MOE_M
    n_used = pend[-1] // MOE_M
    blk = jnp.minimum(jnp.arange(n_blocks, dtype=jnp.int32), n_used - 1)
    block_e = jnp.sum((pend[None, :] <= (blk * MOE_M)[:, None]).astype(jnp.int32), axis=1)
    block_e = jnp.minimum(block_e, N_EXPERTS - 1)
    last_blk = jnp.maximum(pend - MOE_M, 0).astype(jnp.int32)
    has = (counts > 0).astype(jnp.int32)
    ids = jnp.arange(N_EXPERTS, dtype=jnp.int32)
    later = (ids[None, :] > ids[:, None]) & (counts[None, :] > 0)
    nxt = jnp.min(jnp.where(later, ids[None, :], N_EXPERTS), axis=1)
    nxt = jnp.where(nxt == N_EXPERTS, ids, nxt)
    next_e = jnp.sum(jnp.where(block_e[:, None] == ids[None, :], nxt[None, :], 0), axis=1)
    seg_end = jnp.sum(jnp.where(block_e[:, None] == ids[None, :], (pstart + counts)[None, :], 0), axis=1)
    block_valid = jnp.clip(seg_end - blk * MOE_M, 0, MOE_M)
    return (pos.reshape(TOP_K, n_tok).astype(jnp.int32), block_e.astype(jnp.int32),
            next_e.astype(jnp.int32), block_valid.astype(jnp.int32), n_used.reshape(1).astype(jnp.int32),
            last_blk, has, n_blocks * MOE_M)


def _tile_pos(pos, tile):
    k, t = pos.shape
    return pos.reshape(k * (t // tile), 1, tile)


def kernel(x, mem, mix_norm_g, w_in, conv_w, w_conv_out, w_ret_out, w_mix_out, xa_norm_g, mem_norm_g,
           w_xa_q, w_xa_kv, w_xa_o, moe_norm_g, w_group, b_group, w_router, b_router, w_gate, w_up,
           w_down, final_norm_g):
    batch, seq, d = x.shape
    depth = w_in.shape[0]
    t = batch * seq
    h = x.reshape(t, d)
    for l in range(depth):
        proj = _in_proj(h, mix_norm_g[l][None], w_in[l])
        h1 = _mixer(proj, h, conv_w[l], w_conv_out[l].astype(BF16), w_ret_out[l].astype(BF16),
                    w_mix_out[l].astype(BF16), batch, seq)
        kv = _mem_kv(mem, mem_norm_g[l][None], w_xa_kv[l])

        rw = jnp.zeros((LANES, d), F32)
        rw = rw.at[0:N_GROUPS].set(w_group[l].T).at[ROUTE_E0:ROUTE_E0 + N_EXPERTS].set(w_router[l].T)
        rwh = rw.astype(BF16)
        rwl = (rw - rwh.astype(F32)).astype(BF16)
        rw_cat = jnp.concatenate([rwh.T, rwl.T], axis=1)
        rb = jnp.zeros((ROUTE_ROWS,), F32)
        rb = rb.at[0:N_GROUPS].set(b_group[l]).at[ROUTE_E0:ROUTE_E0 + N_EXPERTS].set(b_router[l])
        rb = jnp.broadcast_to(rb[:, None], (ROUTE_ROWS, LANES))

        h2, xn2, rid, rwt = _xattn(h1, kv, xa_norm_g[l][None], w_xa_q[l].astype(BF16),
                                   w_xa_o[l].astype(BF16), moe_norm_g[l][None], rw_cat, rb, batch, seq)

        pos, block_e, next_e, block_valid, n_used, last_blk, has, n_rows = _routing_tables(rid, t)
        xs = _dispatch(last_blk, has, n_used, _tile_pos(pos, DISP_T), xn2, n_rows)
        ys = _experts(block_e, next_e, block_valid, n_used, xs, w_gate[l], w_up[l], w_down[l])
        h = _combine(_tile_pos(pos, COMB_T), h2, rwt, final_norm_g[None], ys,
                     final_norm=(l == depth - 1))
    return h.reshape(batch, seq, d)
```

```python
import functools

import numpy as np
import jax
import jax.numpy as jnp
from jax import lax
from jax.experimental import pallas as pl
from jax.experimental.pallas import tpu as pltpu

F32 = jnp.float32
BF16 = jnp.bfloat16

D_MODEL = 1024
CONV_WIDTH = 512
CONV_K = 3
RET_HEADS = 8
RET_DK = 64
RET_DV = 128
RET_CHUNK = 128
QK_WIDTH = RET_HEADS * RET_DK
V_WIDTH = RET_HEADS * RET_DV
ROPE_BASE = 10000.0
IN_WIDTH = 3 * CONV_WIDTH + 2 * QK_WIDTH + 2 * V_WIDTH + 2 * D_MODEL
OFF_XIN = 0
OFF_BG = OFF_XIN + CONV_WIDTH
OFF_CG = OFF_BG + CONV_WIDTH
OFF_Q = OFF_CG + CONV_WIDTH
OFF_K = OFF_Q + QK_WIDTH
OFF_V = OFF_K + QK_WIDTH
OFF_G = OFF_V + V_WIDTH
OFF_GATE_C = OFF_G + V_WIDTH
OFF_GATE_R = OFF_GATE_C + D_MODEL
MEM_LEN = 256
XA_HEADS = 4
XA_HEAD_DIM = D_MODEL // XA_HEADS
N_GROUPS = 4
EXPERTS_PER_GROUP = 8
N_EXPERTS = N_GROUPS * EXPERTS_PER_GROUP
TOP_K = 2
EXPERT_HIDDEN = D_MODEL // 2
EPS = 1e-6

LANES = 128
SUBLANES = 8
VMEM_LIMIT = 56 * 1024 * 1024

IN_TM = 2048
IN_TN = 1664
MIX_TS = 512
MIX_OUT_ROWS = 256
MIX_STAGE_CHUNKS = 4
XA_TS = 1024
ROUTE_ROWS = 40
ROUTE_E0 = 8
MOE_M = 512
DISP_T = 2048
COMB_T = 256
XS_SLOTS = 3
MOE_SKIP = 128


def _rms(x, g):
    ms = jnp.mean(x * x, axis=-1, keepdims=True)
    return x * lax.rsqrt(ms + EPS) * g


def _sigmoid(x):
    return 1.0 / (1.0 + jnp.exp(-x))


def _dot(a, b):
    return jnp.dot(a, b, preferred_element_type=F32)


def _dot_nt(a, b):
    return lax.dot_general(a, b, (((1,), (1,)), ((), ())), preferred_element_type=F32)


def _dot_tn(a, b):
    return lax.dot_general(a, b, (((0,), (0,)), ((), ())), preferred_element_type=F32)


ROW_TILE = D_MODEL // LANES


def _rows_to_tiles(ref, x):
    n = x.shape[0]
    for c in range(ROW_TILE):
        ref[pl.ds(c, n, stride=ROW_TILE), :] = x[:, c * LANES:(c + 1) * LANES]


def _tiles_to_rows(ref, n):
    return jnp.concatenate([ref[pl.ds(c, n, stride=ROW_TILE), :] for c in range(ROW_TILE)], axis=1)


def _inproj_kernel(x_ref, g_ref, w_ref, o_ref, xn_ref):
    @pl.when(pl.program_id(1) == 0)
    def _():
        xn_ref[...] = _rms(x_ref[...], g_ref[...]).astype(BF16)

    o_ref[...] = _dot(xn_ref[...], w_ref[...].astype(BF16)).astype(BF16)


def _in_proj(x2, g, w_in):
    t = x2.shape[0]
    return pl.pallas_call(
        _inproj_kernel,
        out_shape=jax.ShapeDtypeStruct((t, IN_WIDTH), BF16),
        grid=(t // IN_TM, IN_WIDTH // IN_TN),
        in_specs=[
            pl.BlockSpec((IN_TM, D_MODEL), lambda i, j: (i, 0)),
            pl.BlockSpec((1, D_MODEL), lambda i, j: (0, 0)),
            pl.BlockSpec((D_MODEL, IN_TN), lambda i, j: (0, j)),
        ],
        out_specs=pl.BlockSpec((IN_TM, IN_TN), lambda i, j: (i, j)),
        scratch_shapes=[pltpu.VMEM((IN_TM, D_MODEL), BF16)],
        compiler_params=pltpu.CompilerParams(
            dimension_semantics=("arbitrary", "arbitrary"), vmem_limit_bytes=VMEM_LIMIT),
        name="in_proj",
    )(x2, g, w_in)


def _retention_constants(seq):
    pos = np.arange(seq, dtype=np.float64)
    inv_freq = ROPE_BASE ** (-np.arange(0, RET_DK, 2, dtype=np.float64) / RET_DK)
    ang = pos[:, None] * inv_freq[None, :]
    cos, sin = np.cos(ang), np.sin(ang)
    cos_t = np.concatenate([cos, cos, cos, cos], axis=1)
    sin_t = np.concatenate([-sin, sin, -sin, sin], axis=1)
    log_g = np.log(1.0 - 2.0 ** (-5.0 - np.arange(RET_HEADS, dtype=np.float64)))
    idx = np.arange(RET_CHUNK, dtype=np.float64)
    diff = idx[:, None] - idx[None, :]
    decay = np.where(diff >= 0, np.exp(np.maximum(diff, 0.0)[None] * log_g[:, None, None]), 0.0)
    zeta = np.exp((RET_CHUNK - 1 - idx)[None, :] * log_g[:, None])
    xi = np.exp((idx + 1)[None, :] * log_g[:, None])
    zeta_t = np.repeat(zeta.T, RET_DK, axis=1)
    xi_t = np.repeat(xi.T, RET_DK, axis=1)
    chunk_decay = np.exp(RET_CHUNK * log_g)
    f = lambda a: jnp.asarray(a, dtype=F32)
    return f(cos_t), f(sin_t), f(xi_t), f(zeta_t), f(decay), [float(c) for c in chunk_decay]


def _mixer_kernel(chunk_decay, proj_ref, x_ref, cos_ref, sin_ref, xi_ref, zeta_ref, decay_ref,
                  convw_ref, wc_ref, wr_ref, wm_ref, o_ref, state_ref, tail_ref, yin_ref):
    ts = x_ref.shape[0]

    @pl.when(pl.program_id(1) == 0)
    def _():
        state_ref[...] = jnp.zeros_like(state_ref)
        tail_ref[...] = jnp.zeros_like(tail_ref)

    xin = proj_ref[:, OFF_XIN:OFF_XIN + CONV_WIDTH].astype(F32)
    bg = proj_ref[:, OFF_BG:OFF_BG + CONV_WIDTH].astype(F32)
    cg = proj_ref[:, OFF_CG:OFF_CG + CONV_WIDTH].astype(F32)
    u = cg * xin
    ue = jnp.concatenate([tail_ref[...], u], axis=0)
    u1 = pltpu.roll(ue, 1, 0)[SUBLANES:]
    u2 = pltpu.roll(ue, 2, 0)[SUBLANES:]
    tail_ref[...] = u[ts - SUBLANES:]
    c = convw_ref[2:3, :] * u + convw_ref[1:2, :] * u1 + convw_ref[0:1, :] * u2
    y_conv = _dot((bg * c).astype(BF16), wc_ref[...])

    lane = lax.broadcasted_iota(jnp.int32, (1, LANES), 1)
    low_half = (lane % RET_DK) < (RET_DK // 2)
    head_masks = [(lane // RET_DK) == j for j in range(LANES // RET_DK)]

    def rotary(t, cosv, sinv):
        outs = []
        for p in range(QK_WIDTH // LANES):
            tp = t[:, p * LANES:(p + 1) * LANES]
            fwd = pltpu.roll(tp, LANES - RET_DK // 2, 1)
            bwd = pltpu.roll(tp, RET_DK // 2, 1)
            outs.append(tp * cosv + jnp.where(low_half, fwd, bwd) * sinv)
        return jnp.concatenate(outs, axis=1)

    n_chunks = ts // RET_CHUNK
    chunk_rows = [slice(ci * RET_CHUNK, (ci + 1) * RET_CHUNK) for ci in range(n_chunks)]
    zero = jnp.zeros((), BF16)
    qm, qxm, kb, kz = {}, {}, {}, {}
    for ci, rows in enumerate(chunk_rows):
        cosv = cos_ref[rows, :]
        sinv = sin_ref[rows, :]
        qr = rotary(proj_ref[rows, OFF_Q:OFF_Q + QK_WIDTH].astype(F32), cosv, sinv)
        kr = rotary(proj_ref[rows, OFF_K:OFF_K + QK_WIDTH].astype(F32), cosv, sinv) * (RET_DK ** -0.5)
        qb = qr.astype(BF16)
        qx = (qr * xi_ref[...]).astype(BF16)
        kb[ci] = kr.astype(BF16)
        kz[ci] = (kr * zeta_ref[...]).astype(BF16)
        for h in range(RET_HEADS):
            lanes = slice((h // 2) * LANES, (h // 2 + 1) * LANES)
            qm[ci, h] = jnp.where(head_masks[h % 2], qb[:, lanes], zero)
            qxm[ci, h] = jnp.where(head_masks[h % 2], qx[:, lanes], zero)

    def pair_lanes(h):
        return slice((h // 2) * LANES, (h // 2 + 1) * LANES)

    def v_of(ci, h):
        return proj_ref[chunk_rows[ci], OFF_V + h * RET_DV:OFF_V + (h + 1) * RET_DV]

    state = [state_ref[h] for h in range(RET_HEADS)]
    for c0 in range(0, n_chunks, MIX_STAGE_CHUNKS):
        pairs = [(ci, h) for ci in range(c0, c0 + MIX_STAGE_CHUNKS) for h in range(RET_HEADS)]
        scores = {(ci, h): _dot_nt(qm[ci, h], kb[ci][:, pair_lanes(h)]) for ci, h in pairs}
        kv = {(ci, h): _dot_tn(kz[ci][:, pair_lanes(h)], v_of(ci, h)) for ci, h in pairs}
        probs = {(ci, h): (scores[ci, h] * decay_ref[h]).astype(BF16) for ci, h in pairs}
        inner = {(ci, h): _dot(probs[ci, h], v_of(ci, h)) for ci, h in pairs}
        st_before = {}
        for ci, h in pairs:
            st_before[ci, h] = state[h].astype(BF16)
            state[h] = chunk_decay[h] * state[h] + kv[ci, h]
        o = {(ci, h): inner[ci, h] + _dot(qxm[ci, h], st_before[ci, h]) for ci, h in pairs}
        mu = {p: jnp.mean(o[p], axis=-1, keepdims=True) for p in pairs}
        dlt = {p: o[p] - mu[p] for p in pairs}
        var = {p: jnp.mean(dlt[p] * dlt[p], axis=-1, keepdims=True) for p in pairs}
        for ci, h in pairs:
            g = proj_ref[chunk_rows[ci], OFF_G + h * RET_DV:OFF_G + (h + 1) * RET_DV].astype(F32)
            yin_ref[chunk_rows[ci], h * RET_DV:(h + 1) * RET_DV] = (
                g * _sigmoid(g) * (dlt[ci, h] * lax.rsqrt(var[ci, h] + EPS))).astype(BF16)
    for h in range(RET_HEADS):
        state_ref[h] = state[h]

    for r0 in range(0, ts, MIX_OUT_ROWS):
        rows = slice(r0, r0 + MIX_OUT_ROWS)
        y_ret = _dot(yin_ref[rows, :], wr_ref[...])
        gate_c = proj_ref[rows, OFF_GATE_C:OFF_GATE_C + D_MODEL].astype(F32)
        gate_r = proj_ref[rows, OFF_GATE_R:OFF_GATE_R + D_MODEL].astype(F32)
        merged = _sigmoid(gate_c) * y_conv[rows] + _sigmoid(gate_r) * y_ret
        o_ref[rows, :] = x_ref[rows, :] + _dot(merged.astype(BF16), wm_ref[...])


def _mixer(proj, x2, conv_w, wc_bf, wr_bf, wm_bf, batch, seq):
    ts = MIX_TS
    ns = seq // ts
    cos_t, sin_t, xi_t, zeta_t, decay, chunk_decay = _retention_constants(seq)
    full = lambda shape: pl.BlockSpec(shape, lambda b, s: (0,) * len(shape))
    return pl.pallas_call(
        functools.partial(_mixer_kernel, chunk_decay),
        out_shape=jax.ShapeDtypeStruct((batch * seq, D_MODEL), F32),
        grid=(batch, ns),
        in_specs=[
            pl.BlockSpec((ts, IN_WIDTH), lambda b, s: (b * ns + s, 0)),
            pl.BlockSpec((ts, D_MODEL), lambda b, s: (b * ns + s, 0)),
            pl.BlockSpec((ts, LANES), lambda b, s: (s, 0)),
            pl.BlockSpec((ts, LANES), lambda b, s: (s, 0)),
            full((RET_CHUNK, QK_WIDTH)),
            full((RET_CHUNK, QK_WIDTH)),
            full((RET_HEADS, RET_CHUNK, RET_CHUNK)),
            full((CONV_K, CONV_WIDTH)),
            full((CONV_WIDTH, D_MODEL)),
            full((V_WIDTH, D_MODEL)),
            full((D_MODEL, D_MODEL)),
        ],
        out_specs=pl.BlockSpec((ts, D_MODEL), lambda b, s: (b * ns + s, 0)),
        scratch_shapes=[
            pltpu.VMEM((RET_HEADS, LANES, RET_DV), F32),
            pltpu.VMEM((SUBLANES, CONV_WIDTH), F32),
            pltpu.VMEM((ts, V_WIDTH), BF16),
        ],
        compiler_params=pltpu.CompilerParams(
            dimension_semantics=("arbitrary", "arbitrary"), vmem_limit_bytes=VMEM_LIMIT),
        name="mixer",
    )(proj, x2, cos_t, sin_t, xi_t, zeta_t, decay, conv_w, wc_bf, wr_bf, wm_bf)


def _memkv_kernel(m_ref, g_ref, w_ref, o_ref):
    o_ref[0] = _dot(_rms(m_ref[0], g_ref[...]).astype(BF16), w_ref[...].astype(BF16)).astype(BF16)


def _mem_kv(mem, g, w_kv):
    b = mem.shape[0]
    return pl.pallas_call(
        _memkv_kernel,
        out_shape=jax.ShapeDtypeStruct((b, MEM_LEN, 2 * D_MODEL), BF16),
        grid=(b,),
        in_specs=[
            pl.BlockSpec((1, MEM_LEN, D_MODEL), lambda i: (i, 0, 0)),
            pl.BlockSpec((1, D_MODEL), lambda i: (0, 0)),
            pl.BlockSpec((D_MODEL, 2 * D_MODEL), lambda i: (0, 0)),
        ],
        out_specs=pl.BlockSpec((1, MEM_LEN, 2 * D_MODEL), lambda i: (i, 0, 0)),
        compiler_params=pltpu.CompilerParams(
            dimension_semantics=("arbitrary",), vmem_limit_bytes=VMEM_LIMIT),
        name="mem_kv",
    )(mem, g, w_kv)


def _xattn_kernel(h_ref, kv_ref, gx_ref, wq_ref, wo_ref, gm_ref, rw_ref, rb_ref,
                  h2_ref, xn2_ref, rid_ref, rwt_ref):
    ts = h_ref.shape[0]
    h = h_ref[...]
    q = _dot(_rms(h, gx_ref[...]).astype(BF16), wq_ref[...])
    qb = q.astype(BF16)
    head_cols = [slice(hd * XA_HEAD_DIM, (hd + 1) * XA_HEAD_DIM) for hd in range(XA_HEADS)]
    scores = [_dot_nt(qb[:, cols], kv_ref[0, :, cols]) * (XA_HEAD_DIM ** -0.5) for cols in head_cols]
    probs = []
    for s in scores:
        e = jnp.exp(s - jnp.max(s, axis=-1, keepdims=True))
        probs.append((e / jnp.sum(e, axis=-1, keepdims=True)).astype(BF16))
    outs = [_dot(p, kv_ref[0, :, D_MODEL + hd * XA_HEAD_DIM:D_MODEL + (hd + 1) * XA_HEAD_DIM])
            for hd, p in enumerate(probs)]
    h2 = h + _dot(jnp.concatenate(outs, axis=1).astype(BF16), wo_ref[...])
    h2_ref[...] = h2
    xn = _rms(h2, gm_ref[...])
    _rows_to_tiles(xn2_ref, xn)

    hi = xn.astype(BF16)
    lo = (xn - hi.astype(F32)).astype(BF16)
    hi_prod = _dot(hi, rw_ref[...])
    lo_prod = _dot(lo, rw_ref[:, 0:LANES])
    lg_t = (hi_prod[:, 0:LANES] + lo_prod) + hi_prod[:, LANES:2 * LANES]
    lg = lg_t.T[0:ROUTE_ROWS] + rb_ref[:, 0:1]

    gl = lg[0:N_GROUPS]
    ge = jnp.exp(gl - jnp.max(gl, axis=0, keepdims=True))
    gp = ge / jnp.sum(ge, axis=0, keepdims=True)
    p_g = gp[0:1]
    g_idx = jnp.zeros((1, ts), jnp.int32)
    for i in range(1, N_GROUPS):
        better = gp[i:i + 1] > p_g
        g_idx = jnp.where(better, i, g_idx)
        p_g = jnp.where(better, gp[i:i + 1], p_g)
    sel = jnp.zeros((EXPERTS_PER_GROUP, ts), F32)
    for i in range(N_GROUPS):
        r0 = ROUTE_E0 + i * EXPERTS_PER_GROUP
        sel = jnp.where(g_idx == i, lg[r0:r0 + EXPERTS_PER_GROUP], sel)
    se = jnp.exp(sel - jnp.max(sel, axis=0, keepdims=True))
    sp = se / jnp.sum(se, axis=0, keepdims=True)
    ridx = lax.broadcasted_iota(jnp.int32, (EXPERTS_PER_GROUP, ts), 0)
    m1 = jnp.max(sp, axis=0, keepdims=True)
    i1 = jnp.min(jnp.where(sp == m1, ridx, EXPERTS_PER_GROUP), axis=0, keepdims=True)
    sp2 = jnp.where(ridx == i1, -1.0, sp)
    m2 = jnp.max(sp2, axis=0, keepdims=True)
    i2 = jnp.min(jnp.where(sp2 == m2, ridx, EXPERTS_PER_GROUP), axis=0, keepdims=True)
    den = m1 + m2
    rid_ref[...] = jnp.zeros_like(rid_ref)
    rwt_ref[...] = jnp.zeros_like(rwt_ref)
    rid_ref[0:1, :] = g_idx * EXPERTS_PER_GROUP + i1
    rid_ref[1:2, :] = g_idx * EXPERTS_PER_GROUP + i2
    rwt_ref[0:1, :] = p_g * m1 / den
    rwt_ref[1:2, :] = p_g * m2 / den


def _xattn(h1, kv, gx, wq_bf, wo_bf, gm, rw_cat, rb, batch, seq):
    ts = XA_TS
    ns = seq // ts
    t = batch * seq
    full = lambda shape: pl.BlockSpec(shape, lambda b, s: (0,) * len(shape))
    return pl.pallas_call(
        _xattn_kernel,
        out_shape=(
            jax.ShapeDtypeStruct((t, D_MODEL), F32),
            jax.ShapeDtypeStruct((t * ROW_TILE, LANES), F32),
            jax.ShapeDtypeStruct((SUBLANES, t), jnp.int32),
            jax.ShapeDtypeStruct((SUBLANES, t), F32),
        ),
        grid=(batch, ns),
        in_specs=[
            pl.BlockSpec((ts, D_MODEL), lambda b, s: (b * ns + s, 0)),
            pl.BlockSpec((1, MEM_LEN, 2 * D_MODEL), lambda b, s: (b, 0, 0)),
            full((1, D_MODEL)),
            full((D_MODEL, D_MODEL)),
            full((D_MODEL, D_MODEL)),
            full((1, D_MODEL)),
            full((D_MODEL, 2 * LANES)),
            full((ROUTE_ROWS, LANES)),
        ],
        out_specs=(
            pl.BlockSpec((ts, D_MODEL), lambda b, s: (b * ns + s, 0)),
            pl.BlockSpec((ts * ROW_TILE, LANES), lambda b, s: (b * ns + s, 0)),
            pl.BlockSpec((SUBLANES, ts), lambda b, s: (0, b * ns + s)),
            pl.BlockSpec((SUBLANES, ts), lambda b, s: (0, b * ns + s)),
        ),
        compiler_params=pltpu.CompilerParams(
            dimension_semantics=("arbitrary", "arbitrary"), vmem_limit_bytes=VMEM_LIMIT),
        name="xattn",
    )(h1, kv, gx, wq_bf, wo_bf, gm, rw_cat, rb)


def _tile_of(ref, row):
    start = row * ROW_TILE
    if not isinstance(start, int):
        start = pl.multiple_of(start, ROW_TILE)
    return ref.at[pl.ds(start, ROW_TILE)]


def _row_copy(src_ref, src_row, dst_ref, dst_row, sem):
    return pltpu.make_async_copy(_tile_of(src_ref, src_row), _tile_of(dst_ref, dst_row), sem)


def _dispatch_kernel(last_ref, has_ref, nu_ref, pos0_ref, pos1_ref, xn_ref, xs_ref, zero_ref, sem, zsem):
    pos_refs = (pos0_ref, pos1_ref)
    td = xn_ref.shape[0] // ROW_TILE
    blk = MOE_M * ROW_TILE
    n_blocks = xs_ref.shape[0] // blk

    def fill(start):
        start = pl.multiple_of(start * ROW_TILE, blk)
        return pltpu.make_async_copy(zero_ref, xs_ref.at[pl.ds(start, blk)], zsem)

    @pl.when(pl.program_id(0) == 0)
    def _():
        zero_ref[...] = jnp.zeros_like(zero_ref)

        for e in range(N_EXPERTS):
            @pl.when(has_ref[e] > 0)
            def _():
                fill(last_ref[e]).start()
        for e in range(N_EXPERTS):
            @pl.when(has_ref[e] > 0)
            def _():
                fill(last_ref[e]).wait()

        def start_tail(b, carry):
            fill(b * MOE_M).start()
            return carry

        lax.fori_loop(nu_ref[0], n_blocks, start_tail, 0)

    for r in range(td):
        for k in range(TOP_K):
            _row_copy(xn_ref, r, xs_ref, pos_refs[k][0, 0, r], sem).start(priority=k)
    for k in range(TOP_K):
        pltpu.make_async_copy(xn_ref, xs_ref.at[pl.ds(0, td * ROW_TILE)], sem).wait()

    @pl.when(pl.program_id(0) == pl.num_programs(0) - 1)
    def _():
        def wait_tail(b, carry):
            fill(b * MOE_M).wait()
            return carry

        lax.fori_loop(nu_ref[0], n_blocks, wait_tail, 0)


def _dispatch(last_blk, has, n_used, pos3, xn2, n_rows):
    t = xn2.shape[0] // ROW_TILE
    td = DISP_T
    return pl.pallas_call(
        _dispatch_kernel,
        out_shape=jax.ShapeDtypeStruct((n_rows * ROW_TILE, LANES), F32),
        grid_spec=pltpu.PrefetchScalarGridSpec(
            num_scalar_prefetch=3,
            grid=(t // td,),
            in_specs=[
                pl.BlockSpec((1, 1, td), lambda i, *_: (i, 0, 0), memory_space=pltpu.SMEM),
                pl.BlockSpec((1, 1, td), lambda i, *_: (t // td + i, 0, 0), memory_space=pltpu.SMEM),
                pl.BlockSpec((td * ROW_TILE, LANES), lambda i, *_: (i, 0)),
            ],
            out_specs=pl.BlockSpec(memory_space=pl.ANY),
            scratch_shapes=[
                pltpu.VMEM((MOE_M * ROW_TILE, LANES), F32),
                pltpu.SemaphoreType.DMA,
                pltpu.SemaphoreType.DMA,
            ],
        ),
        compiler_params=pltpu.CompilerParams(
            dimension_semantics=("arbitrary",), vmem_limit_bytes=VMEM_LIMIT),
        name="dispatch",
    )(last_blk, has, n_used, pos3, pos3, xn2)


def _expert_kernel(be_ref, ne_ref, bv_ref, nu_ref, xs_ref, wg_ref, wu_ref, wd_ref, ys_ref,
                   wgf, wuf, wdf, wgb, wub, wdb, wsem, xb0, xb1, xb2, xsem):
    xbufs = (xb0, xb1, xb2)
    assert len(xbufs) == XS_SLOTS
    i = pl.program_id(0)
    e = be_ref[i]

    def fetch(expert):
        return [pltpu.make_async_copy(src.at[expert], dst, wsem)
                for src, dst in ((wg_ref, wgf), (wu_ref, wuf), (wd_ref, wdf))]

    @pl.when(i == 0)
    def _():
        for c in fetch(e):
            c.start()

    @pl.when((i == 0) | (e != be_ref[jnp.maximum(i - 1, 0)]))
    def _():
        for c in fetch(e):
            c.wait()
        wgb[...] = wgf[...].astype(BF16)
        wub[...] = wuf[...].astype(BF16)
        wdb[...] = wdf[...].astype(BF16)

        @pl.when(ne_ref[i] != e)
        def _():
            for c in fetch(ne_ref[i]):
                c.start()

    nu = nu_ref[0]
    blk = MOE_M * ROW_TILE

    def rows_in(b, s):
        start = b * blk if isinstance(b, int) else pl.multiple_of(b * blk, blk)
        return pltpu.make_async_copy(xs_ref.at[pl.ds(start, blk)], xbufs[s], xsem.at[s])

    @pl.when(i == 0)
    def _():
        for b in range(XS_SLOTS - 1):
            @pl.when(b < nu)
            def _():
                rows_in(b, b).start()

    for s in range(XS_SLOTS):
        @pl.when((i < nu) & (lax.rem(i, XS_SLOTS) == s))
        def _():
            @pl.when(i + (XS_SLOTS - 1) < nu)
            def _():
                rows_in(i + (XS_SLOTS - 1), (s + XS_SLOTS - 1) % XS_SLOTS).start()

            rows_in(i, s).wait()

            def mlp(n_rows):
                x = _tiles_to_rows(xbufs[s], n_rows).astype(BF16)
                g = _dot(x, wgb[...])
                u = _dot(x, wub[...])
                _rows_to_tiles(ys_ref, _dot((g * _sigmoid(g) * u).astype(BF16), wdb[...]))

            for n_rows in range(MOE_SKIP, MOE_M + 1, MOE_SKIP):
                lo = n_rows - MOE_SKIP if n_rows > MOE_SKIP else -1

                @pl.when((bv_ref[i] > lo) & (bv_ref[i] <= n_rows))
                def _():
                    mlp(n_rows)
                    if n_rows < MOE_M:
                        rest = (MOE_M - n_rows) * ROW_TILE
                        ys_ref[pl.ds(n_rows * ROW_TILE, rest), :] = jnp.zeros((rest, LANES), F32)

    @pl.when(i >= nu)
    def _():
        ys_ref[...] = jnp.zeros_like(ys_ref)


def _experts(block_e, next_e, block_valid, n_used, xs, w_gate, w_up, w_down):
    n_rows = xs.shape[0] // ROW_TILE
    out_map = lambda i, be, ne, bv, nu: (i, 0)
    return pl.pallas_call(
        _expert_kernel,
        out_shape=jax.ShapeDtypeStruct((n_rows * ROW_TILE, LANES), F32),
        grid_spec=pltpu.PrefetchScalarGridSpec(
            num_scalar_prefetch=4,
            grid=(n_rows // MOE_M,),
            in_specs=[
                pl.BlockSpec(memory_space=pl.ANY),
                pl.BlockSpec(memory_space=pl.ANY),
                pl.BlockSpec(memory_space=pl.ANY),
                pl.BlockSpec(memory_space=pl.ANY),
            ],
            out_specs=pl.BlockSpec((MOE_M * ROW_TILE, LANES), out_map),
            scratch_shapes=[
                pltpu.VMEM((D_MODEL, EXPERT_HIDDEN), F32),
                pltpu.VMEM((D_MODEL, EXPERT_HIDDEN), F32),
                pltpu.VMEM((EXPERT_HIDDEN, D_MODEL), F32),
                pltpu.VMEM((D_MODEL, EXPERT_HIDDEN), BF16),
                pltpu.VMEM((D_MODEL, EXPERT_HIDDEN), BF16),
                pltpu.VMEM((EXPERT_HIDDEN, D_MODEL), BF16),
                pltpu.SemaphoreType.DMA,
                pltpu.VMEM((MOE_M * ROW_TILE, LANES), F32),
                pltpu.VMEM((MOE_M * ROW_TILE, LANES), F32),
                pltpu.VMEM((MOE_M * ROW_TILE, LANES), F32),
                pltpu.SemaphoreType.DMA((XS_SLOTS,)),
            ],
        ),
        compiler_params=pltpu.CompilerParams(
            dimension_semantics=("arbitrary",), vmem_limit_bytes=VMEM_LIMIT),
        name="experts",
    )(block_e, next_e, block_valid, n_used, xs, w_gate, w_up, w_down)


def _combine_kernel(final_norm, pos0_ref, pos1_ref, nxt0_ref, nxt1_ref, h2_ref, wt_ref, g_ref, ys_ref,
                    o_ref, b00_ref, b01_ref, b10_ref, b11_ref, sem):
    i = pl.program_id(0)
    n_steps = pl.num_programs(0)
    tc = h2_ref.shape[0]
    bufs = ((b00_ref, b01_ref), (b10_ref, b11_ref))

    def issue(rows_refs, s):
        for r in range(tc):
            for k in range(TOP_K):
                _row_copy(ys_ref, rows_refs[k][0, 0, r], bufs[s][k], r, sem.at[s]).start(priority=k)

    @pl.when(i == 0)
    def _():
        issue((pos0_ref, pos1_ref), 0)

    for s in range(2):
        @pl.when(i % 2 == s)
        def _():
            @pl.when(i + 1 < n_steps)
            def _():
                issue((nxt0_ref, nxt1_ref), 1 - s)

            for k in range(TOP_K):
                pltpu.make_async_copy(ys_ref.at[pl.ds(0, tc * ROW_TILE)], bufs[s][k], sem.at[s]).wait()
            y = (h2_ref[...] + wt_ref[:, 0:1] * _tiles_to_rows(bufs[s][0], tc)
                 + wt_ref[:, 1:2] * _tiles_to_rows(bufs[s][1], tc))
            o_ref[...] = _rms(y, g_ref[...]) if final_norm else y


def _combine(pos3, h2, wt, g, ys, final_norm):
    t = h2.shape[0]
    tc = COMB_T
    nt = t // tc
    return pl.pallas_call(
        functools.partial(_combine_kernel, final_norm),
        out_shape=jax.ShapeDtypeStruct((t, D_MODEL), F32),
        grid=(nt,),
        in_specs=[
            pl.BlockSpec((1, 1, tc), lambda i: (i, 0, 0), memory_space=pltpu.SMEM),
            pl.BlockSpec((1, 1, tc), lambda i: (nt + i, 0, 0), memory_space=pltpu.SMEM),
            pl.BlockSpec((1, 1, tc), lambda i: (jnp.minimum(i + 1, nt - 1), 0, 0), memory_space=pltpu.SMEM),
            pl.BlockSpec((1, 1, tc), lambda i: (nt + jnp.minimum(i + 1, nt - 1), 0, 0),
                         memory_space=pltpu.SMEM),
            pl.BlockSpec((tc, D_MODEL), lambda i: (i, 0)),
            pl.BlockSpec((tc, TOP_K), lambda i: (i, 0)),
            pl.BlockSpec((1, D_MODEL), lambda i: (0, 0)),
            pl.BlockSpec(memory_space=pl.ANY),
        ],
        out_specs=pl.BlockSpec((tc, D_MODEL), lambda i: (i, 0)),
        scratch_shapes=[
            pltpu.VMEM((tc * ROW_TILE, LANES), F32),
            pltpu.VMEM((tc * ROW_TILE, LANES), F32),
            pltpu.VMEM((tc * ROW_TILE, LANES), F32),
            pltpu.VMEM((tc * ROW_TILE, LANES), F32),
            pltpu.SemaphoreType.DMA((2,)),
        ],
        compiler_params=pltpu.CompilerParams(
            dimension_semantics=("arbitrary",), vmem_limit_bytes=VMEM_LIMIT),
        name="combine",
    )(pos3, pos3, pos3, pos3, h2, wt, g, ys)


def _routing_tables(rid, n_tok):
    e_flat = rid[:TOP_K].reshape(-1)
    onehot = (e_flat[None, :] == jnp.arange(N_EXPERTS, dtype=jnp.int32)[:, None]).astype(jnp.int32)
    csum = jnp.cumsum(onehot, axis=1)
    counts = csum[:, -1]
    rank = jnp.sum(onehot * csum, axis=0) - 1
    padded = (counts + MOE_M - 1) // MOE_M * MOE_M
    pend = jnp.cumsum(padded)
    pstart = pend - padded
    pos = jnp.sum(onehot * pstart[:, None], axis=0) + rank
    n_blocks = (TOP_K * n_tok + N_EXPERTS * (MOE_M - 1) + MOE_M - 1) // MOE_M
    n_used = pend[-1] // MOE_M
    blk = jnp.minimum(jnp.arange(n_blocks, dtype=jnp.int32), n_used - 1)
    block_e = jnp.sum((pend[None, :] <= (blk * MOE_M)[:, None]).astype(jnp.int32), axis=1)
    block_e = jnp.minimum(block_e, N_EXPERTS - 1)
    last_blk = jnp.maximum(pend - MOE_M, 0).astype(jnp.int32)
    has = (counts > 0).astype(jnp.int32)
    ids = jnp.arange(N_EXPERTS, dtype=jnp.int32)
    later = (ids[None, :] > ids[:, None]) & (counts[None, :] > 0)
    nxt = jnp.min(jnp.where(later, ids[None, :], N_EXPERTS), axis=1)
    nxt = jnp.where(nxt == N_EXPERTS, ids, nxt)
    next_e = jnp.sum(jnp.where(block_e[:, None] == ids[None, :], nxt[None, :], 0), axis=1)
    seg_end = jnp.sum(jnp.where(block_e[:, None] == ids[None, :], (pstart + counts)[None, :], 0), axis=1)
    block_valid = jnp.clip(seg_end - blk * MOE_M, 0, MOE_M)
    return (pos.reshape(TOP_K, n_tok).astype(jnp.int32), block_e.astype(jnp.int32),
            next_e.astype(jnp.int32), block_valid.astype(jnp.int32), n_used.reshape(1).astype(jnp.int32),
            last_blk, has, n_blocks * MOE_M)


def _tile_pos(pos, tile):
    k, t = pos.shape
    return pos.reshape(k * (t // tile), 1, tile)


def kernel(x, mem, mix_norm_g, w_in, conv_w, w_conv_out, w_ret_out, w_mix_out, xa_norm_g, mem_norm_g,
           w_xa_q, w_xa_kv, w_xa_o, moe_norm_g, w_group, b_group, w_router, b_router, w_gate, w_up,
           w_down, final_norm_g):
    batch, seq, d = x.shape
    depth = w_in.shape[0]
    t = batch * seq
    h = x.reshape(t, d)
    for l in range(depth):
        proj = _in_proj(h, mix_norm_g[l][None], w_in[l])
        h1 = _mixer(proj, h, conv_w[l], w_conv_out[l].astype(BF16), w_ret_out[l].astype(BF16),
                    w_mix_out[l].astype(BF16), batch, seq)
        kv = _mem_kv(mem, mem_norm_g[l][None], w_xa_kv[l])

        rw = jnp.zeros((LANES, d), F32)
        rw = rw.at[0:N_GROUPS].set(w_group[l].T).at[ROUTE_E0:ROUTE_E0 + N_EXPERTS].set(w_router[l].T)
        rwh = rw.astype(BF16)
        rwl = (rw - rwh.astype(F32)).astype(BF16)
        rw_cat = jnp.concatenate([rwh.T, rwl.T], axis=1)
        rb = jnp.zeros((ROUTE_ROWS,), F32)
        rb = rb.at[0:N_GROUPS].set(b_group[l]).at[ROUTE_E0:ROUTE_E0 + N_EXPERTS].set(b_router[l])
        rb = jnp.broadcast_to(rb[:, None], (ROUTE_ROWS, LANES))

        h2, xn2, rid, rwt = _xattn(h1, kv, xa_norm_g[l][None], w_xa_q[l].astype(BF16),
                                   w_xa_o[l].astype(BF16), moe_norm_g[l][None], rw_cat, rb, batch, seq)

        pos, block_e, next_e, block_valid, n_used, last_blk, has, n_rows = _routing_tables(rid, t)
        xs = _dispatch(last_blk, has, n_used, _tile_pos(pos, DISP_T), xn2, n_rows)
        ys = _experts(block_e, next_e, block_valid, n_used, xs, w_gate[l], w_up[l], w_down[l])
        h = _combine(_tile_pos(pos, COMB_T), h2, rwt[:TOP_K].T, final_norm_g[None], ys,
                     final_norm=(l == depth - 1))
    return h.reshape(batch, seq, d)
```
